```python
import jax, jax.numpy as jnp
from jax import lax
import numpy as np


D_MODEL = 1024
BATCH = 2
SEQ = 16384
DEPTH = 4

HEAD_DIM = 32
SB_HEADS = 4
NSA_HEADS = 4
MOBA_HEADS = 4
Q_BLOCK = 128
SB_KTILE = 64
NSA_CMP_LEN = 32
NSA_CMP_STRIDE = 16
NSA_SEL_LEN = 64
NSA_TOP_N = 8
NSA_WINDOW = 512
MOBA_BLOCK = 256
MOBA_TOPK = 3
PEER_HEADS = 4
PEER_NKEYS = 128
PEER_N_EXPERTS = PEER_NKEYS * PEER_NKEYS
PEER_TOPK = 8
PEER_QDIM = 256
TOK_BLOCK = 128

N_BRANCHES = 3
SB_W = SB_HEADS * HEAD_DIM
NSA_W = NSA_HEADS * HEAD_DIM
MOBA_W = MOBA_HEADS * HEAD_DIM
IN_WIDTHS = (SB_W, SB_W, SB_W, NSA_W, HEAD_DIM, HEAD_DIM, HEAD_DIM, HEAD_DIM, HEAD_DIM, HEAD_DIM, 3 * NSA_HEADS, MOBA_W, HEAD_DIM, HEAD_DIM, N_BRANCHES * D_MODEL)
IN_WIDTH = sum(IN_WIDTHS)
DEEPNORM_ALPHA = (2.0 * DEPTH) ** 0.25
DEEPNORM_BETA = (8.0 * DEPTH) ** -0.25
LN_EPS = 1e-5

kernel_name = 'hybrid_sb_nsa_moba_peer_deepnorm'


def layer_norm(x, g, b):
    xf = x.astype(jnp.float32)
    mu = jnp.mean(xf, axis=-1, keepdims=True)
    var = jnp.mean(jnp.square(xf - mu), axis=-1, keepdims=True)
    return ((xf - mu) * lax.rsqrt(var + LN_EPS) * g + b).astype(x.dtype)


def masked_softmax(z, mask):
    z = jnp.where(mask, z.astype(jnp.float32), -jnp.inf)
    m = jnp.max(z, axis=-1, keepdims=True)
    m = jnp.where(jnp.isfinite(m), m, 0.0)
    p = jnp.exp(z - m)
    return p / jnp.maximum(jnp.sum(p, axis=-1, keepdims=True), 1e-30)


def alibi_slopes(n):
    return jnp.asarray([2.0 ** (-8.0 * (i + 1) / n) for i in range(n)], jnp.float32)


def stick_breaking_attention(q, k, v):
    B, H, S, dh = q.shape
    q = q * (dh ** -0.5)
    nkt = S // SB_KTILE
    kt = k.reshape(B, H, nkt, SB_KTILE, dh)
    vt = v.reshape(B, H, nkt, SB_KTILE, dh)
    tri = jnp.asarray(np.tril(np.ones((SB_KTILE, SB_KTILE), np.float32), k=-1))
    outs = []
    for i in range(S // Q_BLOCK):
        q0 = i * Q_BLOCK
        n = (q0 + Q_BLOCK) // SB_KTILE
        qb = q[:, :, q0:q0 + Q_BLOCK]
        z = jnp.einsum('bhqd,bhnkd->bhqnk', qb, kt[:, :, :n]).astype(jnp.float32)
        kpos = jnp.arange(n * SB_KTILE).reshape(n, SB_KTILE)
        qpos = q0 + jnp.arange(Q_BLOCK)
        past = kpos[None] < qpos[:, None, None]
        ls = jnp.where(past, -jax.nn.softplus(z), 0.0)
        tot = jnp.sum(ls, axis=-1)
        carry = lax.cumsum(tot, axis=3, reverse=True) - tot
        excl = jnp.einsum('bhqnj,js->bhqns', ls, tri) + carry[..., None]
        w = jnp.where(past, jnp.exp(z + ls + excl), 0.0)
        outs.append(jnp.einsum('bhqnk,bhnkd->bqhd', w.astype(v.dtype), vt[:, :, :n]))
    return jnp.concatenate(outs, axis=1).reshape(B, S, H * dh)


def nsa_attention(q, k_cmp, v_cmp, k_slc, v_slc, k_win, v_win, gates, pe_k, pe_v, w_ck, w_cv, slopes):
    B, S, H, dh = q.shape
    scale = dh ** -0.5
    nchunk = S // NSA_CMP_STRIDE
    nc = nchunk - 1

    def compress(t, pe, w):
        c = t.reshape(B, nchunk, NSA_CMP_STRIDE, dh)
        blocks = jnp.concatenate([c[:, :-1], c[:, 1:]], axis=2) + pe
        return jnp.einsum('bnf,fd->bnd', blocks.reshape(B, nc, NSA_CMP_LEN * dh), w)

    kc = compress(k_cmp, pe_k, w_ck)
    vc = compress(v_cmp, pe_v, w_cv)
    cend = jnp.arange(nc) * NSA_CMP_STRIDE + (NSA_CMP_LEN - 1)
    nsel = S // NSA_SEL_LEN
    top_n = min(NSA_TOP_N, nsel)
    ratio = NSA_SEL_LEN // NSA_CMP_STRIDE
    overlap = np.convolve(np.ones(ratio), np.ones(NSA_CMP_LEN // NSA_CMP_STRIDE)).tolist()
    n_left = (NSA_CMP_LEN - NSA_CMP_STRIDE) // NSA_CMP_STRIDE
    n_right = len(overlap) + ratio * (nsel - 1) - nc - n_left
    ks_blk = k_slc.reshape(B, nsel, NSA_SEL_LEN, dh)
    vs_blk = v_slc.reshape(B, nsel, NSA_SEL_LEN, dh)
    kw_pad = jnp.pad(k_win, ((0, 0), (NSA_WINDOW, 0), (0, 0)))
    vw_pad = jnp.pad(v_win, ((0, 0), (NSA_WINDOW, 0), (0, 0)))
    blk = jnp.arange(nsel)
    bidx = jnp.arange(B)[:, None, None]
    lpos = jnp.arange(NSA_SEL_LEN)
    nkeys_sel = top_n * NSA_SEL_LEN

    def block(i):
        q0 = i * Q_BLOCK
        qb = lax.dynamic_slice_in_dim(q, q0, Q_BLOCK, axis=1)
        qpos = q0 + jnp.arange(Q_BLOCK)
        dc = qpos[:, None] - cend[None, :]
        zc = jnp.einsum('bqhd,bnd->bhqn', qb, kc).astype(jnp.float32) * scale - slopes[:, None, None] * dc
        pc = masked_softmax(zc, dc >= 0)
        o_cmp = jnp.einsum('bhqn,bnd->bqhd', pc.astype(vc.dtype), vc)
        pg = jnp.pad(jnp.sum(pc, axis=1), ((0, 0), (0, 0), (n_left, n_right)))
        imp = sum(c * pg[..., o:o + ratio * nsel:ratio] for o, c in enumerate(overlap))
        cur = qpos // NSA_SEL_LEN
        forced = (blk[None, :] == 0) | (blk[None, :] == cur[:, None]) | (blk[None, :] == cur[:, None] - 1)
        future = blk[None, :] > cur[:, None]
        imp = jnp.where(future, -jnp.inf, jnp.where(forced, jnp.inf, imp))
        _, sel = lax.top_k(imp, top_n)
        ksel = ks_blk[bidx, sel]
        vsel = vs_blk[bidx, sel]
        spos = sel[..., None] * NSA_SEL_LEN + lpos
        ds = qpos[None, None, :, None, None] - spos[:, None]
        zs = jnp.einsum('bqhd,bqnld->bhqnl', qb, ksel).astype(jnp.float32) * scale - slopes[:, None, None, None] * ds
        ps = masked_softmax(zs.reshape(B, H, Q_BLOCK, nkeys_sel), (ds >= 0).reshape(B, 1, Q_BLOCK, nkeys_sel))
        o_slc = jnp.einsum('bhqnl,bqnld->bqhd', ps.reshape(B, H, Q_BLOCK, top_n, NSA_SEL_LEN).astype(vsel.dtype), vsel)
        kw = lax.dynamic_slice_in_dim(kw_pad, q0, Q_BLOCK + NSA_WINDOW, axis=1)
        vw = lax.dynamic_slice_in_dim(vw_pad, q0, Q_BLOCK + NSA_WINDOW, axis=1)
        wpos = q0 - NSA_WINDOW + jnp.arange(Q_BLOCK + NSA_WINDOW)
        dw = qpos[:, None] - wpos[None, :]
        mw = (dw >= 0) & (dw < NSA_WINDOW) & (wpos[None, :] >= 0)
        zw = jnp.einsum('bqhd,bkd->bhqk', qb, kw).astype(jnp.float32) * scale - slopes[:, None, None] * dw
        o_win = jnp.einsum('bhqk,bkd->bqhd', masked_softmax(zw, mw).astype(vw.dtype), vw)
        g = lax.dynamic_slice_in_dim(gates, q0, Q_BLOCK, axis=1)
        return g[..., 0:1] * o_cmp + g[..., 1:2] * o_slc + g[..., 2:3] * o_win

    out = lax.map(block, jnp.arange(S // Q_BLOCK))
    return jnp.swapaxes(out, 0, 1).reshape(B, S, H * dh)


def moba_attention(q, k, v, slopes):
    B, S, H, dh = q.shape
    scale = dh ** -0.5
    nb = -(-S // MOBA_BLOCK)
    pad = nb * MOBA_BLOCK - S
    kp = jnp.pad(k, ((0, 0), (0, pad), (0, 0)))
    vp = jnp.pad(v, ((0, 0), (0, pad), (0, 0)))
    k_blk = kp.reshape(B, nb, MOBA_BLOCK, dh)
    v_blk = vp.reshape(B, nb, MOBA_BLOCK, dh)
    k_mean = jnp.mean(k_blk, axis=2)
    top_k = min(MOBA_TOPK, nb)
    bidx = jnp.arange(B)[:, None, None]
    blk = jnp.arange(nb)
    lpos = jnp.arange(MOBA_BLOCK)
    nkeys_sel = top_k * MOBA_BLOCK

    def block(i):
        q0 = i * Q_BLOCK
        qb = lax.dynamic_slice_in_dim(q, q0, Q_BLOCK, axis=1)
        qpos = q0 + jnp.arange(Q_BLOCK)
        cur = q0 // MOBA_BLOCK
        sg = jnp.einsum('bqhd,bnd->bqn', qb, k_mean).astype(jnp.float32)
        sg = jnp.where(blk < cur, sg, -jnp.inf)
        gv, sel = lax.top_k(sg, top_k)
        ksel = k_blk[bidx, sel]
        vsel = v_blk[bidx, sel]
        spos = sel[..., None] * MOBA_BLOCK + lpos
        ds = qpos[None, None, :, None, None] - spos[:, None]
        zs = jnp.einsum('bqhd,bqnld->bhqnl', qb, ksel).astype(jnp.float32) * scale - slopes[:, None, None, None] * ds
        ms = jnp.broadcast_to(jnp.isfinite(gv)[:, None, :, :, None], zs.shape)
        ko = lax.dynamic_slice_in_dim(kp, cur * MOBA_BLOCK, MOBA_BLOCK, axis=1)
        vo = lax.dynamic_slice_in_dim(vp, cur * MOBA_BLOCK, MOBA_BLOCK, axis=1)
        do = qpos[:, None] - (cur * MOBA_BLOCK + lpos)[None, :]
        zo = jnp.einsum('bqhd,bld->bhql', qb, ko).astype(jnp.float32) * scale - slopes[:, None, None] * do
        mo = jnp.broadcast_to(do >= 0, zo.shape)
        z = jnp.concatenate([zs.reshape(B, H, Q_BLOCK, nkeys_sel), zo], axis=-1)
        m = jnp.concatenate([ms.reshape(B, H, Q_BLOCK, nkeys_sel), mo], axis=-1)
        p = masked_softmax(z, m).astype(v.dtype)
        ps = p[..., :nkeys_sel].reshape(B, H, Q_BLOCK, top_k, MOBA_BLOCK)
        po = p[..., nkeys_sel:]
        return jnp.einsum('bhqnl,bqnld->bqhd', ps, vsel) + jnp.einsum('bhql,bld->bqhd', po, vo)

    out = lax.map(block, jnp.arange(S // Q_BLOCK))
    return jnp.swapaxes(out, 0, 1).reshape(B, S, H * dh)


def peer_ffn(x, w_q, sub_k1, sub_k2, u_tab, v_tab):
    B, S, D = x.shape
    half = PEER_QDIM // 2
    xt = x.reshape(B * S // TOK_BLOCK, TOK_BLOCK, D)

    def block(xb):
        q = jnp.einsum('td,dhc->thc', xb, w_q)
        s1 = jnp.einsum('thc,hkc->thk', q[..., :half], sub_k1).astype(jnp.float32)
        s2 = jnp.einsum('thc,hkc->thk', q[..., half:], sub_k2).astype(jnp.float32)
        v1, i1 = lax.top_k(s1, PEER_TOPK)
        v2, i2 = lax.top_k(s2, PEER_TOPK)
        cand = (v1[..., :, None] + v2[..., None, :]).reshape(TOK_BLOCK, PEER_HEADS, PEER_TOPK * PEER_TOPK)
        cidx = (i1[..., :, None] * PEER_NKEYS + i2[..., None, :]).reshape(TOK_BLOCK, PEER_HEADS, PEER_TOPK * PEER_TOPK)
        top, pos = lax.top_k(cand, PEER_TOPK)
        idx = jnp.take_along_axis(cidx, pos, axis=-1)
        g = jax.nn.softmax(top, axis=-1)
        act = jax.nn.gelu(jnp.einsum('thkd,td->thk', u_tab[idx], xb).astype(jnp.float32))
        return jnp.einsum('thk,thkd->td', (g * act).astype(v_tab.dtype), v_tab[idx])

    return lax.map(block, xt).reshape(B, S, D)


def hybrid_layer(x, w_in, nsa_pe_k, nsa_pe_v, nsa_w_ck, nsa_w_cv, w_br_sb, w_br_nsa, w_br_moba, w_out, ln1_g, ln1_b, peer_wq, peer_k1, peer_k2, peer_u, peer_v, ln2_g, ln2_b, nsa_slopes, moba_slopes):
    B, S, D = x.shape
    split_points = np.cumsum(IN_WIDTHS)[:-1].tolist()
    (sb_q, sb_k, sb_v, nsa_q, ck, cv, sk, sv, wk, wv, nsa_g, mb_q, mb_k, mb_v, merge_g) = jnp.split(x @ w_in, split_points, axis=-1)
    heads = lambda t, h: t.reshape(B, S, h, HEAD_DIM)
    bhsd = lambda t, h: jnp.transpose(heads(t, h), (0, 2, 1, 3))
    o_sb = stick_breaking_attention(bhsd(sb_q, SB_HEADS), bhsd(sb_k, SB_HEADS), bhsd(sb_v, SB_HEADS))
    o_nsa = nsa_attention(heads(nsa_q, NSA_HEADS), ck, cv, sk, sv, wk, wv, jax.nn.sigmoid(nsa_g).reshape(B, S, NSA_HEADS, 3), nsa_pe_k, nsa_pe_v, nsa_w_ck, nsa_w_cv, nsa_slopes)
    o_moba = moba_attention(heads(mb_q, MOBA_HEADS), mb_k, mb_v, moba_slopes)
    g = jax.nn.sigmoid(merge_g).reshape(B, S, N_BRANCHES, D)
    mix = g[:, :, 0] * (o_sb @ w_br_sb) + g[:, :, 1] * (o_nsa @ w_br_nsa) + g[:, :, 2] * (o_moba @ w_br_moba)
    x = layer_norm(DEEPNORM_ALPHA * x + mix @ w_out, ln1_g, ln1_b)
    return layer_norm(DEEPNORM_ALPHA * x + peer_ffn(x, peer_wq, peer_k1, peer_k2, peer_u, peer_v), ln2_g, ln2_b)


def setup_inputs(seed: int = 0) -> dict:
    key = jax.random.key(seed)
    ks = jax.random.split(key, 20)
    L, D = DEPTH, D_MODEL
    nrm = lambda k, shape, s: s * jax.random.normal(k, shape, jnp.float32)
    return {
        'x': nrm(ks[0], (BATCH, SEQ, D), 1.0),
        'w_in': nrm(ks[1], (L, D, IN_WIDTH), D ** -0.5),
        'nsa_pe_k': nrm(ks[2], (L, NSA_CMP_LEN, HEAD_DIM), 0.02),
        'nsa_pe_v': nrm(ks[3], (L, NSA_CMP_LEN, HEAD_DIM), 0.02),
        'nsa_w_ck': nrm(ks[4], (L, NSA_CMP_LEN * HEAD_DIM, HEAD_DIM), (NSA_CMP_LEN * HEAD_DIM) ** -0.5),
        'nsa_w_cv': nrm(ks[5], (L, NSA_CMP_LEN * HEAD_DIM, HEAD_DIM), (NSA_CMP_LEN * HEAD_DIM) ** -0.5),
        'w_br_sb': nrm(ks[6], (L, SB_W, D), DEEPNORM_BETA * SB_W ** -0.5),
        'w_br_nsa': nrm(ks[7], (L, NSA_W, D), DEEPNORM_BETA * NSA_W ** -0.5),
        'w_br_moba': nrm(ks[8], (L, MOBA_W, D), DEEPNORM_BETA * MOBA_W ** -0.5),
        'w_out': nrm(ks[9], (L, D, D), DEEPNORM_BETA * D ** -0.5),
        'ln1_g': 1.0 + nrm(ks[10], (L, D), 0.01),
        'ln1_b': nrm(ks[11], (L, D), 0.01),
        'peer_wq': nrm(ks[12], (L, D, PEER_HEADS, PEER_QDIM), D ** -0.5),
        'peer_k1': nrm(ks[13], (L, PEER_HEADS, PEER_NKEYS, PEER_QDIM // 2), (PEER_QDIM // 2) ** -0.5),
        'peer_k2': nrm(ks[14], (L, PEER_HEADS, PEER_NKEYS, PEER_QDIM // 2), (PEER_QDIM // 2) ** -0.5),
        'peer_u': nrm(ks[15], (L, PEER_N_EXPERTS, D), D ** -0.5),
        'peer_v': nrm(ks[16], (L, PEER_N_EXPERTS, D), DEEPNORM_BETA * PEER_HEADS ** -0.5),
        'ln2_g': 1.0 + nrm(ks[17], (L, D), 0.01),
        'ln2_b': nrm(ks[18], (L, D), 0.01),
    }


def reference(x, w_in, nsa_pe_k, nsa_pe_v, nsa_w_ck, nsa_w_cv, w_br_sb, w_br_nsa, w_br_moba, w_out, ln1_g, ln1_b, peer_wq, peer_k1, peer_k2, peer_u, peer_v, ln2_g, ln2_b):
    slopes = alibi_slopes(NSA_HEADS + MOBA_HEADS)
    nsa_slopes = slopes[0::2]
    moba_slopes = slopes[1::2]
    for l in range(DEPTH):
        x = hybrid_layer(x, w_in[l], nsa_pe_k[l], nsa_pe_v[l], nsa_w_ck[l], nsa_w_cv[l], w_br_sb[l], w_br_nsa[l], w_br_moba[l], w_out[l], ln1_g[l], ln1_b[l], peer_wq[l], peer_k1[l], peer_k2[l], peer_u[l], peer_v[l], ln2_g[l], ln2_b[l], nsa_slopes, moba_slopes)
    return x
```

```python
import functools
import math

import numpy as np
import jax
import jax.numpy as jnp
from jax import lax
from jax.experimental import pallas as pl
from jax.experimental.pallas import tpu as pltpu

HEAD_DIM = 32
N_HEADS = 4
Q_BLOCK = 128
NSA_CMP_LEN = 32
NSA_CMP_STRIDE = 16
NSA_SEL_LEN = 64
NSA_TOP_N = 8
NSA_WINDOW = 512
MOBA_BLOCK = 256
MOBA_TOPK = 3
PEER_HEADS = 4
PEER_NKEYS = 128
PEER_TOPK = 8
PEER_QDIM = 256
N_BRANCHES = 3
LN_EPS = 1e-5

LANES = 128
VMEM_LIMIT = 48 * 1024 * 1024

_BF = jnp.bfloat16
_F32 = jnp.float32
_NEG = -1e30

_ALIBI = [2.0 ** (-8.0 * (i + 1) / (2 * N_HEADS)) for i in range(2 * N_HEADS)]
NSA_SLOPES = tuple(_ALIBI[0::2])
MOBA_SLOPES = tuple(_ALIBI[1::2])

G_SBQ, G_SBK, G_SBV, G_NQ, G_SK, G_SV, G_WK, G_WV, G_MQ, G_MK, G_MV = range(11)
N_PROJ_GROUPS = 11


def _dot(a, b):
    return jnp.dot(a.astype(_BF), b.astype(_BF), preferred_element_type=_F32)


def _dot_nt(a, b):
    return lax.dot_general(a.astype(_BF), b.astype(_BF), (((1,), (1,)), ((), ())),
                           preferred_element_type=_F32)


def _dot_split(a, b):
    hi = a.astype(_BF)
    lo = (a - hi.astype(_F32)).astype(_BF)
    return (jnp.dot(hi, b, preferred_element_type=_F32)
            + jnp.dot(lo, b, preferred_element_type=_F32))


def _iota(shape, dim):
    return lax.broadcasted_iota(jnp.int32, shape, dim)


def _lane_head():
    return jnp.right_shift(_iota((Q_BLOCK, LANES), 1), int(math.log2(HEAD_DIM)))


def _head_queries(q, lane_head):
    qf = q.astype(_F32)
    return [jnp.where(lane_head == h, qf, 0.0).astype(_BF) for h in range(N_HEADS)]


def _first_max(x, ids, none):
    m = jnp.max(x, axis=1, keepdims=True)
    return m, jnp.min(jnp.where(x == m, ids, none), axis=1, keepdims=True)


def _params(*sem):
    return pltpu.CompilerParams(dimension_semantics=sem, vmem_limit_bytes=VMEM_LIMIT)


def _layer_norm(y, g, b):
    mu = jnp.mean(y, axis=-1, keepdims=True)
    d = y - mu
    var = jnp.mean(d * d, axis=-1, keepdims=True)
    return d * lax.rsqrt(var + LN_EPS) * g + b


def _inproj_kernel(x_ref, w_ref, proj_ref, ckv_ref, gate_ref):
    y = _dot(x_ref[...], w_ref[...])
    npj = N_PROJ_GROUPS * LANES
    proj_ref[...] = y[:, :npj].astype(proj_ref.dtype)
    ckv_ref[...] = y[:, npj:npj + LANES].astype(ckv_ref.dtype)
    gate_ref[...] = jax.nn.sigmoid(y[:, npj + LANES:])


def _inproj(x2, w_small, tm=512):
    T, D = x2.shape
    n = w_small.shape[1]
    npj = N_PROJ_GROUPS * LANES
    return pl.pallas_call(
        _inproj_kernel,
        grid=(T // tm,),
        in_specs=[pl.BlockSpec((tm, D), lambda i: (i, 0)),
                  pl.BlockSpec((D, n), lambda i: (0, 0))],
        out_specs=[pl.BlockSpec((tm, npj), lambda i: (i, 0)),
                   pl.BlockSpec((tm, LANES), lambda i: (i, 0)),
                   pl.BlockSpec((tm, 3 * LANES), lambda i: (i, 0))],
        out_shape=[jax.ShapeDtypeStruct((T, npj), _BF),
                   jax.ShapeDtypeStruct((T, LANES), _BF),
                   jax.ShapeDtypeStruct((T, 3 * LANES), _F32)],
        compiler_params=_params("parallel"),
        name="inproj",
    )(x2, w_small)


_SB_LOG_CUTOFF = -104.0


def _sb_kernel(q_ref, k_ref, v_ref, o_ref):
    i = pl.program_id(1)
    lane_head = _lane_head()
    qhs = _head_queries(q_ref[0], lane_head)
    row = _iota((Q_BLOCK, Q_BLOCK), 0)
    col = _iota((Q_BLOCK, Q_BLOCK), 1)
    tri = (row > col).astype(_BF)
    diag_past = col < row

    def tile(qh, j, carry, acc, diagonal):
        start = pl.multiple_of(j * Q_BLOCK, Q_BLOCK)
        kt = k_ref[0, pl.ds(start, Q_BLOCK), :]
        vt = v_ref[0, pl.ds(start, Q_BLOCK), :]
        z = _dot_nt(qh, kt)
        ls = -(jnp.maximum(z, 0.0) + jnp.log(1.0 + jnp.exp(-jnp.abs(z))))
        if diagonal:
            ls = jnp.where(diag_past, ls, 0.0)
        excl = _dot_split(ls, tri)
        w = jnp.exp(z + ls + excl + carry)
        if diagonal:
            w = jnp.where(diag_past, w, 0.0)
        acc = acc + _dot(w, vt)
        carry = carry + jnp.sum(ls, axis=1, keepdims=True)
        return carry, acc

    out = jnp.zeros((Q_BLOCK, LANES), _F32)
    for h in range(N_HEADS):
        qh = qhs[h]
        carry, acc = tile(qh, i, jnp.zeros((Q_BLOCK, 1), _F32),
                          jnp.zeros((Q_BLOCK, LANES), _F32), True)

        def cond(st):
            j, _, _, cmax = st
            return jnp.logical_and(j >= 0, cmax > _SB_LOG_CUTOFF)

        def body(st, qh=qh):
            j, carry, acc, _ = st
            carry, acc = tile(qh, j, carry, acc, False)
            return j - 1, carry, acc, jnp.max(carry)

        _, _, acc, _ = lax.while_loop(cond, body, (i - 1, carry, acc, jnp.max(carry)))
        out = out + jnp.where(lane_head == h, acc, 0.0)
    o_ref[0] = out.astype(o_ref.dtype)


def _sb_attention(proj3):
    B, S, _ = proj3.shape
    return pl.pallas_call(
        _sb_kernel,
        grid=(B, S // Q_BLOCK),
        in_specs=[pl.BlockSpec((1, Q_BLOCK, LANES), lambda b, i: (b, i, G_SBQ)),
                  pl.BlockSpec((1, S, LANES), lambda b, i: (b, 0, G_SBK)),
                  pl.BlockSpec((1, S, LANES), lambda b, i: (b, 0, G_SBV))],
        out_specs=pl.BlockSpec((1, Q_BLOCK, LANES), lambda b, i: (b, i, 0)),
        out_shape=jax.ShapeDtypeStruct((B, S, LANES), _BF),
        compiler_params=_params("parallel", "arbitrary"),
        name="sb_attn",
    )(proj3, proj3, proj3)


def _nsa_compress_kernel(c_ref, w_ref, pe_ref, kc_ref, vc_ref):
    c = c_ref[0]
    a = _dot(c, w_ref[0]) + _dot(pe_ref[0], w_ref[0])[0:1]
    b = _dot(c, w_ref[1]) + _dot(pe_ref[1], w_ref[1])[0:1]
    n = a.shape[0]
    b_next = pltpu.roll(b, n - 1, 0)
    y = a + b_next
    kc_ref[0] = y[:, :LANES].astype(kc_ref.dtype)
    vc_ref[0] = y[:, LANES:].astype(vc_ref.dtype)


def _nsa_compress(ckv3, wc, pe):
    B, nchunk, width = ckv3.shape
    return pl.pallas_call(
        _nsa_compress_kernel,
        grid=(B,),
        in_specs=[pl.BlockSpec((1, nchunk, width), lambda b: (b, 0, 0)),
                  pl.BlockSpec((2, width, 2 * LANES), lambda b: (0, 0, 0)),
                  pl.BlockSpec((2, 8, width), lambda b: (0, 0, 0))],
        out_specs=[pl.BlockSpec((1, nchunk, LANES), lambda b: (b, 0, 0)),
                   pl.BlockSpec((1, nchunk, LANES), lambda b: (b, 0, 0))],
        out_shape=[jax.ShapeDtypeStruct((B, nchunk, LANES), _BF),
                   jax.ShapeDtypeStruct((B, nchunk, LANES), _BF)],
        compiler_params=_params("parallel"),
        name="nsa_compress",
    )(ckv3, wc, pe)


def _softmax_rows(z, valid):
    zm = jnp.where(valid, z, _NEG)
    m = jnp.max(zm, axis=1, keepdims=True)
    p = jnp.where(valid, jnp.exp(z - m), 0.0)
    s = jnp.sum(p, axis=1, keepdims=True)
    return p * (1.0 / jnp.maximum(s, 1e-30))


def _nsa_select_kernel(q_ref, kc_ref, vc_ref, wk_ref, wv_ref, g_ref, m_ref, ocw_ref, selm_ref,
                       *, seq):
    i = pl.program_id(1)
    q0 = i * Q_BLOCK
    ncp = seq // NSA_CMP_STRIDE
    nsel = seq // NSA_SEL_LEN
    win = NSA_WINDOW + Q_BLOCK
    lane_head = _lane_head()
    qhs = _head_queries(q_ref[0], lane_head)

    dc = (q0 + _iota((Q_BLOCK, ncp), 0)) - (_iota((Q_BLOCK, ncp), 1) * NSA_CMP_STRIDE + (NSA_CMP_LEN - 1))
    valid_c = dc >= 0
    dcf = dc.astype(_F32)
    kc = kc_ref[0]
    vc = vc_ref[0]
    pg = jnp.zeros((Q_BLOCK, ncp), _F32)
    o_cmp = jnp.zeros((Q_BLOCK, LANES), _F32)
    for h in range(N_HEADS):
        p = _softmax_rows(_dot_nt(qhs[h], kc) - NSA_SLOPES[h] * dcf, valid_c)
        o_cmp = o_cmp + jnp.where(lane_head == h, _dot(p, vc), 0.0)
        pg = pg + p

    imp = _dot_split(pg, m_ref[...])
    blk = _iota((Q_BLOCK, nsel), 1)
    cur = jnp.right_shift(q0 + _iota((Q_BLOCK, nsel), 0), int(math.log2(NSA_SEL_LEN)))
    forced = (blk == 0) | (blk == cur) | (blk == cur - 1)
    x = jnp.where(blk > cur, -jnp.inf, jnp.where(forced, jnp.inf, imp))
    blkf = blk.astype(_F32)
    sel = jnp.zeros((Q_BLOCK, nsel), _F32)
    for _ in range(min(NSA_TOP_N, nsel)):
        _, idx = _first_max(x, blkf, float(nsel))
        hit = blkf == idx
        sel = jnp.where(hit, 1.0, sel)
        x = jnp.where(hit, -jnp.inf, x)
    selm_ref[0] = sel.astype(selm_ref.dtype)

    ws = pl.multiple_of(jnp.maximum(q0 - NSA_WINDOW, 0), Q_BLOCK)
    kw = wk_ref[0, pl.ds(ws, win), :]
    vw = wv_ref[0, pl.ds(ws, win), :]
    dw = (q0 + _iota((Q_BLOCK, win), 0)) - (ws + _iota((Q_BLOCK, win), 1))
    valid_w = (dw >= 0) & (dw < NSA_WINDOW)
    dwf = dw.astype(_F32)
    o_win = jnp.zeros((Q_BLOCK, LANES), _F32)
    for h in range(N_HEADS):
        p = _softmax_rows(_dot_nt(qhs[h], kw) - NSA_SLOPES[h] * dwf, valid_w)
        o_win = o_win + jnp.where(lane_head == h, _dot(p, vw), 0.0)

    g = g_ref[0]
    ocw_ref[0] = g[:, :LANES] * o_cmp + g[:, 2 * LANES:] * o_win


def _nsa_select(proj3, kc4, vc4, gate3, imp_mat):
    B, S, _ = proj3.shape
    ncp = S // NSA_CMP_STRIDE
    nsel = S // NSA_SEL_LEN
    return pl.pallas_call(
        functools.partial(_nsa_select_kernel, seq=S),
        grid=(B, S // Q_BLOCK),
        in_specs=[pl.BlockSpec((1, Q_BLOCK, LANES), lambda b, i: (b, i, G_NQ)),
                  pl.BlockSpec((1, ncp, LANES), lambda b, i: (b, 0, 0)),
                  pl.BlockSpec((1, ncp, LANES), lambda b, i: (b, 0, 0)),
                  pl.BlockSpec((1, S, LANES), lambda b, i: (b, 0, G_WK)),
                  pl.BlockSpec((1, S, LANES), lambda b, i: (b, 0, G_WV)),
                  pl.BlockSpec((1, Q_BLOCK, 3 * LANES), lambda b, i: (b, i, 0)),
                  pl.BlockSpec((ncp, nsel), lambda b, i: (0, 0))],
        out_specs=[pl.BlockSpec((1, Q_BLOCK, LANES), lambda b, i: (b, i, 0)),
                   pl.BlockSpec((1, Q_BLOCK, nsel), lambda b, i: (b, i, 0))],
        out_shape=[jax.ShapeDtypeStruct((B, S, LANES), _F32),
                   jax.ShapeDtypeStruct((B, S, nsel), _BF)],
        compiler_params=_params("parallel", "arbitrary"),
        name="nsa_select",
    )(proj3, kc4, vc4, proj3, proj3, gate3, imp_mat)


def _moba_mean_kernel(k_ref, o_ref, *, nb):
    k = k_ref[0].astype(_F32)
    o_ref[0] = jnp.mean(k.reshape(nb, MOBA_BLOCK, LANES), axis=1)


def _moba_mean(proj3):
    B, S, _ = proj3.shape
    nb = S // MOBA_BLOCK
    return pl.pallas_call(
        functools.partial(_moba_mean_kernel, nb=nb),
        grid=(B,),
        in_specs=[pl.BlockSpec((1, S, LANES), lambda b: (b, 0, G_MK))],
        out_specs=pl.BlockSpec((1, nb, LANES), lambda b: (b, 0, 0)),
        out_shape=jax.ShapeDtypeStruct((B, nb, LANES), _F32),
        compiler_params=_params("parallel"),
        name="moba_mean",
    )(proj3)


def _moba_select_kernel(q_ref, km_ref, selm_ref, *, nb):
    i = pl.program_id(1)
    q0 = i * Q_BLOCK
    sg = _dot_nt(q_ref[0], km_ref[0])
    blk = _iota((Q_BLOCK, nb), 1)
    cur = jnp.right_shift(q0, int(math.log2(MOBA_BLOCK)))
    x = jnp.where(blk < cur, sg, -jnp.inf)
    blkf = blk.astype(_F32)
    sel = jnp.where(blk == cur, 1.0, 0.0)
    for _ in range(min(MOBA_TOPK, nb)):
        m, idx = _first_max(x, blkf, float(nb))
        hit = blkf == idx
        sel = jnp.where(hit & (m > -jnp.inf), 1.0, sel)
        x = jnp.where(hit, -jnp.inf, x)
    selm_ref[0] = sel.astype(selm_ref.dtype)


def _moba_select(proj3, kmean):
    B, S, _ = proj3.shape
    nb = S // MOBA_BLOCK
    return pl.pallas_call(
        functools.partial(_moba_select_kernel, nb=nb),
        grid=(B, S // Q_BLOCK),
        in_specs=[pl.BlockSpec((1, Q_BLOCK, LANES), lambda b, i: (b, i, G_MQ)),
                  pl.BlockSpec((1, nb, LANES), lambda b, i: (b, 0, 0))],
        out_specs=pl.BlockSpec((1, Q_BLOCK, nb), lambda b, i: (b, i, 0)),
        out_shape=jax.ShapeDtypeStruct((B, S, nb), _BF),
        compiler_params=_params("parallel", "arbitrary"),
        name="moba_select",
    )(proj3, kmean)


def _blk_attn_kernel(*refs, blk_len, tile, slopes, nblk, gated):
    if gated:
        q_ref, k_ref, v_ref, selm_ref, g_ref, add_ref, o_ref = refs
    else:
        q_ref, k_ref, v_ref, selm_ref, o_ref = refs
    i = pl.program_id(1)
    q0 = i * Q_BLOCK
    selm = selm_ref[0]
    lane_head = _lane_head()
    qhs = _head_queries(q_ref[0], lane_head)
    bpt = tile // blk_len
    n_tiles = jnp.right_shift(q0 + Q_BLOCK + tile - 1, int(math.log2(tile)))
    d0 = _iota((nblk, tile), 0) - jnp.right_shift(_iota((nblk, tile), 1), int(math.log2(blk_len)))
    rel = _iota((Q_BLOCK, tile), 1) - _iota((Q_BLOCK, tile), 0)
    relf = rel.astype(_F32)

    def body(j, st):
        k0 = pl.multiple_of(j * tile, tile)
        kt = k_ref[0, pl.ds(k0, tile), :]
        vt = v_ref[0, pl.ds(k0, tile), :]
        expand = jnp.where(d0 == j * bpt, 1.0, 0.0).astype(_BF)
        picked = jnp.dot(selm, expand, preferred_element_type=_F32)
        valid = (picked > 0.5) & (rel <= q0 - k0)
        dist = relf + (k0 - q0).astype(_F32)
        new = []
        for h in range(N_HEADS):
            m, l, acc = st[h]
            z = _dot_nt(qhs[h], kt) + slopes[h] * dist
            m_new = jnp.maximum(m, jnp.max(jnp.where(valid, z, _NEG), axis=1, keepdims=True))
            p = jnp.where(valid, jnp.exp(z - m_new), 0.0)
            a = jnp.exp(m - m_new)
            l = a * l + jnp.sum(p, axis=1, keepdims=True)
            acc = a * acc + _dot(p, vt)
            new.append((m_new, l, acc))
        return tuple(new)

    init = tuple((jnp.full((Q_BLOCK, 1), _NEG, _F32), jnp.zeros((Q_BLOCK, 1), _F32),
                  jnp.zeros((Q_BLOCK, LANES), _F32)) for _ in range(N_HEADS))
    st = lax.fori_loop(0, n_tiles, body, init)
    out = jnp.zeros((Q_BLOCK, LANES), _F32)
    for h in range(N_HEADS):
        _, l, acc = st[h]
        out = out + jnp.where(lane_head == h, acc * (1.0 / jnp.maximum(l, 1e-30)), 0.0)
    if gated:
        out = add_ref[0] + g_ref[0] * out
    o_ref[0] = out.astype(o_ref.dtype)


def _blk_attention(proj3, selm, gq, gk, gv, blk_len, slopes, gate3=None, addend=None, tile=256):
    B, S, _ = proj3.shape
    nblk = S // blk_len
    gated = gate3 is not None
    in_specs = [pl.BlockSpec((1, Q_BLOCK, LANES), lambda b, i: (b, i, gq)),
                pl.BlockSpec((1, S, LANES), lambda b, i: (b, 0, gk)),
                pl.BlockSpec((1, S, LANES), lambda b, i: (b, 0, gv)),
                pl.BlockSpec((1, Q_BLOCK, nblk), lambda b, i: (b, i, 0))]
    args = [proj3, proj3, proj3, selm]
    if gated:
        in_specs += [pl.BlockSpec((1, Q_BLOCK, LANES), lambda b, i: (b, i, 1)),
                     pl.BlockSpec((1, Q_BLOCK, LANES), lambda b, i: (b, i, 0))]
        args += [gate3, addend]
    return pl.pallas_call(
        functools.partial(_blk_attn_kernel, blk_len=blk_len, tile=tile, slopes=slopes,
                          nblk=nblk, gated=gated),
        grid=(B, S // Q_BLOCK),
        in_specs=in_specs,
        out_specs=pl.BlockSpec((1, Q_BLOCK, LANES), lambda b, i: (b, i, 0)),
        out_shape=jax.ShapeDtypeStruct((B, S, LANES), _BF),
        compiler_params=_params("parallel", "arbitrary"),
        name="blk_attn_%d" % blk_len,
    )(*args)


def _merge_kernel(x_ref, osb_ref, onsa_ref, omb_ref, wg_ref, wbr_ref, wo_ref, lg_ref, lb_ref, o_ref,
                  *, alpha):
    x = x_ref[...]
    d = x.shape[1]
    gates = jax.nn.sigmoid(_dot(x, wg_ref[...]))
    mix = (gates[:, :d] * _dot(osb_ref[...], wbr_ref[0])
           + gates[:, d:2 * d] * _dot(onsa_ref[...], wbr_ref[1])
           + gates[:, 2 * d:] * _dot(omb_ref[...], wbr_ref[2]))
    y = alpha * x + _dot(mix, wo_ref[...])
    o_ref[...] = _layer_norm(y, lg_ref[...], lb_ref[...])


def _merge(x2, o_sb, o_nsa, o_mb, w_gate, w_br, w_out, ln_g, ln_b, alpha, tm=256):
    T, D = x2.shape
    row = lambda i: (i, 0)
    fixed2 = lambda i: (0, 0)
    return pl.pallas_call(
        functools.partial(_merge_kernel, alpha=alpha),
        grid=(T // tm,),
        in_specs=[pl.BlockSpec((tm, D), row),
                  pl.BlockSpec((tm, LANES), row),
                  pl.BlockSpec((tm, LANES), row),
                  pl.BlockSpec((tm, LANES), row),
                  pl.BlockSpec((D, N_BRANCHES * D), fixed2),
                  pl.BlockSpec((N_BRANCHES, LANES, D), lambda i: (0, 0, 0)),
                  pl.BlockSpec((D, D), fixed2),
                  pl.BlockSpec((1, D), fixed2),
                  pl.BlockSpec((1, D), fixed2)],
        out_specs=pl.BlockSpec((tm, D), row),
        out_shape=jax.ShapeDtypeStruct((T, D), _F32),
        compiler_params=_params("parallel"),
        name="merge",
    )(x2, o_sb, o_nsa, o_mb, w_gate, w_br, w_out, ln_g, ln_b)


_NOT_RETRIEVED = 99.0


def _top_rows(s, k):
    n = s.shape[0]
    rows = _iota(s.shape, 0).astype(_F32)
    vals, ids = [], []
    for _ in range(k):
        m = jnp.max(s, axis=0, keepdims=True)
        idx = jnp.min(jnp.where(s == m, rows, float(n)), axis=0, keepdims=True)
        vals.append(m)
        ids.append(idx)
        s = jnp.where(rows == idx, -jnp.inf, s)
    return vals, ids


def _peer_route_kernel(x_ref, wq_ref, k1_ref, k2_ref, fa_ref, qb_ref, e1_ref, e2_ref):
    half = PEER_QDIM // 2
    qf = _dot(x_ref[...], wq_ref[...]).astype(_BF)
    tm = qf.shape[0]
    rows = _iota((PEER_NKEYS, tm), 0).astype(_F32)
    rows_k = _iota((PEER_TOPK, tm), 0)
    ncand = PEER_TOPK * PEER_TOPK
    pos = _iota((ncand, tm), 0).astype(_F32)
    for h in range(PEER_HEADS):
        s1 = _dot_nt(k1_ref[h], qf[:, h * PEER_QDIM:h * PEER_QDIM + half])
        s2 = _dot_nt(k2_ref[h], qf[:, h * PEER_QDIM + half:(h + 1) * PEER_QDIM])
        v1, i1 = _top_rows(s1, PEER_TOPK)
        v2, i2 = _top_rows(s2, PEER_TOPK)
        v2_all = jnp.zeros((PEER_TOPK, tm), _F32)
        for qi in range(PEER_TOPK):
            v2_all = jnp.where(rows_k == qi, v2[qi], v2_all)
        cand = jnp.concatenate([v1[p] + v2_all for p in range(PEER_TOPK)], axis=0)
        c = cand
        pickf = jnp.zeros(cand.shape, _F32)
        for _ in range(PEER_TOPK):
            m = jnp.max(c, axis=0, keepdims=True)
            idx = jnp.min(jnp.where(c == m, pos, float(ncand)), axis=0, keepdims=True)
            hit = pos == idx
            pickf = jnp.where(hit, 1.0, pickf)
            c = jnp.where(hit, -jnp.inf, c)
        cmax = v1[0] + v2[0]
        z = jnp.sum(pickf * jnp.exp(cand - cmax), axis=0, keepdims=True)
        fa = jnp.zeros((PEER_NKEYS, tm), _F32)
        qb = jnp.full((PEER_NKEYS, tm), _NOT_RETRIEVED, _F32)
        for p in range(PEER_TOPK):
            count = jnp.sum(pickf[p * PEER_TOPK:(p + 1) * PEER_TOPK], axis=0, keepdims=True)
            fa = jnp.where(rows == i1[p], count, fa)
            qb = jnp.where(rows == i2[p], float(p), qb)
        fa_ref[h] = fa
        qb_ref[h] = qb
        e1_ref[h] = jnp.exp(s1 - v1[0]) * (1.0 / z)
        e2_ref[h] = jnp.exp(s2 - v2[0])


def _peer_route(x2, wq, k1, k2, tm=256):
    T, D = x2.shape
    tab = jax.ShapeDtypeStruct((PEER_HEADS, PEER_NKEYS, T), _F32)
    tab_spec = pl.BlockSpec((PEER_HEADS, PEER_NKEYS, tm), lambda i: (0, 0, i))
    return pl.pallas_call(
        _peer_route_kernel,
        grid=(T // tm,),
        in_specs=[pl.BlockSpec((tm, D), lambda i: (i, 0)),
                  pl.BlockSpec(wq.shape, lambda i: (0, 0)),
                  pl.BlockSpec(k1.shape, lambda i: (0, 0, 0)),
                  pl.BlockSpec(k2.shape, lambda i: (0, 0, 0))],
        out_specs=[tab_spec] * 4,
        out_shape=[tab] * 4,
        compiler_params=_params("parallel"),
        name="peer_route",
    )(x2, wq, k1, k2)


def _peer_dense_kernel(x_ref, u_ref, vt_ref, fa_ref, qb_ref, e1_ref, e2_ref, lg_ref, lb_ref, o_ref,
                       acc_ref, xb_ref, *, alpha, te):
    j = pl.program_id(1)

    @pl.when(j == 0)
    def _():
        acc_ref[...] = jnp.zeros_like(acc_ref)
        xb_ref[...] = x_ref[...].astype(_BF)

    s = _dot_nt(u_ref[...], xb_ref[...])
    act = jax.nn.gelu(s)
    parts = []
    for r in range(te // PEER_NKEYS):
        a = j * (te // PEER_NKEYS) + r
        gate = jnp.zeros((PEER_NKEYS, s.shape[1]), _F32)
        for h in range(PEER_HEADS):
            fa = fa_ref[h, pl.ds(a, 1), :]
            e1 = e1_ref[h, pl.ds(a, 1), :]
            gate = gate + e1 * jnp.where(qb_ref[h] < fa, e2_ref[h], 0.0)
        parts.append((gate * act[r * PEER_NKEYS:(r + 1) * PEER_NKEYS]).astype(_BF))
    c = jnp.concatenate(parts, axis=0)
    acc_ref[...] += jnp.dot(vt_ref[...], c, preferred_element_type=_F32)

    @pl.when(j == pl.num_programs(1) - 1)
    def _():
        y = alpha * x_ref[...] + acc_ref[...].T
        o_ref[...] = _layer_norm(y, lg_ref[...], lb_ref[...])


def _peer_dense(x2, u_all, vt_all, layer, tabs, ln_g, ln_b, alpha, tm=512, te=512):
    T, D = x2.shape
    ne = u_all.shape[1]
    tab_spec = pl.BlockSpec((PEER_HEADS, PEER_NKEYS, tm), lambda i, j: (0, 0, i))
    return pl.pallas_call(
        functools.partial(_peer_dense_kernel, alpha=alpha, te=te),
        grid=(T // tm, ne // te),
        in_specs=[pl.BlockSpec((tm, D), lambda i, j: (i, 0)),
                  pl.BlockSpec((None, te, D), lambda i, j: (layer, j, 0)),
                  pl.BlockSpec((None, D, te), lambda i, j: (layer, 0, j)),
                  tab_spec, tab_spec, tab_spec, tab_spec,
                  pl.BlockSpec((1, D), lambda i, j: (0, 0)),
                  pl.BlockSpec((1, D), lambda i, j: (0, 0))],
        out_specs=pl.BlockSpec((tm, D), lambda i, j: (i, 0)),
        out_shape=jax.ShapeDtypeStruct((T, D), _F32),
        scratch_shapes=[pltpu.VMEM((D, tm), _F32), pltpu.VMEM((tm, D), _BF)],
        compiler_params=_params("parallel", "arbitrary"),
        name="peer_dense",
    )(x2, u_all, vt_all, *tabs, ln_g, ln_b)


def _in_widths(d_model):
    w = N_HEADS * HEAD_DIM
    return (w, w, w, w) + (HEAD_DIM,) * 6 + (3 * N_HEADS, w, HEAD_DIM, HEAD_DIM, N_BRANCHES * d_model)


def _arrange_w_in(w_in, d_model):
    off = np.concatenate([[0], np.cumsum(_in_widths(d_model))])
    scale = HEAD_DIM ** -0.5
    seg = lambda n: w_in[:, off[n]:off[n + 1]]
    rep = lambda n: jnp.tile(seg(n), (1, N_HEADS))
    gate_cols = np.array([off[10] + h * 3 + c for c in range(3) for h in range(N_HEADS)
                          for _ in range(HEAD_DIM)])
    ckv = jnp.concatenate([seg(4), seg(5), jnp.zeros((w_in.shape[0], LANES - 2 * HEAD_DIM), w_in.dtype)], axis=1)
    groups = [seg(0) * scale, seg(1), seg(2), seg(3) * scale, rep(6), rep(7), rep(8), rep(9),
              seg(11) * scale, rep(12), rep(13), ckv, w_in[:, gate_cols]]
    return jnp.concatenate(groups, axis=1).astype(_BF), seg(14).astype(_BF)


def _arrange_compress(w_ck, w_cv, pe_k, pe_v):
    half = NSA_CMP_STRIDE
    wk = jnp.tile(w_ck.reshape(2, half, HEAD_DIM, HEAD_DIM), (1, 1, 1, N_HEADS))
    wv = jnp.tile(w_cv.reshape(2, half, HEAD_DIM, HEAD_DIM), (1, 1, 1, N_HEADS))
    w = jnp.zeros((2, half, LANES, 2 * LANES), _F32)
    w = w.at[:, :, :HEAD_DIM, :LANES].set(wk)
    w = w.at[:, :, HEAD_DIM:2 * HEAD_DIM, LANES:].set(wv)
    pe = jnp.zeros((2, half, LANES), _F32)
    pe = pe.at[:, :, :HEAD_DIM].set(pe_k.reshape(2, half, HEAD_DIM))
    pe = pe.at[:, :, HEAD_DIM:2 * HEAD_DIM].set(pe_v.reshape(2, half, HEAD_DIM))
    pe = jnp.broadcast_to(pe.reshape(2, 1, half * LANES), (2, 8, half * LANES))
    return w.reshape(2, half * LANES, 2 * LANES).astype(_BF), pe.astype(_BF)


def _importance_matrix(seq):
    ncp = seq // NSA_CMP_STRIDE
    nsel = seq // NSA_SEL_LEN
    ratio = NSA_SEL_LEN // NSA_CMP_STRIDE
    overlap = np.convolve(np.ones(ratio), np.ones(NSA_CMP_LEN // NSA_CMP_STRIDE))
    n_left = (NSA_CMP_LEN - NSA_CMP_STRIDE) // NSA_CMP_STRIDE
    m = np.zeros((ncp, nsel), np.float32)
    for j in range(nsel):
        for o, c in enumerate(overlap):
            n = ratio * j + o - n_left
            if 0 <= n < ncp - 1:
                m[n, j] += c
    return jnp.asarray(m, _BF)


def kernel(x, w_in, nsa_pe_k, nsa_pe_v, nsa_w_ck, nsa_w_cv, w_br_sb, w_br_nsa, w_br_moba, w_out, ln1_g, ln1_b, peer_wq, peer_k1, peer_k2, peer_u, peer_v, ln2_g, ln2_b):
    B, S, D = x.shape
    depth = w_in.shape[0]
    T = B * S
    assert S % MOBA_BLOCK == 0 and S >= NSA_WINDOW + Q_BLOCK
    alpha = (2.0 * depth) ** 0.25
    imp_mat = _importance_matrix(S)
    u_all = peer_u.astype(_BF)
    vt_all = jnp.swapaxes(peer_v, 1, 2).astype(_BF)
    x2 = x.reshape(T, D)
    for l in range(depth):
        w_small, w_gate = _arrange_w_in(w_in[l], D)
        wc, pe = _arrange_compress(nsa_w_ck[l], nsa_w_cv[l], nsa_pe_k[l], nsa_pe_v[l])
        proj, ckv, ngate = _inproj(x2, w_small)
        proj3 = proj.reshape(B, S, N_PROJ_GROUPS * LANES)
        gate3 = ngate.reshape(B, S, 3 * LANES)

        o_sb = _sb_attention(proj3)

        kc4, vc4 = _nsa_compress(ckv.reshape(B, S // NSA_CMP_STRIDE, NSA_CMP_STRIDE * LANES), wc, pe)
        o_cw, nsa_selm = _nsa_select(proj3, kc4, vc4, gate3, imp_mat)
        o_nsa = _blk_attention(proj3, nsa_selm, G_NQ, G_SK, G_SV, NSA_SEL_LEN, NSA_SLOPES,
                               gate3=gate3, addend=o_cw)

        mb_selm = _moba_select(proj3, _moba_mean(proj3))
        o_mb = _blk_attention(proj3, mb_selm, G_MQ, G_MK, G_MV, MOBA_BLOCK, MOBA_SLOPES)

        w_br = jnp.stack([w_br_sb[l], w_br_nsa[l], w_br_moba[l]]).astype(_BF)
        x2 = _merge(x2, o_sb.reshape(T, LANES), o_nsa.reshape(T, LANES), o_mb.reshape(T, LANES),
                    w_gate, w_br, w_out[l].astype(_BF), ln1_g[l].reshape(1, D), ln1_b[l].reshape(1, D), alpha)

        wq = peer_wq[l].reshape(D, PEER_HEADS * PEER_QDIM).astype(_BF)
        tabs = _peer_route(x2, wq, peer_k1[l].astype(_BF), peer_k2[l].astype(_BF))
        x2 = _peer_dense(x2, u_all, vt_all, l, tabs, ln2_g[l].reshape(1, D), ln2_b[l].reshape(1, D), alpha)
    return x2.reshape(B, S, D)
```

```python
import functools
import math

import numpy as np
import jax
import jax.numpy as jnp
from jax import lax
from jax.experimental import pallas as pl
from jax.experimental.pallas import tpu as pltpu

HEAD_DIM = 32
N_HEADS = 4
Q_BLOCK = 128
NSA_CMP_LEN = 32
NSA_CMP_STRIDE = 16
NSA_SEL_LEN = 64
NSA_TOP_N = 8
NSA_WINDOW = 512
MOBA_BLOCK = 256
MOBA_TOPK = 3
PEER_HEADS = 4
PEER_NKEYS = 128
PEER_TOPK = 8
PEER_QDIM = 256
N_BRANCHES = 3
LN_EPS = 1e-5

LANES = 128
VMEM_LIMIT = 48 * 1024 * 1024

_BF = jnp.bfloat16
_F32 = jnp.float32
_NEG = -1e30

_ALIBI = [2.0 ** (-8.0 * (i + 1) / (2 * N_HEADS)) for i in range(2 * N_HEADS)]
NSA_SLOPES = tuple(_ALIBI[0::2])
MOBA_SLOPES = tuple(_ALIBI[1::2])

G_SBQ, G_SBK, G_SBV, G_NQ, G_SK, G_SKF, G_MK, G_MKF, G_SV, G_WK, G_WV, G_MQ, G_MV = range(13)
N_PROJ_GROUPS = 13
KEY_TILE = 512
F_HI, F_LO, F_ONE = 125, 126, 127
MASK_BIAS = -1e30
SCORE_FLOOR = -5e29


def _dot(a, b):
    return jnp.dot(a.astype(_BF), b.astype(_BF), preferred_element_type=_F32)


def _dot_nt(a, b):
    return lax.dot_general(a.astype(_BF), b.astype(_BF), (((1,), (1,)), ((), ())),
                           preferred_element_type=_F32)


def _dot_split(a, b):
    hi = a.astype(_BF)
    lo = (a - hi.astype(_F32)).astype(_BF)
    return (jnp.dot(hi, b, preferred_element_type=_F32)
            + jnp.dot(lo, b, preferred_element_type=_F32))


def _iota(shape, dim):
    return lax.broadcasted_iota(jnp.int32, shape, dim)


def _lane_head():
    return jnp.right_shift(_iota((Q_BLOCK, LANES), 1), int(math.log2(HEAD_DIM)))


def _head_queries(q, lane_head):
    qf = q.astype(_F32)
    return [jnp.where(lane_head == h, qf, 0.0).astype(_BF) for h in range(N_HEADS)]


def _stack_heads(q, lane_head):
    return jnp.concatenate(_head_queries(q, lane_head), axis=0)


def _unstack_heads(acc, lane_head):
    out = jnp.zeros((Q_BLOCK, LANES), _F32)
    for h in range(N_HEADS):
        out = out + jnp.where(lane_head == h, acc[h * Q_BLOCK:(h + 1) * Q_BLOCK], 0.0)
    return out


def _first_max(x, ids, none):
    m = jnp.max(x, axis=1, keepdims=True)
    return m, jnp.min(jnp.where(x == m, ids, none), axis=1, keepdims=True)


def _params(*sem):
    return pltpu.CompilerParams(dimension_semantics=sem, vmem_limit_bytes=VMEM_LIMIT)


def _layer_norm(y, g, b):
    mu = jnp.mean(y, axis=-1, keepdims=True)
    d = y - mu
    var = jnp.mean(d * d, axis=-1, keepdims=True)
    return d * lax.rsqrt(var + LN_EPS) * g + b


def _inproj_kernel(x_ref, w_ref, feat_ref, proj_ref, ckv_ref, gate_ref):
    y = _dot(x_ref[...], w_ref[...])
    npj = N_PROJ_GROUPS * LANES
    proj_ref[...] = y[:, :npj].astype(proj_ref.dtype)
    proj_ref[:, G_SKF * LANES:(G_SKF + 1) * LANES] = feat_ref[0]
    proj_ref[:, G_MKF * LANES:(G_MKF + 1) * LANES] = feat_ref[1]
    ones = jnp.ones((y.shape[0], HEAD_DIM), proj_ref.dtype)
    proj_ref[:, G_SV * LANES + HEAD_DIM:G_SV * LANES + 2 * HEAD_DIM] = ones
    proj_ref[:, G_MV * LANES + HEAD_DIM:G_MV * LANES + 2 * HEAD_DIM] = ones
    ckv_ref[...] = y[:, npj:npj + LANES].astype(ckv_ref.dtype)
    gate_ref[...] = jax.nn.sigmoid(y[:, npj + LANES:])


def _inproj(x2, w_small, feats):
    T, D = x2.shape
    n = w_small.shape[1]
    npj = N_PROJ_GROUPS * LANES
    tm = KEY_TILE
    return pl.pallas_call(
        _inproj_kernel,
        grid=(T // tm,),
        in_specs=[pl.BlockSpec((tm, D), lambda i: (i, 0)),
                  pl.BlockSpec((D, n), lambda i: (0, 0)),
                  pl.BlockSpec((2, tm, LANES), lambda i: (0, 0, 0))],
        out_specs=[pl.BlockSpec((tm, npj), lambda i: (i, 0)),
                   pl.BlockSpec((tm, LANES), lambda i: (i, 0)),
                   pl.BlockSpec((tm, 3 * LANES), lambda i: (i, 0))],
        out_shape=[jax.ShapeDtypeStruct((T, npj), _BF),
                   jax.ShapeDtypeStruct((T, LANES), _BF),
                   jax.ShapeDtypeStruct((T, 3 * LANES), _F32)],
        compiler_params=_params("parallel"),
        name="inproj",
    )(x2, w_small, feats)


_SB_LOG_CUTOFF = -104.0


def _sb_kernel(q_ref, k_ref, v_ref, o_ref):
    i = pl.program_id(1)
    lane_head = _lane_head()
    qs = _stack_heads(q_ref[0], lane_head)
    rows = N_HEADS * Q_BLOCK
    tri = (_iota((Q_BLOCK, Q_BLOCK), 0) > _iota((Q_BLOCK, Q_BLOCK), 1)).astype(_BF)
    q_off = jnp.bitwise_and(_iota((rows, Q_BLOCK), 0), Q_BLOCK - 1)
    diag_past = _iota((rows, Q_BLOCK), 1) < q_off

    def tile(j, carry, acc, diagonal):
        start = pl.multiple_of(j * Q_BLOCK, Q_BLOCK)
        kt = k_ref[0, pl.ds(start, Q_BLOCK), :]
        vt = v_ref[0, pl.ds(start, Q_BLOCK), :]
        z = _dot_nt(qs, kt)
        ls = -(jnp.maximum(z, 0.0) + jnp.log(1.0 + jnp.exp(-jnp.abs(z))))
        if diagonal:
            ls = jnp.where(diag_past, ls, 0.0)
        excl = _dot_split(ls, tri)
        w = jnp.exp(z + ls + excl + carry)
        if diagonal:
            w = jnp.where(diag_past, w, 0.0)
        acc = acc + _dot(w, vt)
        carry = carry + jnp.sum(ls, axis=1, keepdims=True)
        return carry, acc

    carry, acc = tile(i, jnp.zeros((rows, 1), _F32), jnp.zeros((rows, LANES), _F32), True)

    def cond(st):
        j, _, _, cmax = st
        return jnp.logical_and(j >= 0, cmax > _SB_LOG_CUTOFF)

    def body(st):
        j, carry, acc, _ = st
        carry, acc = tile(j, carry, acc, False)
        return j - 1, carry, acc, jnp.max(carry)

    _, _, acc, _ = lax.while_loop(cond, body, (i - 1, carry, acc, jnp.max(carry)))
    o_ref[0] = _unstack_heads(acc, lane_head).astype(o_ref.dtype)


def _sb_attention(proj3):
    B, S, _ = proj3.shape
    return pl.pallas_call(
        _sb_kernel,
        grid=(B, S // Q_BLOCK),
        in_specs=[pl.BlockSpec((1, Q_BLOCK, LANES), lambda b, i: (b, i, G_SBQ)),
                  pl.BlockSpec((1, S, LANES), lambda b, i: (b, 0, G_SBK)),
                  pl.BlockSpec((1, S, LANES), lambda b, i: (b, 0, G_SBV))],
        out_specs=pl.BlockSpec((1, Q_BLOCK, LANES), lambda b, i: (b, i, 0)),
        out_shape=jax.ShapeDtypeStruct((B, S, LANES), _BF),
        compiler_params=_params("parallel", "arbitrary"),
        name="sb_attn",
    )(proj3, proj3, proj3)


def _nsa_compress_kernel(c_ref, w_ref, pe_ref, kc_ref, vc_ref):
    c = c_ref[0]
    a = _dot(c, w_ref[0]) + _dot(pe_ref[0], w_ref[0])[0:1]
    b = _dot(c, w_ref[1]) + _dot(pe_ref[1], w_ref[1])[0:1]
    n = a.shape[0]
    b_next = pltpu.roll(b, n - 1, 0)
    y = a + b_next
    kc_ref[0] = y[:, :LANES].astype(kc_ref.dtype)
    vc_ref[0] = y[:, LANES:].astype(vc_ref.dtype)


def _nsa_compress(ckv3, wc, pe):
    B, nchunk, width = ckv3.shape
    return pl.pallas_call(
        _nsa_compress_kernel,
        grid=(B,),
        in_specs=[pl.BlockSpec((1, nchunk, width), lambda b: (b, 0, 0)),
                  pl.BlockSpec((2, width, 2 * LANES), lambda b: (0, 0, 0)),
                  pl.BlockSpec((2, 8, width), lambda b: (0, 0, 0))],
        out_specs=[pl.BlockSpec((1, nchunk, LANES), lambda b: (b, 0, 0)),
                   pl.BlockSpec((1, nchunk, LANES), lambda b: (b, 0, 0))],
        out_shape=[jax.ShapeDtypeStruct((B, nchunk, LANES), _BF),
                   jax.ShapeDtypeStruct((B, nchunk, LANES), _BF)],
        compiler_params=_params("parallel"),
        name="nsa_compress",
    )(ckv3, wc, pe)


def _softmax_rows(z, valid):
    zm = jnp.where(valid, z, _NEG)
    m = jnp.max(zm, axis=1, keepdims=True)
    p = jnp.where(valid, jnp.exp(z - m), 0.0)
    s = jnp.sum(p, axis=1, keepdims=True)
    return p * (1.0 / jnp.maximum(s, 1e-30))


def _nsa_select_kernel(q_ref, kc_ref, vc_ref, wk_ref, wv_ref, g_ref, m_ref, ocw_ref, selm_ref,
                       *, seq):
    i = pl.program_id(1)
    q0 = i * Q_BLOCK
    ncp = seq // NSA_CMP_STRIDE
    nsel = m_ref.shape[1]
    win = NSA_WINDOW + Q_BLOCK
    lane_head = _lane_head()
    qhs = _head_queries(q_ref[0], lane_head)

    dc = (q0 + _iota((Q_BLOCK, ncp), 0)) - (_iota((Q_BLOCK, ncp), 1) * NSA_CMP_STRIDE + (NSA_CMP_LEN - 1))
    valid_c = dc >= 0
    dcf = dc.astype(_F32)
    kc = kc_ref[0]
    vc = vc_ref[0]
    pg = jnp.zeros((Q_BLOCK, ncp), _F32)
    o_cmp = jnp.zeros((Q_BLOCK, LANES), _F32)
    for h in range(N_HEADS):
        p = _softmax_rows(_dot_nt(qhs[h], kc) - NSA_SLOPES[h] * dcf, valid_c)
        o_cmp = o_cmp + jnp.where(lane_head == h, _dot(p, vc), 0.0)
        pg = pg + p

    imp = _dot_split(pg, m_ref[...])
    blk = _iota((Q_BLOCK, nsel), 1)
    cur = jnp.right_shift(q0 + _iota((Q_BLOCK, nsel), 0), int(math.log2(NSA_SEL_LEN)))
    forced = (blk == 0) | (blk == cur) | (blk == cur - 1)
    x = jnp.where(blk > cur, -jnp.inf, jnp.where(forced, jnp.inf, imp))
    blkf = blk.astype(_F32)
    sel = jnp.zeros((Q_BLOCK, nsel), _F32)
    for _ in range(min(NSA_TOP_N, seq // NSA_SEL_LEN)):
        _, idx = _first_max(x, blkf, float(nsel))
        hit = blkf == idx
        sel = jnp.where(hit, 1.0, sel)
        x = jnp.where(hit, -jnp.inf, x)
    selm_ref[0] = sel.astype(selm_ref.dtype)

    ws = pl.multiple_of(jnp.maximum(q0 - NSA_WINDOW, 0), Q_BLOCK)
    kw = wk_ref[0, pl.ds(ws, win), :]
    vw = wv_ref[0, pl.ds(ws, win), :]
    dw = (q0 + _iota((Q_BLOCK, win), 0)) - (ws + _iota((Q_BLOCK, win), 1))
    valid_w = (dw >= 0) & (dw < NSA_WINDOW)
    dwf = dw.astype(_F32)
    o_win = jnp.zeros((Q_BLOCK, LANES), _F32)
    for h in range(N_HEADS):
        p = _softmax_rows(_dot_nt(qhs[h], kw) - NSA_SLOPES[h] * dwf, valid_w)
        o_win = o_win + jnp.where(lane_head == h, _dot(p, vw), 0.0)

    g = g_ref[0]
    ocw_ref[0] = g[:, :LANES] * o_cmp + g[:, 2 * LANES:] * o_win


def _nsa_select(proj3, kc4, vc4, gate3, imp_mat):
    B, S, _ = proj3.shape
    ncp = S // NSA_CMP_STRIDE
    nsel = imp_mat.shape[1]
    return pl.pallas_call(
        functools.partial(_nsa_select_kernel, seq=S),
        grid=(B, S // Q_BLOCK),
        in_specs=[pl.BlockSpec((1, Q_BLOCK, LANES), lambda b, i: (b, i, G_NQ)),
                  pl.BlockSpec((1, ncp, LANES), lambda b, i: (b, 0, 0)),
                  pl.BlockSpec((1, ncp, LANES), lambda b, i: (b, 0, 0)),
                  pl.BlockSpec((1, S, LANES), lambda b, i: (b, 0, G_WK)),
                  pl.BlockSpec((1, S, LANES), lambda b, i: (b, 0, G_WV)),
                  pl.BlockSpec((1, Q_BLOCK, 3 * LANES), lambda b, i: (b, i, 0)),
                  pl.BlockSpec((ncp, nsel), lambda b, i: (0, 0))],
        out_specs=[pl.BlockSpec((1, Q_BLOCK, LANES), lambda b, i: (b, i, 0)),
                   pl.BlockSpec((1, Q_BLOCK, nsel), lambda b, i: (b, i, 0))],
        out_shape=[jax.ShapeDtypeStruct((B, S, LANES), _F32),
                   jax.ShapeDtypeStruct((B, S, nsel), _BF)],
        compiler_params=_params("parallel", "arbitrary"),
        name="nsa_select",
    )(proj3, kc4, vc4, proj3, proj3, gate3, imp_mat)


def _moba_mean_kernel(k_ref, o_ref, *, nb):
    k = k_ref[0].astype(_F32)
    o_ref[0] = jnp.zeros(o_ref.shape[1:], _F32)
    o_ref[0, :nb, :] = jnp.mean(k.reshape(nb, MOBA_BLOCK, LANES), axis=1)


def _moba_mean(proj3):
    B, S, _ = proj3.shape
    nb = S // MOBA_BLOCK
    return pl.pallas_call(
        functools.partial(_moba_mean_kernel, nb=nb),
        grid=(B,),
        in_specs=[pl.BlockSpec((1, S, LANES), lambda b: (b, 0, G_MK))],
        out_specs=pl.BlockSpec((1, LANES, LANES), lambda b: (b, 0, 0)),
        out_shape=jax.ShapeDtypeStruct((B, LANES, LANES), _F32),
        compiler_params=_params("parallel"),
        name="moba_mean",
    )(proj3)


def _moba_select_kernel(q_ref, km_ref, selm_ref, *, nb):
    i = pl.program_id(1)
    q0 = i * Q_BLOCK
    sg = _dot_nt(q_ref[0], km_ref[0])
    blk = _iota((Q_BLOCK, LANES), 1)
    cur = jnp.right_shift(q0, int(math.log2(MOBA_BLOCK)))
    x = jnp.where(blk < cur, sg, -jnp.inf)
    blkf = blk.astype(_F32)
    sel = jnp.where(blk == cur, 1.0, 0.0)
    for _ in range(min(MOBA_TOPK, nb)):
        m, idx = _first_max(x, blkf, float(LANES))
        hit = blkf == idx
        sel = jnp.where(hit & (m > -jnp.inf), 1.0, sel)
        x = jnp.where(hit, -jnp.inf, x)
    selm_ref[0] = sel.astype(selm_ref.dtype)


def _moba_select(proj3, kmean):
    B, S, _ = proj3.shape
    nb = S // MOBA_BLOCK
    assert nb <= LANES
    return pl.pallas_call(
        functools.partial(_moba_select_kernel, nb=nb),
        grid=(B, S // Q_BLOCK),
        in_specs=[pl.BlockSpec((1, Q_BLOCK, LANES), lambda b, i: (b, i, G_MQ)),
                  pl.BlockSpec((1, LANES, LANES), lambda b, i: (b, 0, 0))],
        out_specs=pl.BlockSpec((1, Q_BLOCK, LANES), lambda b, i: (b, i, 0)),
        out_shape=jax.ShapeDtypeStruct((B, S, LANES), _BF),
        compiler_params=_params("parallel", "arbitrary"),
        name="moba_select",
    )(proj3, kmean)


def _blk_attn_kernel(*refs, blk_len, slopes, nblk, seq, gated):
    if gated:
        q_ref, kx_ref, v_ref, selm_ref, g_ref, add_ref, o_ref = refs[:7]
    else:
        q_ref, kx_ref, v_ref, selm_ref, o_ref = refs[:5]
    qx_ref, m_ref, acc_ref, za_ref, zb_ref, flag_ref, list_ref = refs[-7:]
    tile = KEY_TILE
    rows = N_HEADS * Q_BLOCK
    bpt = tile // blk_len
    max_tiles = seq // tile
    i = pl.program_id(1)
    q0 = i * Q_BLOCK
    lane_head = _lane_head()
    lane = _iota((Q_BLOCK, LANES), 1)

    qx_ref[:, :LANES] = _stack_heads(q_ref[0], lane_head)
    m_ref[...] = jnp.full((rows, LANES), SCORE_FLOOR, _F32)
    acc_ref[...] = jnp.zeros((rows, LANES), _F32)
    lane1 = _iota((1, LANES), 1)
    fixed_lanes = [jnp.where(lane1 == F_HI, slopes[h] * 256.0, jnp.where(lane1 == F_LO, slopes[h], 0.0))
                   for h in range(N_HEADS)]
    dist_lanes = [jnp.where(lane1 == F_ONE, slopes[h] * Q_BLOCK, 0.0) for h in range(N_HEADS)]

    selm = selm_ref[0].astype(_F32)
    sneg = (1.0 - selm) * MASK_BIAS
    any_q = jnp.broadcast_to(jnp.max(selm, axis=0, keepdims=True), (8, nblk))
    group = (jnp.right_shift(_iota((nblk, LANES), 0), int(math.log2(bpt))) == _iota((nblk, LANES), 1))
    tile_hits = _dot(any_q, jnp.where(group, 1.0, 0.0))
    for t in range(max_tiles):
        flag_ref[t] = (tile_hits[0, t] > 0.5).astype(jnp.int32)

    n_tiles = jnp.right_shift(q0 + Q_BLOCK + tile - 1, int(math.log2(tile)))

    dead_bit = 1 << 20

    def scan(t, n):
        @pl.when(flag_ref[t] > 0)
        def _():
            list_ref[n] = t
        return n + (flag_ref[t] > 0).astype(jnp.int32)

    n_before = lax.fori_loop(0, n_tiles - 1, scan, 0)
    list_ref[n_before] = dead_bit
    n_pairs = jnp.right_shift(n_before + 1, 1)
    list_ref[2 * n_pairs] = n_tiles - 1

    def scores(entry, z_ref):
        j = jnp.bitwise_and(entry, dead_bit - 1)
        k0 = pl.multiple_of(j * tile, tile)
        first = j * bpt
        if nblk > LANES:
            half = jnp.right_shift(first, int(math.log2(LANES)))
            base = sneg[:, :LANES]
            for c in range(1, nblk // LANES):
                base = jnp.where(half == c, sneg[:, c * LANES:(c + 1) * LANES], base)
        else:
            base = sneg
        shift = jnp.bitwise_and(LANES - jnp.bitwise_and(first, LANES - 1), LANES - 1)
        tile_mask = pltpu.roll(base, shift, 1)
        tile_mask = jnp.where(entry >= dead_bit, MASK_BIAS, tile_mask)
        tile_mask = jnp.where(lane >= F_HI, 0.0, tile_mask)
        dist = (j * (tile // Q_BLOCK) - i).astype(_F32)
        for h in range(N_HEADS):
            ext = tile_mask + (fixed_lanes[h] + dist * dist_lanes[h])
            qx_ref[h * Q_BLOCK:(h + 1) * Q_BLOCK, LANES:] = ext.astype(_BF)
        kx = kx_ref[0, pl.ds(k0, tile), :]
        z_ref[...] = lax.dot_general(qx_ref[...], kx, (((1,), (1,)), ((), ())),
                                     preferred_element_type=_F32)

    def absorb(z_ref, entry, diagonal):
        j = jnp.bitwise_and(entry, dead_bit - 1)
        k0 = pl.multiple_of(j * tile, tile)
        vt = v_ref[0, pl.ds(k0, tile), :]
        z = z_ref[...]
        if diagonal:
            q_off = jnp.bitwise_and(_iota((rows, tile), 0), Q_BLOCK - 1)
            z = jnp.where(_iota((rows, tile), 1) - q_off <= q0 - k0, z, MASK_BIAS)
        m_old = m_ref[...]
        m_new = jnp.maximum(m_old, jnp.max(z, axis=1, keepdims=True))
        p = jnp.exp(z - jnp.concatenate([m_new] * (tile // LANES), axis=1))
        a = jnp.exp(m_old - m_new)
        acc_ref[...] = a * acc_ref[...] + _dot(p, vt)
        m_ref[...] = m_new

    def body(kk, c):
        scores(list_ref[2 * kk + 1], zb_ref)
        absorb(za_ref, list_ref[2 * kk], False)
        scores(list_ref[2 * kk + 2], za_ref)
        absorb(zb_ref, list_ref[2 * kk + 1], False)
        return c

    scores(list_ref[0], za_ref)
    lax.fori_loop(0, n_pairs, body, 0)
    absorb(za_ref, n_tiles - 1, True)

    acc = acc_ref[...]
    total = pltpu.roll(acc, LANES - HEAD_DIM, 1)
    o = acc * (1.0 / jnp.maximum(total, 1e-30))
    out = jnp.zeros((Q_BLOCK, LANES), _F32)
    for h in range(N_HEADS):
        o_h = o[h * Q_BLOCK:(h + 1) * Q_BLOCK]
        if h:
            o_h = pltpu.roll(o_h, h * HEAD_DIM, 1)
        out = out + jnp.where(lane_head == h, o_h, 0.0)
    if gated:
        out = add_ref[0] + g_ref[0] * out
    o_ref[0] = out.astype(o_ref.dtype)


def _blk_attention(proj3, selm, gq, gkx, gv, blk_len, slopes, gate3=None, addend=None):
    B, S, _ = proj3.shape
    nblk = selm.shape[-1]
    assert S % KEY_TILE == 0 and gkx % 2 == 0 and nblk % LANES == 0
    gated = gate3 is not None
    rows = N_HEADS * Q_BLOCK
    in_specs = [pl.BlockSpec((1, Q_BLOCK, LANES), lambda b, i: (b, i, gq)),
                pl.BlockSpec((1, S, 2 * LANES), lambda b, i: (b, 0, gkx // 2)),
                pl.BlockSpec((1, S, LANES), lambda b, i: (b, 0, gv)),
                pl.BlockSpec((1, Q_BLOCK, nblk), lambda b, i: (b, i, 0))]
    args = [proj3, proj3, proj3, selm]
    if gated:
        in_specs += [pl.BlockSpec((1, Q_BLOCK, LANES), lambda b, i: (b, i, 1)),
                     pl.BlockSpec((1, Q_BLOCK, LANES), lambda b, i: (b, i, 0))]
        args += [gate3, addend]
    return pl.pallas_call(
        functools.partial(_blk_attn_kernel, blk_len=blk_len, slopes=slopes, nblk=nblk, seq=S,
                          gated=gated),
        grid=(B, S // Q_BLOCK),
        in_specs=in_specs,
        out_specs=pl.BlockSpec((1, Q_BLOCK, LANES), lambda b, i: (b, i, 0)),
        out_shape=jax.ShapeDtypeStruct((B, S, LANES), _BF),
        scratch_shapes=[pltpu.VMEM((rows, 2 * LANES), _BF),
                        pltpu.VMEM((rows, LANES), _F32),
                        pltpu.VMEM((rows, LANES), _F32),
                        pltpu.VMEM((rows, KEY_TILE), _F32),
                        pltpu.VMEM((rows, KEY_TILE), _F32),
                        pltpu.SMEM((S // KEY_TILE,), jnp.int32),
                        pltpu.SMEM((S // KEY_TILE + 2,), jnp.int32)],
        compiler_params=_params("parallel", "arbitrary"),
        name="blk_attn_%d" % blk_len,
    )(*args)


def _merge_kernel(x_ref, osb_ref, onsa_ref, omb_ref, wg_ref, wbr_ref, wo_ref, lg_ref, lb_ref, o_ref,
                  *, alpha):
    x = x_ref[...]
    d = x.shape[1]
    gates = jax.nn.sigmoid(_dot(x, wg_ref[...]))
    mix = (gates[:, :d] * _dot(osb_ref[...], wbr_ref[0])
           + gates[:, d:2 * d] * _dot(onsa_ref[...], wbr_ref[1])
           + gates[:, 2 * d:] * _dot(omb_ref[...], wbr_ref[2]))
    y = alpha * x + _dot(mix, wo_ref[...])
    o_ref[...] = _layer_norm(y, lg_ref[...], lb_ref[...])


def _merge(x2, o_sb, o_nsa, o_mb, w_gate, w_br, w_out, ln_g, ln_b, alpha, tm=256):
    T, D = x2.shape
    row = lambda i: (i, 0)
    fixed2 = lambda i: (0, 0)
    return pl.pallas_call(
        functools.partial(_merge_kernel, alpha=alpha),
        grid=(T // tm,),
        in_specs=[pl.BlockSpec((tm, D), row),
                  pl.BlockSpec((tm, LANES), row),
                  pl.BlockSpec((tm, LANES), row),
                  pl.BlockSpec((tm, LANES), row),
                  pl.BlockSpec((D, N_BRANCHES * D), fixed2),
                  pl.BlockSpec((N_BRANCHES, LANES, D), lambda i: (0, 0, 0)),
                  pl.BlockSpec((D, D), fixed2),
                  pl.BlockSpec((1, D), fixed2),
                  pl.BlockSpec((1, D), fixed2)],
        out_specs=pl.BlockSpec((tm, D), row),
        out_shape=jax.ShapeDtypeStruct((T, D), _F32),
        compiler_params=_params("parallel"),
        name="merge",
    )(x2, o_sb, o_nsa, o_mb, w_gate, w_br, w_out, ln_g, ln_b)


_NOT_RETRIEVED = 99.0


def _top_rows(s, k):
    n = s.shape[0]
    rows = _iota(s.shape, 0).astype(_F32)
    vals, ids = [], []
    for _ in range(k):
        m = jnp.max(s, axis=0, keepdims=True)
        idx = jnp.min(jnp.where(s == m, rows, float(n)), axis=0, keepdims=True)
        vals.append(m)
        ids.append(idx)
        s = jnp.where(rows == idx, -jnp.inf, s)
    return vals, ids


def _peer_route_kernel(x_ref, wq_ref, k1_ref, k2_ref, fa_ref, qb_ref, e1_ref, e2_ref):
    half = PEER_QDIM // 2
    qf = _dot(x_ref[...], wq_ref[...]).astype(_BF)
    tm = qf.shape[0]
    rows = _iota((PEER_NKEYS, tm), 0).astype(_F32)
    rows_k = _iota((PEER_TOPK, tm), 0)
    ncand = PEER_TOPK * PEER_TOPK
    pos = _iota((ncand, tm), 0).astype(_F32)
    for h in range(PEER_HEADS):
        s1 = _dot_nt(k1_ref[h], qf[:, h * PEER_QDIM:h * PEER_QDIM + half])
        s2 = _dot_nt(k2_ref[h], qf[:, h * PEER_QDIM + half:(h + 1) * PEER_QDIM])
        v1, i1 = _top_rows(s1, PEER_TOPK)
        v2, i2 = _top_rows(s2, PEER_TOPK)
        v2_all = jnp.zeros((PEER_TOPK, tm), _F32)
        for qi in range(PEER_TOPK):
            v2_all = jnp.where(rows_k == qi, v2[qi], v2_all)
        cand = jnp.concatenate([v1[p] + v2_all for p in range(PEER_TOPK)], axis=0)
        c = cand
        pickf = jnp.zeros(cand.shape, _F32)
        for _ in range(PEER_TOPK):
            m = jnp.max(c, axis=0, keepdims=True)
            idx = jnp.min(jnp.where(c == m, pos, float(ncand)), axis=0, keepdims=True)
            hit = pos == idx
            pickf = jnp.where(hit, 1.0, pickf)
            c = jnp.where(hit, -jnp.inf, c)
        cmax = v1[0] + v2[0]
        z = jnp.sum(pickf * jnp.exp(cand - cmax), axis=0, keepdims=True)
        fa = jnp.zeros((PEER_NKEYS, tm), _F32)
        qb = jnp.full((PEER_NKEYS, tm), _NOT_RETRIEVED, _F32)
        for p in range(PEER_TOPK):
            count = jnp.sum(pickf[p * PEER_TOPK:(p + 1) * PEER_TOPK], axis=0, keepdims=True)
            fa = jnp.where(rows == i1[p], count, fa)
            qb = jnp.where(rows == i2[p], float(p), qb)
        fa_ref[h] = fa
        qb_ref[h] = qb.astype(qb_ref.dtype)
        e1_ref[h] = jnp.exp(s1 - v1[0]) * (1.0 / z)
        e2_ref[h] = jnp.exp(s2 - v2[0]).astype(e2_ref.dtype)


def _peer_route(x2, wq, k1, k2, tm=256):
    T, D = x2.shape
    tab = jax.ShapeDtypeStruct((PEER_HEADS, PEER_NKEYS, T), _F32)
    tab_bf = jax.ShapeDtypeStruct((PEER_HEADS, PEER_NKEYS, T), _BF)
    tab_spec = pl.BlockSpec((PEER_HEADS, PEER_NKEYS, tm), lambda i: (0, 0, i))
    return pl.pallas_call(
        _peer_route_kernel,
        grid=(T // tm,),
        in_specs=[pl.BlockSpec((tm, D), lambda i: (i, 0)),
                  pl.BlockSpec(wq.shape, lambda i: (0, 0)),
                  pl.BlockSpec(k1.shape, lambda i: (0, 0, 0)),
                  pl.BlockSpec(k2.shape, lambda i: (0, 0, 0))],
        out_specs=[tab_spec] * 4,
        out_shape=[tab, tab_bf, tab, tab_bf],
        compiler_params=_params("parallel"),
        name="peer_route",
    )(x2, wq, k1, k2)


def _gelu_tanh(s):
    c1 = math.sqrt(2.0 / math.pi)
    k1 = jnp.asarray(c1, s.dtype)
    k2 = jnp.asarray(c1 * 0.044715, s.dtype)
    inner = s * (k1 + k2 * (s * s))
    half = jnp.asarray(0.5, s.dtype) * s
    return half + half * jnp.tanh(inner)


def _peer_dense_kernel(x_ref, u_ref, vt_ref, fa_ref, qb_ref, e1_ref, e2_ref, lg_ref, lb_ref, o_ref,
                       acc_ref, xb_ref, s_ref, c_ref, *, alpha, te):
    j = pl.program_id(1)
    n_tiles = pl.num_programs(1) - 2

    @pl.when(j == 0)
    def _():
        acc_ref[...] = jnp.zeros_like(acc_ref)
        s_ref[...] = jnp.zeros_like(s_ref)
        c_ref[...] = jnp.zeros_like(c_ref)
        xb_ref[...] = x_ref[...].astype(_BF)

    jg = jnp.clip(j - 1, 0, n_tiles - 1)

    cur = j % 2
    tm = s_ref.shape[2]
    tc = tm // 2
    for ck in range(tm // tc):
        cols = slice(ck * tc, (ck + 1) * tc)
        acc_ref[:, cols] += jnp.dot(vt_ref[...], c_ref[1 - cur, :, cols], preferred_element_type=_F32)
        s = s_ref[1 - cur, :, cols]
        act = _gelu_tanh(s)
        for r in range(te // PEER_NKEYS):
            a = jg * (te // PEER_NKEYS) + r
            gate = jnp.zeros((PEER_NKEYS, tc), _BF)
            for h in range(PEER_HEADS):
                fa = fa_ref[h, pl.ds(a, 1), cols].astype(_BF)
                e1 = e1_ref[h, pl.ds(a, 1), cols].astype(_BF)
                gate = gate + e1 * jnp.where(qb_ref[h, :, cols] < fa, e2_ref[h, :, cols], jnp.zeros((), _BF))
            c_ref[cur, r * PEER_NKEYS:(r + 1) * PEER_NKEYS, cols] = (
                gate * act[r * PEER_NKEYS:(r + 1) * PEER_NKEYS])
        s_ref[cur, :, cols] = _dot_nt(u_ref[...], xb_ref[cols, :]).astype(_BF)

    @pl.when(j == n_tiles + 1)
    def _():
        y = alpha * x_ref[...] + acc_ref[...].T
        o_ref[...] = _layer_norm(y, lg_ref[...], lb_ref[...])


def _peer_dense(x2, u_all, vt_all, layer, tabs, ln_g, ln_b, alpha, tm=512, te=512):
    T, D = x2.shape
    n_tiles = u_all.shape[1] // te
    tab_spec = pl.BlockSpec((PEER_HEADS, PEER_NKEYS, tm), lambda i, j: (0, 0, i))
    return pl.pallas_call(
        functools.partial(_peer_dense_kernel, alpha=alpha, te=te),
        grid=(T // tm, n_tiles + 2),
        in_specs=[pl.BlockSpec((tm, D), lambda i, j: (i, 0)),
                  pl.BlockSpec((None, te, D), lambda i, j: (layer, jnp.minimum(j, n_tiles - 1), 0)),
                  pl.BlockSpec((None, D, te), lambda i, j: (layer, 0, jnp.clip(j - 2, 0, n_tiles - 1))),
                  tab_spec, tab_spec, tab_spec, tab_spec,
                  pl.BlockSpec((1, D), lambda i, j: (0, 0)),
                  pl.BlockSpec((1, D), lambda i, j: (0, 0))],
        out_specs=pl.BlockSpec((tm, D), lambda i, j: (i, 0)),
        out_shape=jax.ShapeDtypeStruct((T, D), _F32),
        scratch_shapes=[pltpu.VMEM((D, tm), _F32), pltpu.VMEM((tm, D), _BF),
                        pltpu.VMEM((2, te, tm), _BF), pltpu.VMEM((2, te, tm), _BF)],
        compiler_params=_params("parallel", "arbitrary"),
        name="peer_dense",
    )(x2, u_all, vt_all, *tabs, ln_g, ln_b)


def _in_widths(d_model):
    w = N_HEADS * HEAD_DIM
    return (w, w, w, w) + (HEAD_DIM,) * 6 + (3 * N_HEADS, w, HEAD_DIM, HEAD_DIM, N_BRANCHES * d_model)


def _arrange_w_in(w_in, d_model):
    off = np.concatenate([[0], np.cumsum(_in_widths(d_model))])
    scale = HEAD_DIM ** -0.5
    seg = lambda n: w_in[:, off[n]:off[n + 1]]
    rep = lambda n: jnp.tile(seg(n), (1, N_HEADS))
    gate_cols = np.array([off[10] + h * 3 + c for c in range(3) for h in range(N_HEADS)
                          for _ in range(HEAD_DIM)])
    ckv = jnp.concatenate([seg(4), seg(5), jnp.zeros((w_in.shape[0], LANES - 2 * HEAD_DIM), w_in.dtype)], axis=1)
    blank = jnp.zeros((w_in.shape[0], LANES), w_in.dtype)
    once = lambda n: jnp.concatenate([seg(n), blank[:, HEAD_DIM:]], axis=1)
    groups = [seg(0) * scale, seg(1), seg(2), seg(3) * scale, rep(6), blank, rep(12), blank,
              once(7), rep(8), rep(9), seg(11) * scale, once(13), ckv, w_in[:, gate_cols]]
    return jnp.concatenate(groups, axis=1).astype(_BF), seg(14).astype(_BF)


def _arrange_compress(w_ck, w_cv, pe_k, pe_v):
    half = NSA_CMP_STRIDE
    wk = jnp.tile(w_ck.reshape(2, half, HEAD_DIM, HEAD_DIM), (1, 1, 1, N_HEADS))
    wv = jnp.tile(w_cv.reshape(2, half, HEAD_DIM, HEAD_DIM), (1, 1, 1, N_HEADS))
    w = jnp.zeros((2, half, LANES, 2 * LANES), _F32)
    w = w.at[:, :, :HEAD_DIM, :LANES].set(wk)
    w = w.at[:, :, HEAD_DIM:2 * HEAD_DIM, LANES:].set(wv)
    pe = jnp.zeros((2, half, LANES), _F32)
    pe = pe.at[:, :, :HEAD_DIM].set(pe_k.reshape(2, half, HEAD_DIM))
    pe = pe.at[:, :, HEAD_DIM:2 * HEAD_DIM].set(pe_v.reshape(2, half, HEAD_DIM))
    pe = jnp.broadcast_to(pe.reshape(2, 1, half * LANES), (2, 8, half * LANES))
    return w.reshape(2, half * LANES, 2 * LANES).astype(_BF), pe.astype(_BF)


def _key_features():
    c = np.arange(KEY_TILE)
    out = np.zeros((2, KEY_TILE, LANES), np.float32)
    for n, blk_len in enumerate((NSA_SEL_LEN, MOBA_BLOCK)):
        out[n, c, c // blk_len] = 1.0
        out[n, :, F_HI] = c // 256
        out[n, :, F_LO] = c % 256
        out[n, :, F_ONE] = 1.0
    return jnp.asarray(out, _BF)


def _importance_matrix(seq):
    ncp = seq // NSA_CMP_STRIDE
    nsel = seq // NSA_SEL_LEN
    ratio = NSA_SEL_LEN // NSA_CMP_STRIDE
    overlap = np.convolve(np.ones(ratio), np.ones(NSA_CMP_LEN // NSA_CMP_STRIDE))
    n_left = (NSA_CMP_LEN - NSA_CMP_STRIDE) // NSA_CMP_STRIDE
    m = np.zeros((ncp, -(-nsel // LANES) * LANES), np.float32)
    for j in range(nsel):
        for o, c in enumerate(overlap):
            n = ratio * j + o - n_left
            if 0 <= n < ncp - 1:
                m[n, j] += c
    return jnp.asarray(m, _BF)


def kernel(x, w_in, nsa_pe_k, nsa_pe_v, nsa_w_ck, nsa_w_cv, w_br_sb, w_br_nsa, w_br_moba, w_out, ln1_g, ln1_b, peer_wq, peer_k1, peer_k2, peer_u, peer_v, ln2_g, ln2_b):
    B, S, D = x.shape
    depth = w_in.shape[0]
    T = B * S
    assert S % MOBA_BLOCK == 0 and S >= NSA_WINDOW + Q_BLOCK
    alpha = (2.0 * depth) ** 0.25
    imp_mat = _importance_matrix(S)
    feats = _key_features()
    u_all = peer_u.astype(_BF)
    vt_all = jnp.swapaxes(peer_v, 1, 2).astype(_BF)
    x2 = x.reshape(T, D)
    for l in range(depth):
        w_small, w_gate = _arrange_w_in(w_in[l], D)
        wc, pe = _arrange_compress(nsa_w_ck[l], nsa_w_cv[l], nsa_pe_k[l], nsa_pe_v[l])
        proj, ckv, ngate = _inproj(x2, w_small, feats)
        proj3 = proj.reshape(B, S, N_PROJ_GROUPS * LANES)
        gate3 = ngate.reshape(B, S, 3 * LANES)

        o_sb = _sb_attention(proj3)

        kc4, vc4 = _nsa_compress(ckv.reshape(B, S // NSA_CMP_STRIDE, NSA_CMP_STRIDE * LANES), wc, pe)
        o_cw, nsa_selm = _nsa_select(proj3, kc4, vc4, gate3, imp_mat)
        o_nsa = _blk_attention(proj3, nsa_selm, G_NQ, G_SK, G_SV, NSA_SEL_LEN, NSA_SLOPES,
                               gate3=gate3, addend=o_cw)

        mb_selm = _moba_select(proj3, _moba_mean(proj3))
        o_mb = _blk_attention(proj3, mb_selm, G_MQ, G_MK, G_MV, MOBA_BLOCK, MOBA_SLOPES)

        w_br = jnp.stack([w_br_sb[l], w_br_nsa[l], w_br_moba[l]]).astype(_BF)
        x2 = _merge(x2, o_sb.reshape(T, LANES), o_nsa.reshape(T, LANES), o_mb.reshape(T, LANES),
                    w_gate, w_br, w_out[l].astype(_BF), ln1_g[l].reshape(1, D), ln1_b[l].reshape(1, D), alpha)

        wq = peer_wq[l].reshape(D, PEER_HEADS * PEER_QDIM).astype(_BF)
        tabs = _peer_route(x2, wq, peer_k1[l].astype(_BF), peer_k2[l].astype(_BF))
        x2 = _peer_dense(x2, u_all, vt_all, l, tabs, ln2_g[l].reshape(1, D), ln2_b[l].reshape(1, D), alpha)
    return x2.reshape(B, S, D)
```

```python
import functools
import math

import numpy as np
import jax
import jax.numpy as jnp
from jax import lax
from jax.experimental import pallas as pl
from jax.experimental.pallas import tpu as pltpu

HEAD_DIM = 32
N_HEADS = 4
Q_BLOCK = 128
NSA_CMP_LEN = 32
NSA_CMP_STRIDE = 16
NSA_SEL_LEN = 64
NSA_TOP_N = 8
NSA_WINDOW = 512
MOBA_BLOCK = 256
MOBA_TOPK = 3
PEER_HEADS = 4
PEER_NKEYS = 128
PEER_TOPK = 8
PEER_QDIM = 256
N_BRANCHES = 3
LN_EPS = 1e-5

LANES = 128
VMEM_LIMIT = 48 * 1024 * 1024

_BF = jnp.bfloat16
_F32 = jnp.float32
_NEG = -1e30

_ALIBI = [2.0 ** (-8.0 * (i + 1) / (2 * N_HEADS)) for i in range(2 * N_HEADS)]
NSA_SLOPES = tuple(_ALIBI[0::2])
MOBA_SLOPES = tuple(_ALIBI[1::2])

G_SBQ, G_SBK, G_SBV, G_NQ, G_SK, G_SKF, G_MK, G_MKF, G_SV, G_WK, G_WV, G_MQ, G_MV = range(13)
N_PROJ_GROUPS = 13
KEY_TILE = 512
PEER_EXPERT_TILE = 1024
F_HI, F_LO, F_ONE = 125, 126, 127
MASK_BIAS = -1e30
SCORE_FLOOR = -5e29


def _dot(a, b):
    return jnp.dot(a.astype(_BF), b.astype(_BF), preferred_element_type=_F32)


def _dot_nt(a, b):
    return lax.dot_general(a.astype(_BF), b.astype(_BF), (((1,), (1,)), ((), ())),
                           preferred_element_type=_F32)


def _dot_split(a, b):
    hi = a.astype(_BF)
    lo = (a - hi.astype(_F32)).astype(_BF)
    return (jnp.dot(hi, b, preferred_element_type=_F32)
            + jnp.dot(lo, b, preferred_element_type=_F32))


def _iota(shape, dim):
    return lax.broadcasted_iota(jnp.int32, shape, dim)


def _lane_head():
    return jnp.right_shift(_iota((Q_BLOCK, LANES), 1), int(math.log2(HEAD_DIM)))


def _head_queries(q, lane_head):
    qf = q.astype(_F32)
    return [jnp.where(lane_head == h, qf, 0.0).astype(_BF) for h in range(N_HEADS)]


def _stack_heads(q, lane_head):
    return jnp.concatenate(_head_queries(q, lane_head), axis=0)


def _unstack_heads(acc, lane_head):
    out = jnp.zeros((Q_BLOCK, LANES), _F32)
    for h in range(N_HEADS):
        out = out + jnp.where(lane_head == h, acc[h * Q_BLOCK:(h + 1) * Q_BLOCK], 0.0)
    return out


def _first_max(x, ids, none):
    m = jnp.max(x, axis=1, keepdims=True)
    return m, jnp.min(jnp.where(x == m, ids, none), axis=1, keepdims=True)


def _params(*sem):
    return pltpu.CompilerParams(dimension_semantics=sem, vmem_limit_bytes=VMEM_LIMIT)


def _layer_norm(y, g, b):
    mu = jnp.mean(y, axis=-1, keepdims=True)
    d = y - mu
    var = jnp.mean(d * d, axis=-1, keepdims=True)
    return d * lax.rsqrt(var + LN_EPS) * g + b


def _inproj_kernel(x_ref, w_ref, feat_ref, proj_ref, ckv_ref, gate_ref):
    y = _dot(x_ref[...], w_ref[...])
    npj = N_PROJ_GROUPS * LANES
    proj_ref[...] = y[:, :npj].astype(proj_ref.dtype)
    proj_ref[:, G_SKF * LANES:(G_SKF + 1) * LANES] = feat_ref[0]
    proj_ref[:, G_MKF * LANES:(G_MKF + 1) * LANES] = feat_ref[1]
    ones = jnp.ones((y.shape[0], HEAD_DIM), proj_ref.dtype)
    proj_ref[:, G_SV * LANES + HEAD_DIM:G_SV * LANES + 2 * HEAD_DIM] = ones
    proj_ref[:, G_MV * LANES + HEAD_DIM:G_MV * LANES + 2 * HEAD_DIM] = ones
    ckv_ref[...] = y[:, npj:npj + LANES].astype(ckv_ref.dtype)
    gate_ref[...] = jax.nn.sigmoid(y[:, npj + LANES:])


def _inproj(x2, w_small, feats):
    T, D = x2.shape
    n = w_small.shape[1]
    npj = N_PROJ_GROUPS * LANES
    tm = KEY_TILE
    return pl.pallas_call(
        _inproj_kernel,
        grid=(T // tm,),
        in_specs=[pl.BlockSpec((tm, D), lambda i: (i, 0)),
                  pl.BlockSpec((D, n), lambda i: (0, 0)),
                  pl.BlockSpec((2, tm, LANES), lambda i: (0, 0, 0))],
        out_specs=[pl.BlockSpec((tm, npj), lambda i: (i, 0)),
                   pl.BlockSpec((tm, LANES), lambda i: (i, 0)),
                   pl.BlockSpec((tm, 3 * LANES), lambda i: (i, 0))],
        out_shape=[jax.ShapeDtypeStruct((T, npj), _BF),
                   jax.ShapeDtypeStruct((T, LANES), _BF),
                   jax.ShapeDtypeStruct((T, 3 * LANES), _F32)],
        compiler_params=_params("parallel"),
        name="inproj",
    )(x2, w_small, feats)


_SB_LOG_CUTOFF = -104.0


def _sb_kernel(q_ref, k_ref, v_ref, o_ref):
    i = pl.program_id(1)
    lane_head = _lane_head()
    qs = _stack_heads(q_ref[0], lane_head)
    rows = N_HEADS * Q_BLOCK
    tri = (_iota((Q_BLOCK, Q_BLOCK), 0) > _iota((Q_BLOCK, Q_BLOCK), 1)).astype(_BF)
    q_off = jnp.bitwise_and(_iota((rows, Q_BLOCK), 0), Q_BLOCK - 1)
    diag_past = _iota((rows, Q_BLOCK), 1) < q_off

    def tile(j, carry, acc, diagonal):
        start = pl.multiple_of(j * Q_BLOCK, Q_BLOCK)
        kt = k_ref[0, pl.ds(start, Q_BLOCK), :]
        vt = v_ref[0, pl.ds(start, Q_BLOCK), :]
        z = _dot_nt(qs, kt)
        ls = -(jnp.maximum(z, 0.0) + jnp.log(1.0 + jnp.exp(-jnp.abs(z))))
        if diagonal:
            ls = jnp.where(diag_past, ls, 0.0)
        excl = _dot_split(ls, tri)
        w = jnp.exp(z + ls + excl + carry)
        if diagonal:
            w = jnp.where(diag_past, w, 0.0)
        acc = acc + _dot(w, vt)
        carry = carry + jnp.sum(ls, axis=1, keepdims=True)
        return carry, acc

    carry, acc = tile(i, jnp.zeros((rows, 1), _F32), jnp.zeros((rows, LANES), _F32), True)

    def cond(st):
        j, _, _, cmax = st
        return jnp.logical_and(j >= 0, cmax > _SB_LOG_CUTOFF)

    def body(st):
        j, carry, acc, _ = st
        carry, acc = tile(j, carry, acc, False)
        return j - 1, carry, acc, jnp.max(carry)

    _, _, acc, _ = lax.while_loop(cond, body, (i - 1, carry, acc, jnp.max(carry)))
    o_ref[0] = _unstack_heads(acc, lane_head).astype(o_ref.dtype)


def _sb_attention(proj3):
    B, S, _ = proj3.shape
    return pl.pallas_call(
        _sb_kernel,
        grid=(B, S // Q_BLOCK),
        in_specs=[pl.BlockSpec((1, Q_BLOCK, LANES), lambda b, i: (b, i, G_SBQ)),
                  pl.BlockSpec((1, S, LANES), lambda b, i: (b, 0, G_SBK)),
                  pl.BlockSpec((1, S, LANES), lambda b, i: (b, 0, G_SBV))],
        out_specs=pl.BlockSpec((1, Q_BLOCK, LANES), lambda b, i: (b, i, 0)),
        out_shape=jax.ShapeDtypeStruct((B, S, LANES), _BF),
        compiler_params=_params("parallel", "arbitrary"),
        name="sb_attn",
    )(proj3, proj3, proj3)


def _nsa_compress_kernel(c_ref, w_ref, pe_ref, kc_ref, vc_ref):
    c = c_ref[0]
    a = _dot(c, w_ref[0]) + _dot(pe_ref[0], w_ref[0])[0:1]
    b = _dot(c, w_ref[1]) + _dot(pe_ref[1], w_ref[1])[0:1]
    n = a.shape[0]
    b_next = pltpu.roll(b, n - 1, 0)
    y = a + b_next
    kc_ref[0] = y[:, :LANES].astype(kc_ref.dtype)
    vc_ref[0] = y[:, LANES:].astype(vc_ref.dtype)


def _nsa_compress(ckv3, wc, pe):
    B, nchunk, width = ckv3.shape
    return pl.pallas_call(
        _nsa_compress_kernel,
        grid=(B,),
        in_specs=[pl.BlockSpec((1, nchunk, width), lambda b: (b, 0, 0)),
                  pl.BlockSpec((2, width, 2 * LANES), lambda b: (0, 0, 0)),
                  pl.BlockSpec((2, 8, width), lambda b: (0, 0, 0))],
        out_specs=[pl.BlockSpec((1, nchunk, LANES), lambda b: (b, 0, 0)),
                   pl.BlockSpec((1, nchunk, LANES), lambda b: (b, 0, 0))],
        out_shape=[jax.ShapeDtypeStruct((B, nchunk, LANES), _BF),
                   jax.ShapeDtypeStruct((B, nchunk, LANES), _BF)],
        compiler_params=_params("parallel"),
        name="nsa_compress",
    )(ckv3, wc, pe)


def _masked_exp(z, valid):
    zm = jnp.where(valid, z, MASK_BIAS)
    m = jnp.maximum(jnp.max(zm, axis=1, keepdims=True), SCORE_FLOOR)
    p = jnp.exp(zm - m)
    return p, 1.0 / jnp.maximum(jnp.sum(p, axis=1, keepdims=True), 1e-30)


def _nsa_select_kernel(q_ref, kc_ref, vc_ref, wk_ref, wv_ref, g_ref, m_ref, ocw_ref, selm_ref,
                       *, seq):
    i = pl.program_id(1)
    q0 = i * Q_BLOCK
    ncp = seq // NSA_CMP_STRIDE
    nsel = m_ref.shape[1]
    win = NSA_WINDOW + Q_BLOCK
    lane_head = _lane_head()
    qs = _stack_heads(q_ref[0], lane_head)
    head_rows = lambda a, h: a[h * Q_BLOCK:(h + 1) * Q_BLOCK]

    cend = _iota((1, ncp), 1) * NSA_CMP_STRIDE + (NSA_CMP_LEN - 1)
    valid_c = (q0 + _iota((Q_BLOCK, ncp), 0)) >= cend
    rel_c = (cend - q0).astype(_F32)
    zc = _dot_nt(qs, kc_ref[0])
    pg = jnp.zeros((Q_BLOCK, ncp), _F32)
    pcs = []
    for h in range(N_HEADS):
        p, inv = _masked_exp(head_rows(zc, h) + NSA_SLOPES[h] * rel_c, valid_c)
        p = p * inv
        pcs.append(p.astype(_BF))
        pg = pg + p
    o_cmp = _unstack_heads(_dot(jnp.concatenate(pcs, axis=0), vc_ref[0]), lane_head)

    imp = _dot_split(pg, m_ref[...])
    blk = _iota((Q_BLOCK, nsel), 1)
    cur = jnp.right_shift(q0 + _iota((Q_BLOCK, nsel), 0), int(math.log2(NSA_SEL_LEN)))
    forced = (blk == 0) | (blk == cur) | (blk == cur - 1)
    x = jnp.where(blk > cur, -jnp.inf, jnp.where(forced, jnp.inf, imp))
    blkf = blk.astype(_F32)
    sel = jnp.zeros((Q_BLOCK, nsel), _F32)
    for _ in range(min(NSA_TOP_N, seq // NSA_SEL_LEN)):
        _, idx = _first_max(x, blkf, float(nsel))
        hit = blkf == idx
        sel = jnp.where(hit, 1.0, sel)
        x = jnp.where(hit, -jnp.inf, x)
    selm_ref[0] = sel.astype(selm_ref.dtype)

    ws = pl.multiple_of(jnp.maximum(q0 - NSA_WINDOW, 0), Q_BLOCK)
    kw = wk_ref[0, pl.ds(ws, win), :]
    vw = wv_ref[0, pl.ds(ws, win), :]
    kpos = ws + _iota((1, win), 1)
    dw = (q0 + _iota((Q_BLOCK, win), 0)) - kpos
    valid_w = (dw >= 0) & (dw < NSA_WINDOW)
    rel_w = (kpos - q0).astype(_F32)
    zw = _dot_nt(qs, kw)
    pws, invs = [], []
    for h in range(N_HEADS):
        p, inv = _masked_exp(head_rows(zw, h) + NSA_SLOPES[h] * rel_w, valid_w)
        pws.append(p.astype(_BF))
        invs.append(jnp.broadcast_to(inv, (Q_BLOCK, LANES)))
    o_win = _unstack_heads(_dot(jnp.concatenate(pws, axis=0), vw) * jnp.concatenate(invs, axis=0), lane_head)

    g = g_ref[0]
    ocw_ref[0] = g[:, :LANES] * o_cmp + g[:, 2 * LANES:] * o_win


def _nsa_select(proj3, kc4, vc4, gate3, imp_mat):
    B, S, _ = proj3.shape
    ncp = S // NSA_CMP_STRIDE
    nsel = imp_mat.shape[1]
    return pl.pallas_call(
        functools.partial(_nsa_select_kernel, seq=S),
        grid=(B, S // Q_BLOCK),
        in_specs=[pl.BlockSpec((1, Q_BLOCK, LANES), lambda b, i: (b, i, G_NQ)),
                  pl.BlockSpec((1, ncp, LANES), lambda b, i: (b, 0, 0)),
                  pl.BlockSpec((1, ncp, LANES), lambda b, i: (b, 0, 0)),
                  pl.BlockSpec((1, S, LANES), lambda b, i: (b, 0, G_WK)),
                  pl.BlockSpec((1, S, LANES), lambda b, i: (b, 0, G_WV)),
                  pl.BlockSpec((1, Q_BLOCK, 3 * LANES), lambda b, i: (b, i, 0)),
                  pl.BlockSpec((ncp, nsel), lambda b, i: (0, 0))],
        out_specs=[pl.BlockSpec((1, Q_BLOCK, LANES), lambda b, i: (b, i, 0)),
                   pl.BlockSpec((1, Q_BLOCK, nsel), lambda b, i: (b, i, 0))],
        out_shape=[jax.ShapeDtypeStruct((B, S, LANES), _F32),
                   jax.ShapeDtypeStruct((B, S, nsel), _BF)],
        compiler_params=_params("parallel", "arbitrary"),
        name="nsa_select",
    )(proj3, kc4, vc4, proj3, proj3, gate3, imp_mat)


def _moba_mean_kernel(k_ref, o_ref, *, nb):
    k = k_ref[0].astype(_F32)
    o_ref[0] = jnp.zeros(o_ref.shape[1:], _F32)
    o_ref[0, :nb, :] = jnp.mean(k.reshape(nb, MOBA_BLOCK, LANES), axis=1)


def _moba_mean(proj3):
    B, S, _ = proj3.shape
    nb = S // MOBA_BLOCK
    return pl.pallas_call(
        functools.partial(_moba_mean_kernel, nb=nb),
        grid=(B,),
        in_specs=[pl.BlockSpec((1, S, LANES), lambda b: (b, 0, G_MK))],
        out_specs=pl.BlockSpec((1, LANES, LANES), lambda b: (b, 0, 0)),
        out_shape=jax.ShapeDtypeStruct((B, LANES, LANES), _F32),
        compiler_params=_params("parallel"),
        name="moba_mean",
    )(proj3)


def _moba_select_kernel(q_ref, km_ref, selm_ref, *, nb):
    i = pl.program_id(1)
    q0 = i * Q_BLOCK
    sg = _dot_nt(q_ref[0], km_ref[0])
    blk = _iota((Q_BLOCK, LANES), 1)
    cur = jnp.right_shift(q0, int(math.log2(MOBA_BLOCK)))
    x = jnp.where(blk < cur, sg, -jnp.inf)
    blkf = blk.astype(_F32)
    sel = jnp.where(blk == cur, 1.0, 0.0)
    for _ in range(min(MOBA_TOPK, nb)):
        m, idx = _first_max(x, blkf, float(LANES))
        hit = blkf == idx
        sel = jnp.where(hit & (m > -jnp.inf), 1.0, sel)
        x = jnp.where(hit, -jnp.inf, x)
    selm_ref[0] = sel.astype(selm_ref.dtype)


def _moba_select(proj3, kmean):
    B, S, _ = proj3.shape
    nb = S // MOBA_BLOCK
    assert nb <= LANES
    return pl.pallas_call(
        functools.partial(_moba_select_kernel, nb=nb),
        grid=(B, S // Q_BLOCK),
        in_specs=[pl.BlockSpec((1, Q_BLOCK, LANES), lambda b, i: (b, i, G_MQ)),
                  pl.BlockSpec((1, LANES, LANES), lambda b, i: (b, 0, 0))],
        out_specs=pl.BlockSpec((1, Q_BLOCK, LANES), lambda b, i: (b, i, 0)),
        out_shape=jax.ShapeDtypeStruct((B, S, LANES), _BF),
        compiler_params=_params("parallel", "arbitrary"),
        name="moba_select",
    )(proj3, kmean)


def _blk_attn_kernel(*refs, blk_len, slopes, nblk, seq, gated):
    if gated:
        q_ref, kx_ref, v_ref, selm_ref, g_ref, add_ref, o_ref = refs[:7]
    else:
        q_ref, kx_ref, v_ref, selm_ref, o_ref = refs[:5]
    qx_ref, m_ref, acc_ref, za_ref, zb_ref, flag_ref, list_ref = refs[-7:]
    tile = KEY_TILE
    rows = N_HEADS * Q_BLOCK
    bpt = tile // blk_len
    max_tiles = seq // tile
    i = pl.program_id(1)
    q0 = i * Q_BLOCK
    lane_head = _lane_head()
    lane = _iota((Q_BLOCK, LANES), 1)

    qx_ref[:, :LANES] = _stack_heads(q_ref[0], lane_head)
    m_ref[...] = jnp.full((rows, LANES), SCORE_FLOOR, _F32)
    acc_ref[...] = jnp.zeros((rows, LANES), _F32)
    lane1 = _iota((1, LANES), 1)
    fixed_lanes = [jnp.where(lane1 == F_HI, slopes[h] * 256.0, jnp.where(lane1 == F_LO, slopes[h], 0.0))
                   for h in range(N_HEADS)]
    dist_lanes = [jnp.where(lane1 == F_ONE, slopes[h] * Q_BLOCK, 0.0) for h in range(N_HEADS)]

    selm = selm_ref[0].astype(_F32)
    sneg = (1.0 - selm) * MASK_BIAS
    n_tiles = jnp.right_shift(q0 + Q_BLOCK + tile - 1, int(math.log2(tile)))
    dead_bit = 1 << 20

    def scores(entry, z_ref):
        j = jnp.bitwise_and(entry, dead_bit - 1)
        k0 = pl.multiple_of(j * tile, tile)
        first = j * bpt
        if nblk > LANES:
            half = jnp.right_shift(first, int(math.log2(LANES)))
            base = sneg[:, :LANES]
            for c in range(1, nblk // LANES):
                base = jnp.where(half == c, sneg[:, c * LANES:(c + 1) * LANES], base)
        else:
            base = sneg
        shift = jnp.bitwise_and(LANES - jnp.bitwise_and(first, LANES - 1), LANES - 1)
        tile_mask = pltpu.roll(base, shift, 1)
        tile_mask = jnp.where(entry >= dead_bit, MASK_BIAS, tile_mask)
        tile_mask = jnp.where(lane >= F_HI, 0.0, tile_mask)
        dist = (j * (tile // Q_BLOCK) - i).astype(_F32)
        for h in range(N_HEADS):
            ext = tile_mask + (fixed_lanes[h] + dist * dist_lanes[h])
            qx_ref[h * Q_BLOCK:(h + 1) * Q_BLOCK, LANES:] = ext.astype(_BF)
        kx = kx_ref[0, pl.ds(k0, tile), :]
        z_ref[...] = lax.dot_general(qx_ref[...], kx, (((1,), (1,)), ((), ())),
                                     preferred_element_type=_F32)

    def absorb(z_ref, entry, diagonal):
        j = jnp.bitwise_and(entry, dead_bit - 1)
        k0 = pl.multiple_of(j * tile, tile)
        vt = v_ref[0, pl.ds(k0, tile), :]
        z = z_ref[...]
        if diagonal:
            q_off = jnp.bitwise_and(_iota((rows, tile), 0), Q_BLOCK - 1)
            z = jnp.where(_iota((rows, tile), 1) - q_off <= q0 - k0, z, MASK_BIAS)
        m_old = m_ref[...]
        m_new = jnp.maximum(m_old, jnp.max(z, axis=1, keepdims=True))
        p = jnp.exp(z - jnp.concatenate([m_new] * (tile // LANES), axis=1))
        a = jnp.exp(m_old - m_new)
        acc_ref[...] = a * acc_ref[...] + _dot(p, vt)
        m_ref[...] = m_new

    scores(n_tiles - 1, za_ref)

    any_q = jnp.broadcast_to(jnp.max(selm, axis=0, keepdims=True), (8, nblk))
    group = (jnp.right_shift(_iota((nblk, LANES), 0), int(math.log2(bpt))) == _iota((nblk, LANES), 1))
    tile_hits = _dot(any_q, jnp.where(group, 1.0, 0.0))
    for t in range(max_tiles):
        flag_ref[t] = (tile_hits[0, t] > 0.5).astype(jnp.int32)

    def scan(t, n):
        @pl.when(flag_ref[t] > 0)
        def _():
            list_ref[n] = t
        return n + (flag_ref[t] > 0).astype(jnp.int32)

    n_before = lax.fori_loop(0, n_tiles - 1, scan, 0)
    list_ref[n_before] = dead_bit
    n_pairs = jnp.right_shift(n_before, 1)

    def body(kk, c):
        scores(list_ref[2 * kk + 1], za_ref)
        absorb(zb_ref, list_ref[2 * kk], False)
        scores(list_ref[2 * kk + 2], zb_ref)
        absorb(za_ref, list_ref[2 * kk + 1], False)
        return c

    scores(list_ref[0], zb_ref)
    absorb(za_ref, n_tiles - 1, True)
    lax.fori_loop(0, n_pairs, body, 0)
    absorb(zb_ref, list_ref[2 * n_pairs], False)

    acc = acc_ref[...]
    total = pltpu.roll(acc, LANES - HEAD_DIM, 1)
    o = acc * (1.0 / jnp.maximum(total, 1e-30))
    out = jnp.zeros((Q_BLOCK, LANES), _F32)
    for h in range(N_HEADS):
        o_h = o[h * Q_BLOCK:(h + 1) * Q_BLOCK]
        if h:
            o_h = pltpu.roll(o_h, h * HEAD_DIM, 1)
        out = out + jnp.where(lane_head == h, o_h, 0.0)
    if gated:
        out = add_ref[0] + g_ref[0] * out
    o_ref[0] = out.astype(o_ref.dtype)


def _blk_attention(proj3, selm, gq, gkx, gv, blk_len, slopes, gate3=None, addend=None):
    B, S, _ = proj3.shape
    nblk = selm.shape[-1]
    assert S % KEY_TILE == 0 and gkx % 2 == 0 and nblk % LANES == 0
    gated = gate3 is not None
    rows = N_HEADS * Q_BLOCK
    in_specs = [pl.BlockSpec((1, Q_BLOCK, LANES), lambda b, i: (b, i, gq)),
                pl.BlockSpec((1, S, 2 * LANES), lambda b, i: (b, 0, gkx // 2)),
                pl.BlockSpec((1, S, LANES), lambda b, i: (b, 0, gv)),
                pl.BlockSpec((1, Q_BLOCK, nblk), lambda b, i: (b, i, 0))]
    args = [proj3, proj3, proj3, selm]
    if gated:
        in_specs += [pl.BlockSpec((1, Q_BLOCK, LANES), lambda b, i: (b, i, 1)),
                     pl.BlockSpec((1, Q_BLOCK, LANES), lambda b, i: (b, i, 0))]
        args += [gate3, addend]
    return pl.pallas_call(
        functools.partial(_blk_attn_kernel, blk_len=blk_len, slopes=slopes, nblk=nblk, seq=S,
                          gated=gated),
        grid=(B, S // Q_BLOCK),
        in_specs=in_specs,
        out_specs=pl.BlockSpec((1, Q_BLOCK, LANES), lambda b, i: (b, i, 0)),
        out_shape=jax.ShapeDtypeStruct((B, S, LANES), _BF),
        scratch_shapes=[pltpu.VMEM((rows, 2 * LANES), _BF),
                        pltpu.VMEM((rows, LANES), _F32),
                        pltpu.VMEM((rows, LANES), _F32),
                        pltpu.VMEM((rows, KEY_TILE), _F32),
                        pltpu.VMEM((rows, KEY_TILE), _F32),
                        pltpu.SMEM((S // KEY_TILE,), jnp.int32),
                        pltpu.SMEM((S // KEY_TILE + 2,), jnp.int32)],
        compiler_params=_params("parallel", "arbitrary"),
        name="blk_attn_%d" % blk_len,
    )(*args)


def _merge_kernel(x_ref, osb_ref, onsa_ref, omb_ref, wg_ref, wbr_ref, wo_ref, lg_ref, lb_ref, o_ref,
                  *, alpha):
    x = x_ref[...]
    d = x.shape[1]
    gates = jax.nn.sigmoid(_dot(x, wg_ref[...]))
    mix = (gates[:, :d] * _dot(osb_ref[...], wbr_ref[0])
           + gates[:, d:2 * d] * _dot(onsa_ref[...], wbr_ref[1])
           + gates[:, 2 * d:] * _dot(omb_ref[...], wbr_ref[2]))
    y = alpha * x + _dot(mix, wo_ref[...])
    o_ref[...] = _layer_norm(y, lg_ref[...], lb_ref[...])


def _merge(x2, o_sb, o_nsa, o_mb, w_gate, w_br, w_out, ln_g, ln_b, alpha, tm=256):
    T, D = x2.shape
    row = lambda i: (i, 0)
    fixed2 = lambda i: (0, 0)
    return pl.pallas_call(
        functools.partial(_merge_kernel, alpha=alpha),
        grid=(T // tm,),
        in_specs=[pl.BlockSpec((tm, D), row),
                  pl.BlockSpec((tm, LANES), row),
                  pl.BlockSpec((tm, LANES), row),
                  pl.BlockSpec((tm, LANES), row),
                  pl.BlockSpec((D, N_BRANCHES * D), fixed2),
                  pl.BlockSpec((N_BRANCHES, LANES, D), lambda i: (0, 0, 0)),
                  pl.BlockSpec((D, D), fixed2),
                  pl.BlockSpec((1, D), fixed2),
                  pl.BlockSpec((1, D), fixed2)],
        out_specs=pl.BlockSpec((tm, D), row),
        out_shape=jax.ShapeDtypeStruct((T, D), _F32),
        compiler_params=_params("parallel"),
        name="merge",
    )(x2, o_sb, o_nsa, o_mb, w_gate, w_br, w_out, ln_g, ln_b)


_NOT_RETRIEVED = 99.0


def _top_rows(s, k):
    n = s.shape[0]
    rows = _iota(s.shape, 0).astype(_F32)
    vals, ids = [], []
    for _ in range(k):
        m = jnp.max(s, axis=0, keepdims=True)
        idx = jnp.min(jnp.where(s == m, rows, float(n)), axis=0, keepdims=True)
        vals.append(m)
        ids.append(idx)
        s = jnp.where(rows == idx, -jnp.inf, s)
    return vals, ids


def _peer_route_kernel(x_ref, wq_ref, k1_ref, k2_ref, fa_ref, qb_ref, e1_ref, e2_ref):
    half = PEER_QDIM // 2
    qf = _dot(x_ref[...], wq_ref[...]).astype(_BF)
    tm = qf.shape[0]
    rows = _iota((PEER_NKEYS, tm), 0).astype(_F32)
    rows_k = _iota((PEER_TOPK, tm), 0)
    ncand = PEER_TOPK * PEER_TOPK
    pos = _iota((ncand, tm), 0).astype(_F32)
    for h in range(PEER_HEADS):
        s1 = _dot_nt(k1_ref[h], qf[:, h * PEER_QDIM:h * PEER_QDIM + half])
        s2 = _dot_nt(k2_ref[h], qf[:, h * PEER_QDIM + half:(h + 1) * PEER_QDIM])
        v1, i1 = _top_rows(s1, PEER_TOPK)
        v2, i2 = _top_rows(s2, PEER_TOPK)
        v2_all = jnp.zeros((PEER_TOPK, tm), _F32)
        for qi in range(PEER_TOPK):
            v2_all = jnp.where(rows_k == qi, v2[qi], v2_all)
        cand = jnp.concatenate([v1[p] + v2_all for p in range(PEER_TOPK)], axis=0)
        c = cand
        pickf = jnp.zeros(cand.shape, _F32)
        for _ in range(PEER_TOPK):
            m = jnp.max(c, axis=0, keepdims=True)
            idx = jnp.min(jnp.where(c == m, pos, float(ncand)), axis=0, keepdims=True)
            hit = pos == idx
            pickf = jnp.where(hit, 1.0, pickf)
            c = jnp.where(hit, -jnp.inf, c)
        cmax = v1[0] + v2[0]
        z = jnp.sum(pickf * jnp.exp(cand - cmax), axis=0, keepdims=True)
        fa = jnp.zeros((PEER_NKEYS, tm), _F32)
        qb = jnp.full((PEER_NKEYS, tm), _NOT_RETRIEVED, _F32)
        for p in range(PEER_TOPK):
            count = jnp.sum(pickf[p * PEER_TOPK:(p + 1) * PEER_TOPK], axis=0, keepdims=True)
            fa = jnp.where(rows == i1[p], count, fa)
            qb = jnp.where(rows == i2[p], float(p), qb)
        fa_ref[h] = fa
        qb_ref[h] = qb.astype(qb_ref.dtype)
        e1_ref[h] = jnp.exp(s1 - v1[0]) * (1.0 / z)
        e2_ref[h] = jnp.exp(s2 - v2[0]).astype(e2_ref.dtype)


def _peer_route(x2, wq, k1, k2, tm=256):
    T, D = x2.shape
    tab = jax.ShapeDtypeStruct((PEER_HEADS, PEER_NKEYS, T), _F32)
    tab_bf = jax.ShapeDtypeStruct((PEER_HEADS, PEER_NKEYS, T), _BF)
    tab_spec = pl.BlockSpec((PEER_HEADS, PEER_NKEYS, tm), lambda i: (0, 0, i))
    return pl.pallas_call(
        _peer_route_kernel,
        grid=(T // tm,),
        in_specs=[pl.BlockSpec((tm, D), lambda i: (i, 0)),
                  pl.BlockSpec(wq.shape, lambda i: (0, 0)),
                  pl.BlockSpec(k1.shape, lambda i: (0, 0, 0)),
                  pl.BlockSpec(k2.shape, lambda i: (0, 0, 0))],
        out_specs=[tab_spec] * 4,
        out_shape=[tab, tab_bf, tab, tab_bf],
        compiler_params=_params("parallel"),
        name="peer_route",
    )(x2, wq, k1, k2)


def _gelu_tanh(s):
    c1 = math.sqrt(2.0 / math.pi)
    k1 = jnp.asarray(c1, s.dtype)
    k2 = jnp.asarray(c1 * 0.044715, s.dtype)
    inner = s * (k1 + k2 * (s * s))
    half = jnp.asarray(0.5, s.dtype) * s
    return half + half * jnp.tanh(inner)


def _peer_dense_kernel(x_ref, u_ref, vt_ref, fa_ref, qb_ref, e1_ref, e2_ref, lg_ref, lb_ref, o_ref,
                       acc_ref, xb_ref, s_ref, c_ref, *, alpha, te):
    j = pl.program_id(1)
    n_tiles = pl.num_programs(1) - 2

    @pl.when(j == 0)
    def _():
        acc_ref[...] = jnp.zeros_like(acc_ref)
        s_ref[...] = jnp.zeros_like(s_ref)
        c_ref[...] = jnp.zeros_like(c_ref)
        xb_ref[...] = x_ref[...].astype(_BF)

    jg = jnp.clip(j - 1, 0, n_tiles - 1)

    cur = j % 2
    tm = s_ref.shape[2]
    tc = tm // 2
    for ck in range(tm // tc):
        cols = slice(ck * tc, (ck + 1) * tc)
        acc_ref[:, cols] += jnp.dot(vt_ref[...], c_ref[1 - cur, :, cols], preferred_element_type=_F32)
        s = s_ref[1 - cur, :, cols]
        act = _gelu_tanh(s)
        for r in range(te // PEER_NKEYS):
            a = jg * (te // PEER_NKEYS) + r
            gate = jnp.zeros((PEER_NKEYS, tc), _BF)
            for h in range(PEER_HEADS):
                fa = fa_ref[h, pl.ds(a, 1), cols].astype(_BF)
                e1 = e1_ref[h, pl.ds(a, 1), cols].astype(_BF)
                gate = gate + e1 * jnp.where(qb_ref[h, :, cols] < fa, e2_ref[h, :, cols], jnp.zeros((), _BF))
            c_ref[cur, r * PEER_NKEYS:(r + 1) * PEER_NKEYS, cols] = (
                gate * act[r * PEER_NKEYS:(r + 1) * PEER_NKEYS])
        s_ref[cur, :, cols] = _dot_nt(u_ref[...], xb_ref[cols, :]).astype(_BF)

    @pl.when(j == n_tiles + 1)
    def _():
        y = alpha * x_ref[...] + acc_ref[...].T
        o_ref[...] = _layer_norm(y, lg_ref[...], lb_ref[...])


def _peer_dense(x2, u_all, vt_all, layer, tabs, ln_g, ln_b, alpha, tm=512):
    T, D = x2.shape
    n_tiles, te = vt_all.shape[1], vt_all.shape[3]
    tab_spec = pl.BlockSpec((PEER_HEADS, PEER_NKEYS, tm), lambda i, j: (0, 0, i))
    return pl.pallas_call(
        functools.partial(_peer_dense_kernel, alpha=alpha, te=te),
        grid=(T // tm, n_tiles + 2),
        in_specs=[pl.BlockSpec((tm, D), lambda i, j: (i, 0)),
                  pl.BlockSpec((None, te, D), lambda i, j: (layer, jnp.minimum(j, n_tiles - 1), 0)),
                  pl.BlockSpec((None, None, D, te), lambda i, j: (layer, jnp.clip(j - 2, 0, n_tiles - 1), 0, 0)),
                  tab_spec, tab_spec, tab_spec, tab_spec,
                  pl.BlockSpec((1, D), lambda i, j: (0, 0)),
                  pl.BlockSpec((1, D), lambda i, j: (0, 0))],
        out_specs=pl.BlockSpec((tm, D), lambda i, j: (i, 0)),
        out_shape=jax.ShapeDtypeStruct((T, D), _F32),
        scratch_shapes=[pltpu.VMEM((D, tm), _F32), pltpu.VMEM((tm, D), _BF),
                        pltpu.VMEM((2, te, tm), _BF), pltpu.VMEM((2, te, tm), _BF)],
        compiler_params=_params("parallel", "arbitrary"),
        name="peer_dense",
    )(x2, u_all, vt_all, *tabs, ln_g, ln_b)


def _in_widths(d_model):
    w = N_HEADS * HEAD_DIM
    return (w, w, w, w) + (HEAD_DIM,) * 6 + (3 * N_HEADS, w, HEAD_DIM, HEAD_DIM, N_BRANCHES * d_model)


def _arrange_w_in(w_in, d_model):
    off = np.concatenate([[0], np.cumsum(_in_widths(d_model))])
    scale = HEAD_DIM ** -0.5
    seg = lambda n: w_in[:, off[n]:off[n + 1]]
    rep = lambda n: jnp.tile(seg(n), (1, N_HEADS))
    gate_cols = np.array([off[10] + h * 3 + c for c in range(3) for h in range(N_HEADS)
                          for _ in range(HEAD_DIM)])
    ckv = jnp.concatenate([seg(4), seg(5), jnp.zeros((w_in.shape[0], LANES - 2 * HEAD_DIM), w_in.dtype)], axis=1)
    blank = jnp.zeros((w_in.shape[0], LANES), w_in.dtype)
    once = lambda n: jnp.concatenate([seg(n), blank[:, HEAD_DIM:]], axis=1)
    groups = [seg(0) * scale, seg(1), seg(2), seg(3) * scale, rep(6), blank, rep(12), blank,
              once(7), rep(8), rep(9), seg(11) * scale, once(13), ckv, w_in[:, gate_cols]]
    return jnp.concatenate(groups, axis=1).astype(_BF), seg(14).astype(_BF)


def _arrange_compress(w_ck, w_cv, pe_k, pe_v):
    half = NSA_CMP_STRIDE
    wk = jnp.tile(w_ck.reshape(2, half, HEAD_DIM, HEAD_DIM), (1, 1, 1, N_HEADS))
    wv = jnp.tile(w_cv.reshape(2, half, HEAD_DIM, HEAD_DIM), (1, 1, 1, N_HEADS))
    w = jnp.zeros((2, half, LANES, 2 * LANES), _F32)
    w = w.at[:, :, :HEAD_DIM, :LANES].set(wk)
    w = w.at[:, :, HEAD_DIM:2 * HEAD_DIM, LANES:].set(wv)
    pe = jnp.zeros((2, half, LANES), _F32)
    pe = pe.at[:, :, :HEAD_DIM].set(pe_k.reshape(2, half, HEAD_DIM))
    pe = pe.at[:, :, HEAD_DIM:2 * HEAD_DIM].set(pe_v.reshape(2, half, HEAD_DIM))
    pe = jnp.broadcast_to(pe.reshape(2, 1, half * LANES), (2, 8, half * LANES))
    return w.reshape(2, half * LANES, 2 * LANES).astype(_BF), pe.astype(_BF)


def _key_features():
    c = np.arange(KEY_TILE)
    out = np.zeros((2, KEY_TILE, LANES), np.float32)
    for n, blk_len in enumerate((NSA_SEL_LEN, MOBA_BLOCK)):
        out[n, c, c // blk_len] = 1.0
        out[n, :, F_HI] = c // 256
        out[n, :, F_LO] = c % 256
        out[n, :, F_ONE] = 1.0
    return jnp.asarray(out, _BF)


def _importance_matrix(seq):
    ncp = seq // NSA_CMP_STRIDE
    nsel = seq // NSA_SEL_LEN
    ratio = NSA_SEL_LEN // NSA_CMP_STRIDE
    overlap = np.convolve(np.ones(ratio), np.ones(NSA_CMP_LEN // NSA_CMP_STRIDE))
    n_left = (NSA_CMP_LEN - NSA_CMP_STRIDE) // NSA_CMP_STRIDE
    m = np.zeros((ncp, -(-nsel // LANES) * LANES), np.float32)
    for j in range(nsel):
        for o, c in enumerate(overlap):
            n = ratio * j + o - n_left
            if 0 <= n < ncp - 1:
                m[n, j] += c
    return jnp.asarray(m, _BF)


def kernel(x, w_in, nsa_pe_k, nsa_pe_v, nsa_w_ck, nsa_w_cv, w_br_sb, w_br_nsa, w_br_moba, w_out, ln1_g, ln1_b, peer_wq, peer_k1, peer_k2, peer_u, peer_v, ln2_g, ln2_b):
    B, S, D = x.shape
    depth = w_in.shape[0]
    T = B * S
    assert S % MOBA_BLOCK == 0 and S >= NSA_WINDOW + Q_BLOCK
    alpha = (2.0 * depth) ** 0.25
    imp_mat = _importance_matrix(S)
    feats = _key_features()
    u_all = peer_u.astype(_BF)
    te = PEER_EXPERT_TILE
    vt_all = jnp.swapaxes(peer_v.reshape(depth, -1, te, D), 2, 3).astype(_BF)
    x2 = x.reshape(T, D)
    for l in range(depth):
        w_small, w_gate = _arrange_w_in(w_in[l], D)
        wc, pe = _arrange_compress(nsa_w_ck[l], nsa_w_cv[l], nsa_pe_k[l], nsa_pe_v[l])
        proj, ckv, ngate = _inproj(x2, w_small, feats)
        proj3 = proj.reshape(B, S, N_PROJ_GROUPS * LANES)
        gate3 = ngate.reshape(B, S, 3 * LANES)

        o_sb = _sb_attention(proj3)

        kc4, vc4 = _nsa_compress(ckv.reshape(B, S // NSA_CMP_STRIDE, NSA_CMP_STRIDE * LANES), wc, pe)
        o_cw, nsa_selm = _nsa_select(proj3, kc4, vc4, gate3, imp_mat)
        o_nsa = _blk_attention(proj3, nsa_selm, G_NQ, G_SK, G_SV, NSA_SEL_LEN, NSA_SLOPES,
                               gate3=gate3, addend=o_cw)

        mb_selm = _moba_select(proj3, _moba_mean(proj3))
        o_mb = _blk_attention(proj3, mb_selm, G_MQ, G_MK, G_MV, MOBA_BLOCK, MOBA_SLOPES)

        w_br = jnp.stack([w_br_sb[l], w_br_nsa[l], w_br_moba[l]]).astype(_BF)
        x2 = _merge(x2, o_sb.reshape(T, LANES), o_nsa.reshape(T, LANES), o_mb.reshape(T, LANES),
                    w_gate, w_br, w_out[l].astype(_BF), ln1_g[l].reshape(1, D), ln1_b[l].reshape(1, D), alpha)

        wq = peer_wq[l].reshape(D, PEER_HEADS * PEER_QDIM).astype(_BF)
        tabs = _peer_route(x2, wq, peer_k1[l].astype(_BF), peer_k2[l].astype(_BF))
        x2 = _peer_dense(x2, u_all, vt_all, l, tabs, ln2_g[l].reshape(1, D), ln2_b[l].reshape(1, D), alpha)
    return x2.reshape(B, S, D)
```

```python
import functools
import math

import numpy as np
import jax
import jax.numpy as jnp
from jax import lax
from jax.experimental import pallas as pl
from jax.experimental.pallas import tpu as pltpu

HEAD_DIM = 32
N_HEADS = 4
Q_BLOCK = 128
NSA_CMP_LEN = 32
NSA_CMP_STRIDE = 16
NSA_SEL_LEN = 64
NSA_TOP_N = 8
NSA_WINDOW = 512
MOBA_BLOCK = 256
MOBA_TOPK = 3
PEER_HEADS = 4
PEER_NKEYS = 128
PEER_TOPK = 8
PEER_QDIM = 256
N_BRANCHES = 3
LN_EPS = 1e-5

LANES = 128
VMEM_LIMIT = 48 * 1024 * 1024

_BF = jnp.bfloat16
_F32 = jnp.float32
_NEG = -1e30

_ALIBI = [2.0 ** (-8.0 * (i + 1) / (2 * N_HEADS)) for i in range(2 * N_HEADS)]
NSA_SLOPES = tuple(_ALIBI[0::2])
MOBA_SLOPES = tuple(_ALIBI[1::2])

G_SBQ, G_SBK, G_SBV, G_NQ, G_SK, G_SKF, G_MK, G_MKF, G_SV, G_WK, G_WV, G_MQ, G_MV = range(13)
N_PROJ_GROUPS = 13
NSA_KEY_TILE = 512
MOBA_KEY_TILE = 1024
KEY_TILE = max(NSA_KEY_TILE, MOBA_KEY_TILE)
PEER_EXPERT_TILE = 1024
F_HI, F_LO, F_ONE = 125, 126, 127
MASK_BIAS = -1e30
SCORE_FLOOR = -5e29


def _dot(a, b):
    return jnp.dot(a.astype(_BF), b.astype(_BF), preferred_element_type=_F32)


def _dot_nt(a, b):
    return lax.dot_general(a.astype(_BF), b.astype(_BF), (((1,), (1,)), ((), ())),
                           preferred_element_type=_F32)


def _dot_split(a, b):
    hi = a.astype(_BF)
    lo = (a - hi.astype(_F32)).astype(_BF)
    return (jnp.dot(hi, b, preferred_element_type=_F32)
            + jnp.dot(lo, b, preferred_element_type=_F32))


def _iota(shape, dim):
    return lax.broadcasted_iota(jnp.int32, shape, dim)


def _lane_head():
    return jnp.right_shift(_iota((Q_BLOCK, LANES), 1), int(math.log2(HEAD_DIM)))


def _head_queries(q, lane_head):
    qf = q.astype(_F32)
    return [jnp.where(lane_head == h, qf, 0.0).astype(_BF) for h in range(N_HEADS)]


def _stack_heads(q, lane_head):
    return jnp.concatenate(_head_queries(q, lane_head), axis=0)


def _unstack_heads(acc, lane_head):
    out = jnp.zeros((Q_BLOCK, LANES), _F32)
    for h in range(N_HEADS):
        out = out + jnp.where(lane_head == h, acc[h * Q_BLOCK:(h + 1) * Q_BLOCK], 0.0)
    return out


def _first_max(x, ids, none):
    m = jnp.max(x, axis=1, keepdims=True)
    return m, jnp.min(jnp.where(x == m, ids, none), axis=1, keepdims=True)


def _params(*sem):
    return pltpu.CompilerParams(dimension_semantics=sem, vmem_limit_bytes=VMEM_LIMIT)


def _layer_norm(y, g, b):
    mu = jnp.mean(y, axis=-1, keepdims=True)
    d = y - mu
    var = jnp.mean(d * d, axis=-1, keepdims=True)
    return d * lax.rsqrt(var + LN_EPS) * g + b


def _inproj_kernel(x_ref, w_ref, feat_ref, proj_ref, ckv_ref, gate_ref):
    y = _dot(x_ref[...], w_ref[...])
    npj = N_PROJ_GROUPS * LANES
    proj_ref[...] = y[:, :npj].astype(proj_ref.dtype)
    proj_ref[:, G_SKF * LANES:(G_SKF + 1) * LANES] = feat_ref[0]
    proj_ref[:, G_MKF * LANES:(G_MKF + 1) * LANES] = feat_ref[1]
    ones = jnp.ones((y.shape[0], HEAD_DIM), proj_ref.dtype)
    proj_ref[:, G_SV * LANES + HEAD_DIM:G_SV * LANES + 2 * HEAD_DIM] = ones
    proj_ref[:, G_MV * LANES + HEAD_DIM:G_MV * LANES + 2 * HEAD_DIM] = ones
    ckv_ref[...] = y[:, npj:npj + LANES].astype(ckv_ref.dtype)
    gate_ref[...] = jax.nn.sigmoid(y[:, npj + LANES:])


def _inproj(x2, w_small, feats):
    T, D = x2.shape
    n = w_small.shape[1]
    npj = N_PROJ_GROUPS * LANES
    tm = KEY_TILE
    return pl.pallas_call(
        _inproj_kernel,
        grid=(T // tm,),
        in_specs=[pl.BlockSpec((tm, D), lambda i: (i, 0)),
                  pl.BlockSpec((D, n), lambda i: (0, 0)),
                  pl.BlockSpec((2, tm, LANES), lambda i: (0, 0, 0))],
        out_specs=[pl.BlockSpec((tm, npj), lambda i: (i, 0)),
                   pl.BlockSpec((tm, LANES), lambda i: (i, 0)),
                   pl.BlockSpec((tm, 3 * LANES), lambda i: (i, 0))],
        out_shape=[jax.ShapeDtypeStruct((T, npj), _BF),
                   jax.ShapeDtypeStruct((T, LANES), _BF),
                   jax.ShapeDtypeStruct((T, 3 * LANES), _F32)],
        compiler_params=_params("parallel"),
        name="inproj",
    )(x2, w_small, feats)


_SB_LOG_CUTOFF = -104.0


def _sb_kernel(q_ref, k_ref, v_ref, o_ref):
    i = pl.program_id(1)
    lane_head = _lane_head()
    qs = _stack_heads(q_ref[0], lane_head)
    rows = N_HEADS * Q_BLOCK
    tri = (_iota((Q_BLOCK, Q_BLOCK), 0) > _iota((Q_BLOCK, Q_BLOCK), 1)).astype(_BF)
    q_off = jnp.bitwise_and(_iota((rows, Q_BLOCK), 0), Q_BLOCK - 1)
    diag_past = _iota((rows, Q_BLOCK), 1) < q_off

    def tile(j, carry, acc, diagonal):
        start = pl.multiple_of(j * Q_BLOCK, Q_BLOCK)
        kt = k_ref[0, pl.ds(start, Q_BLOCK), :]
        vt = v_ref[0, pl.ds(start, Q_BLOCK), :]
        z = _dot_nt(qs, kt)
        ls = -(jnp.maximum(z, 0.0) + jnp.log(1.0 + jnp.exp(-jnp.abs(z))))
        if diagonal:
            ls = jnp.where(diag_past, ls, 0.0)
        excl = _dot_split(ls, tri)
        w = jnp.exp(z + ls + excl + carry)
        if diagonal:
            w = jnp.where(diag_past, w, 0.0)
        acc = acc + _dot(w, vt)
        carry = carry + jnp.sum(ls, axis=1, keepdims=True)
        return carry, acc

    carry, acc = tile(i, jnp.zeros((rows, 1), _F32), jnp.zeros((rows, LANES), _F32), True)

    def cond(st):
        j, _, _, cmax = st
        return jnp.logical_and(j >= 0, cmax > _SB_LOG_CUTOFF)

    def body(st):
        j, carry, acc, _ = st
        carry, acc = tile(j, carry, acc, False)
        return j - 1, carry, acc, jnp.max(carry)

    _, _, acc, _ = lax.while_loop(cond, body, (i - 1, carry, acc, jnp.max(carry)))
    o_ref[0] = _unstack_heads(acc, lane_head).astype(o_ref.dtype)


def _sb_attention(proj3):
    B, S, _ = proj3.shape
    return pl.pallas_call(
        _sb_kernel,
        grid=(B, S // Q_BLOCK),
        in_specs=[pl.BlockSpec((1, Q_BLOCK, LANES), lambda b, i: (b, i, G_SBQ)),
                  pl.BlockSpec((1, S, LANES), lambda b, i: (b, 0, G_SBK)),
                  pl.BlockSpec((1, S, LANES), lambda b, i: (b, 0, G_SBV))],
        out_specs=pl.BlockSpec((1, Q_BLOCK, LANES), lambda b, i: (b, i, 0)),
        out_shape=jax.ShapeDtypeStruct((B, S, LANES), _BF),
        compiler_params=_params("parallel", "arbitrary"),
        name="sb_attn",
    )(proj3, proj3, proj3)


def _nsa_compress_kernel(c_ref, w_ref, pe_ref, kc_ref, vc_ref):
    c = c_ref[0]
    a = _dot(c, w_ref[0]) + _dot(pe_ref[0], w_ref[0])[0:1]
    b = _dot(c, w_ref[1]) + _dot(pe_ref[1], w_ref[1])[0:1]
    n = a.shape[0]
    b_next = pltpu.roll(b, n - 1, 0)
    y = a + b_next
    kc_ref[0] = y[:, :LANES].astype(kc_ref.dtype)
    vc_ref[0] = y[:, LANES:].astype(vc_ref.dtype)


def _nsa_compress(ckv3, wc, pe):
    B, nchunk, width = ckv3.shape
    return pl.pallas_call(
        _nsa_compress_kernel,
        grid=(B,),
        in_specs=[pl.BlockSpec((1, nchunk, width), lambda b: (b, 0, 0)),
                  pl.BlockSpec((2, width, 2 * LANES), lambda b: (0, 0, 0)),
                  pl.BlockSpec((2, 8, width), lambda b: (0, 0, 0))],
        out_specs=[pl.BlockSpec((1, nchunk, LANES), lambda b: (b, 0, 0)),
                   pl.BlockSpec((1, nchunk, LANES), lambda b: (b, 0, 0))],
        out_shape=[jax.ShapeDtypeStruct((B, nchunk, LANES), _BF),
                   jax.ShapeDtypeStruct((B, nchunk, LANES), _BF)],
        compiler_params=_params("parallel"),
        name="nsa_compress",
    )(ckv3, wc, pe)


def _masked_exp(z, valid):
    zm = jnp.where(valid, z, MASK_BIAS)
    m = jnp.maximum(jnp.max(zm, axis=1, keepdims=True), SCORE_FLOOR)
    p = jnp.exp(zm - m)
    return p, 1.0 / jnp.maximum(jnp.sum(p, axis=1, keepdims=True), 1e-30)


def _nsa_select_kernel(q_ref, kc_ref, vc_ref, wk_ref, wv_ref, g_ref, m_ref, ocw_ref, selm_ref,
                       *, seq):
    i = pl.program_id(1)
    q0 = i * Q_BLOCK
    ncp = seq // NSA_CMP_STRIDE
    nsel = m_ref.shape[1]
    win = NSA_WINDOW + Q_BLOCK
    lane_head = _lane_head()
    qs = _stack_heads(q_ref[0], lane_head)
    head_rows = lambda a, h: a[h * Q_BLOCK:(h + 1) * Q_BLOCK]

    cend = _iota((1, ncp), 1) * NSA_CMP_STRIDE + (NSA_CMP_LEN - 1)
    valid_c = (q0 + _iota((Q_BLOCK, ncp), 0)) >= cend
    rel_c = (cend - q0).astype(_F32)
    zc = _dot_nt(qs, kc_ref[0])
    pg = jnp.zeros((Q_BLOCK, ncp), _F32)
    pcs = []
    for h in range(N_HEADS):
        p, inv = _masked_exp(head_rows(zc, h) + NSA_SLOPES[h] * rel_c, valid_c)
        p = p * inv
        pcs.append(p.astype(_BF))
        pg = pg + p
    o_cmp = _unstack_heads(_dot(jnp.concatenate(pcs, axis=0), vc_ref[0]), lane_head)

    imp = _dot_split(pg, m_ref[...])
    blk = _iota((Q_BLOCK, nsel), 1)
    cur = jnp.right_shift(q0 + _iota((Q_BLOCK, nsel), 0), int(math.log2(NSA_SEL_LEN)))
    forced = (blk == 0) | (blk == cur) | (blk == cur - 1)
    x = jnp.where(blk > cur, -jnp.inf, jnp.where(forced, jnp.inf, imp))
    blkf = blk.astype(_F32)
    sel = jnp.zeros((Q_BLOCK, nsel), _F32)
    for _ in range(min(NSA_TOP_N, seq // NSA_SEL_LEN)):
        _, idx = _first_max(x, blkf, float(nsel))
        hit = blkf == idx
        sel = jnp.where(hit, 1.0, sel)
        x = jnp.where(hit, -jnp.inf, x)
    selm_ref[0] = sel.astype(selm_ref.dtype)

    ws = pl.multiple_of(jnp.maximum(q0 - NSA_WINDOW, 0), Q_BLOCK)
    kw = wk_ref[0, pl.ds(ws, win), :]
    vw = wv_ref[0, pl.ds(ws, win), :]
    kpos = ws + _iota((1, win), 1)
    dw = (q0 + _iota((Q_BLOCK, win), 0)) - kpos
    valid_w = (dw >= 0) & (dw < NSA_WINDOW)
    rel_w = (kpos - q0).astype(_F32)
    zw = _dot_nt(qs, kw)
    pws, invs = [], []
    for h in range(N_HEADS):
        p, inv = _masked_exp(head_rows(zw, h) + NSA_SLOPES[h] * rel_w, valid_w)
        pws.append(p.astype(_BF))
        invs.append(jnp.broadcast_to(inv, (Q_BLOCK, LANES)))
    o_win = _unstack_heads(_dot(jnp.concatenate(pws, axis=0), vw) * jnp.concatenate(invs, axis=0), lane_head)

    g = g_ref[0]
    ocw_ref[0] = g[:, :LANES] * o_cmp + g[:, 2 * LANES:] * o_win


def _nsa_select(proj3, kc4, vc4, gate3, imp_mat):
    B, S, _ = proj3.shape
    ncp = S // NSA_CMP_STRIDE
    nsel = imp_mat.shape[1]
    return pl.pallas_call(
        functools.partial(_nsa_select_kernel, seq=S),
        grid=(B, S // Q_BLOCK),
        in_specs=[pl.BlockSpec((1, Q_BLOCK, LANES), lambda b, i: (b, i, G_NQ)),
                  pl.BlockSpec((1, ncp, LANES), lambda b, i: (b, 0, 0)),
                  pl.BlockSpec((1, ncp, LANES), lambda b, i: (b, 0, 0)),
                  pl.BlockSpec((1, S, LANES), lambda b, i: (b, 0, G_WK)),
                  pl.BlockSpec((1, S, LANES), lambda b, i: (b, 0, G_WV)),
                  pl.BlockSpec((1, Q_BLOCK, 3 * LANES), lambda b, i: (b, i, 0)),
                  pl.BlockSpec((ncp, nsel), lambda b, i: (0, 0))],
        out_specs=[pl.BlockSpec((1, Q_BLOCK, LANES), lambda b, i: (b, i, 0)),
                   pl.BlockSpec((1, Q_BLOCK, nsel), lambda b, i: (b, i, 0))],
        out_shape=[jax.ShapeDtypeStruct((B, S, LANES), _F32),
                   jax.ShapeDtypeStruct((B, S, nsel), _BF)],
        compiler_params=_params("parallel", "arbitrary"),
        name="nsa_select",
    )(proj3, kc4, vc4, proj3, proj3, gate3, imp_mat)


def _moba_mean_kernel(k_ref, o_ref, *, nb):
    k = k_ref[0].astype(_F32)
    o_ref[0] = jnp.zeros(o_ref.shape[1:], _F32)
    o_ref[0, :nb, :] = jnp.mean(k.reshape(nb, MOBA_BLOCK, LANES), axis=1)


def _moba_mean(proj3):
    B, S, _ = proj3.shape
    nb = S // MOBA_BLOCK
    return pl.pallas_call(
        functools.partial(_moba_mean_kernel, nb=nb),
        grid=(B,),
        in_specs=[pl.BlockSpec((1, S, LANES), lambda b: (b, 0, G_MK))],
        out_specs=pl.BlockSpec((1, LANES, LANES), lambda b: (b, 0, 0)),
        out_shape=jax.ShapeDtypeStruct((B, LANES, LANES), _F32),
        compiler_params=_params("parallel"),
        name="moba_mean",
    )(proj3)


def _moba_select_kernel(q_ref, km_ref, selm_ref, *, nb):
    i = pl.program_id(1)
    q0 = i * Q_BLOCK
    sg = _dot_nt(q_ref[0], km_ref[0])
    blk = _iota((Q_BLOCK, LANES), 1)
    cur = jnp.right_shift(q0, int(math.log2(MOBA_BLOCK)))
    x = jnp.where(blk < cur, sg, -jnp.inf)
    blkf = blk.astype(_F32)
    sel = jnp.where(blk == cur, 1.0, 0.0)
    for _ in range(min(MOBA_TOPK, nb)):
        m, idx = _first_max(x, blkf, float(LANES))
        hit = blkf == idx
        sel = jnp.where(hit & (m > -jnp.inf), 1.0, sel)
        x = jnp.where(hit, -jnp.inf, x)
    selm_ref[0] = sel.astype(selm_ref.dtype)


def _moba_select(proj3, kmean):
    B, S, _ = proj3.shape
    nb = S // MOBA_BLOCK
    assert nb <= LANES
    return pl.pallas_call(
        functools.partial(_moba_select_kernel, nb=nb),
        grid=(B, S // Q_BLOCK),
        in_specs=[pl.BlockSpec((1, Q_BLOCK, LANES), lambda b, i: (b, i, G_MQ)),
                  pl.BlockSpec((1, LANES, LANES), lambda b, i: (b, 0, 0))],
        out_specs=pl.BlockSpec((1, Q_BLOCK, LANES), lambda b, i: (b, i, 0)),
        out_shape=jax.ShapeDtypeStruct((B, S, LANES), _BF),
        compiler_params=_params("parallel", "arbitrary"),
        name="moba_select",
    )(proj3, kmean)


def _blk_attn_kernel(*refs, blk_len, tile, slopes, nblk, seq, gated):
    if gated:
        q_ref, kx_ref, v_ref, selm_ref, g_ref, add_ref, o_ref = refs[:7]
    else:
        q_ref, kx_ref, v_ref, selm_ref, o_ref = refs[:5]
    qx_ref, m_ref, acc_ref, za_ref, zb_ref, flag_ref, list_ref = refs[-7:]
    rows = N_HEADS * Q_BLOCK
    bpt = tile // blk_len
    max_tiles = seq // tile
    i = pl.program_id(1)
    q0 = i * Q_BLOCK
    lane_head = _lane_head()
    lane = _iota((Q_BLOCK, LANES), 1)

    qx_ref[:, :LANES] = _stack_heads(q_ref[0], lane_head)
    m_ref[...] = jnp.full((rows, LANES), SCORE_FLOOR, _F32)
    acc_ref[...] = jnp.zeros((rows, LANES), _F32)
    lane1 = _iota((1, LANES), 1)
    fixed_lanes = [jnp.where(lane1 == F_HI, slopes[h] * 256.0, jnp.where(lane1 == F_LO, slopes[h], 0.0))
                   for h in range(N_HEADS)]
    dist_lanes = [jnp.where(lane1 == F_ONE, slopes[h] * Q_BLOCK, 0.0) for h in range(N_HEADS)]

    selm = selm_ref[0].astype(_F32)
    sneg = (1.0 - selm) * MASK_BIAS
    n_tiles = jnp.right_shift(q0 + Q_BLOCK + tile - 1, int(math.log2(tile)))
    dead_bit = 1 << 20

    def scores(entry, z_ref):
        j = jnp.bitwise_and(entry, dead_bit - 1)
        k0 = pl.multiple_of(j * tile, tile)
        first = j * bpt
        if nblk > LANES:
            half = jnp.right_shift(first, int(math.log2(LANES)))
            base = sneg[:, :LANES]
            for c in range(1, nblk // LANES):
                base = jnp.where(half == c, sneg[:, c * LANES:(c + 1) * LANES], base)
        else:
            base = sneg
        shift = jnp.bitwise_and(LANES - jnp.bitwise_and(first, LANES - 1), LANES - 1)
        tile_mask = pltpu.roll(base, shift, 1)
        tile_mask = jnp.where(entry >= dead_bit, MASK_BIAS, tile_mask)
        tile_mask = jnp.where(lane >= F_HI, 0.0, tile_mask)
        dist = (j * (tile // Q_BLOCK) - i).astype(_F32)
        for h in range(N_HEADS):
            ext = tile_mask + (fixed_lanes[h] + dist * dist_lanes[h])
            qx_ref[h * Q_BLOCK:(h + 1) * Q_BLOCK, LANES:] = ext.astype(_BF)
        kx = kx_ref[0, pl.ds(k0, tile), :]
        z_ref[...] = lax.dot_general(qx_ref[...], kx, (((1,), (1,)), ((), ())),
                                     preferred_element_type=_F32)

    def absorb(z_ref, entry, diagonal):
        j = jnp.bitwise_and(entry, dead_bit - 1)
        k0 = pl.multiple_of(j * tile, tile)
        vt = v_ref[0, pl.ds(k0, tile), :]
        z = z_ref[...]
        if diagonal:
            q_off = jnp.bitwise_and(_iota((rows, tile), 0), Q_BLOCK - 1)
            z = jnp.where(_iota((rows, tile), 1) - q_off <= q0 - k0, z, MASK_BIAS)
        m_old = m_ref[...]
        m_new = jnp.maximum(m_old, jnp.max(z, axis=1, keepdims=True))
        p = jnp.exp(z - jnp.concatenate([m_new] * (tile // LANES), axis=1))
        a = jnp.exp(m_old - m_new)
        acc_ref[...] = a * acc_ref[...] + _dot(p, vt)
        m_ref[...] = m_new

    scores(n_tiles - 1, za_ref)

    any_q = jnp.broadcast_to(jnp.max(selm, axis=0, keepdims=True), (8, nblk))
    group = (jnp.right_shift(_iota((nblk, LANES), 0), int(math.log2(bpt))) == _iota((nblk, LANES), 1))
    tile_hits = _dot(any_q, jnp.where(group, 1.0, 0.0))
    for t in range(max_tiles):
        flag_ref[t] = (tile_hits[0, t] > 0.5).astype(jnp.int32)

    def scan(t, n):
        @pl.when(flag_ref[t] > 0)
        def _():
            list_ref[n] = t
        return n + (flag_ref[t] > 0).astype(jnp.int32)

    n_before = lax.fori_loop(0, n_tiles - 1, scan, 0)
    list_ref[n_before] = dead_bit
    n_pairs = jnp.right_shift(n_before, 1)

    def body(kk, c):
        scores(list_ref[2 * kk + 1], za_ref)
        absorb(zb_ref, list_ref[2 * kk], False)
        scores(list_ref[2 * kk + 2], zb_ref)
        absorb(za_ref, list_ref[2 * kk + 1], False)
        return c

    scores(list_ref[0], zb_ref)
    absorb(za_ref, n_tiles - 1, True)
    lax.fori_loop(0, n_pairs, body, 0)
    absorb(zb_ref, list_ref[2 * n_pairs], False)

    acc = acc_ref[...]
    total = pltpu.roll(acc, LANES - HEAD_DIM, 1)
    o = acc * (1.0 / jnp.maximum(total, 1e-30))
    out = jnp.zeros((Q_BLOCK, LANES), _F32)
    for h in range(N_HEADS):
        o_h = o[h * Q_BLOCK:(h + 1) * Q_BLOCK]
        if h:
            o_h = pltpu.roll(o_h, h * HEAD_DIM, 1)
        out = out + jnp.where(lane_head == h, o_h, 0.0)
    if gated:
        out = add_ref[0] + g_ref[0] * out
    o_ref[0] = out.astype(o_ref.dtype)


def _blk_attention(proj3, selm, gq, gkx, gv, blk_len, tile, slopes, gate3=None, addend=None):
    B, S, _ = proj3.shape
    nblk = selm.shape[-1]
    assert S % tile == 0 and gkx % 2 == 0 and nblk % LANES == 0
    gated = gate3 is not None
    rows = N_HEADS * Q_BLOCK
    in_specs = [pl.BlockSpec((1, Q_BLOCK, LANES), lambda b, i: (b, i, gq)),
                pl.BlockSpec((1, S, 2 * LANES), lambda b, i: (b, 0, gkx // 2)),
                pl.BlockSpec((1, S, LANES), lambda b, i: (b, 0, gv)),
                pl.BlockSpec((1, Q_BLOCK, nblk), lambda b, i: (b, i, 0))]
    args = [proj3, proj3, proj3, selm]
    if gated:
        in_specs += [pl.BlockSpec((1, Q_BLOCK, LANES), lambda b, i: (b, i, 1)),
                     pl.BlockSpec((1, Q_BLOCK, LANES), lambda b, i: (b, i, 0))]
        args += [gate3, addend]
    return pl.pallas_call(
        functools.partial(_blk_attn_kernel, blk_len=blk_len, tile=tile, slopes=slopes, nblk=nblk, seq=S,
                          gated=gated),
        grid=(B, S // Q_BLOCK),
        in_specs=in_specs,
        out_specs=pl.BlockSpec((1, Q_BLOCK, LANES), lambda b, i: (b, i, 0)),
        out_shape=jax.ShapeDtypeStruct((B, S, LANES), _BF),
        scratch_shapes=[pltpu.VMEM((rows, 2 * LANES), _BF),
                        pltpu.VMEM((rows, LANES), _F32),
                        pltpu.VMEM((rows, LANES), _F32),
                        pltpu.VMEM((rows, tile), _F32),
                        pltpu.VMEM((rows, tile), _F32),
                        pltpu.SMEM((S // tile,), jnp.int32),
                        pltpu.SMEM((S // tile + 2,), jnp.int32)],
        compiler_params=_params("parallel", "arbitrary"),
        name="blk_attn_%d" % blk_len,
    )(*args)


def _merge_kernel(x_ref, osb_ref, onsa_ref, omb_ref, wg_ref, wbr_ref, wo_ref, lg_ref, lb_ref, o_ref,
                  *, alpha):
    x = x_ref[...]
    d = x.shape[1]
    gates = jax.nn.sigmoid(_dot(x, wg_ref[...]))
    mix = (gates[:, :d] * _dot(osb_ref[...], wbr_ref[0])
           + gates[:, d:2 * d] * _dot(onsa_ref[...], wbr_ref[1])
           + gates[:, 2 * d:] * _dot(omb_ref[...], wbr_ref[2]))
    y = alpha * x + _dot(mix, wo_ref[...])
    o_ref[...] = _layer_norm(y, lg_ref[...], lb_ref[...])


def _merge(x2, o_sb, o_nsa, o_mb, w_gate, w_br, w_out, ln_g, ln_b, alpha, tm=256):
    T, D = x2.shape
    row = lambda i: (i, 0)
    fixed2 = lambda i: (0, 0)
    return pl.pallas_call(
        functools.partial(_merge_kernel, alpha=alpha),
        grid=(T // tm,),
        in_specs=[pl.BlockSpec((tm, D), row),
                  pl.BlockSpec((tm, LANES), row),
                  pl.BlockSpec((tm, LANES), row),
                  pl.BlockSpec((tm, LANES), row),
                  pl.BlockSpec((D, N_BRANCHES * D), fixed2),
                  pl.BlockSpec((N_BRANCHES, LANES, D), lambda i: (0, 0, 0)),
                  pl.BlockSpec((D, D), fixed2),
                  pl.BlockSpec((1, D), fixed2),
                  pl.BlockSpec((1, D), fixed2)],
        out_specs=pl.BlockSpec((tm, D), row),
        out_shape=jax.ShapeDtypeStruct((T, D), _F32),
        compiler_params=_params("parallel"),
        name="merge",
    )(x2, o_sb, o_nsa, o_mb, w_gate, w_br, w_out, ln_g, ln_b)


_NOT_RETRIEVED = 99.0


def _top_rows(s, k):
    n = s.shape[0]
    rows = _iota(s.shape, 0).astype(_F32)
    vals, ids = [], []
    for _ in range(k):
        m = jnp.max(s, axis=0, keepdims=True)
        idx = jnp.min(jnp.where(s == m, rows, float(n)), axis=0, keepdims=True)
        vals.append(m)
        ids.append(idx)
        s = jnp.where(rows == idx, -jnp.inf, s)
    return vals, ids


def _peer_route_kernel(x_ref, wq_ref, k1_ref, k2_ref, fa_ref, qb_ref, e1_ref, e2_ref):
    half = PEER_QDIM // 2
    qf = _dot(x_ref[...], wq_ref[...]).astype(_BF)
    tm = qf.shape[0]
    rows = _iota((PEER_NKEYS, tm), 0).astype(_F32)
    rows_k = _iota((PEER_TOPK, tm), 0)
    ncand = PEER_TOPK * PEER_TOPK
    pos = _iota((ncand, tm), 0).astype(_F32)
    for h in range(PEER_HEADS):
        s1 = _dot_nt(k1_ref[h], qf[:, h * PEER_QDIM:h * PEER_QDIM + half])
        s2 = _dot_nt(k2_ref[h], qf[:, h * PEER_QDIM + half:(h + 1) * PEER_QDIM])
        v1, i1 = _top_rows(s1, PEER_TOPK)
        v2, i2 = _top_rows(s2, PEER_TOPK)
        v2_all = jnp.zeros((PEER_TOPK, tm), _F32)
        for qi in range(PEER_TOPK):
            v2_all = jnp.where(rows_k == qi, v2[qi], v2_all)
        cand = jnp.concatenate([v1[p] + v2_all for p in range(PEER_TOPK)], axis=0)
        c = cand
        pickf = jnp.zeros(cand.shape, _F32)
        for _ in range(PEER_TOPK):
            m = jnp.max(c, axis=0, keepdims=True)
            idx = jnp.min(jnp.where(c == m, pos, float(ncand)), axis=0, keepdims=True)
            hit = pos == idx
            pickf = jnp.where(hit, 1.0, pickf)
            c = jnp.where(hit, -jnp.inf, c)
        cmax = v1[0] + v2[0]
        z = jnp.sum(pickf * jnp.exp(cand - cmax), axis=0, keepdims=True)
        fa = jnp.zeros((PEER_NKEYS, tm), _F32)
        qb = jnp.full((PEER_NKEYS, tm), _NOT_RETRIEVED, _F32)
        for p in range(PEER_TOPK):
            count = jnp.sum(pickf[p * PEER_TOPK:(p + 1) * PEER_TOPK], axis=0, keepdims=True)
            fa = jnp.where(rows == i1[p], count, fa)
            qb = jnp.where(rows == i2[p], float(p), qb)
        fa_ref[h] = fa
        qb_ref[h] = qb.astype(qb_ref.dtype)
        e1_ref[h] = jnp.exp(s1 - v1[0]) * (1.0 / z)
        e2_ref[h] = jnp.exp(s2 - v2[0]).astype(e2_ref.dtype)


def _peer_route(x2, wq, k1, k2, tm=256):
    T, D = x2.shape
    tab = jax.ShapeDtypeStruct((PEER_HEADS, PEER_NKEYS, T), _F32)
    tab_bf = jax.ShapeDtypeStruct((PEER_HEADS, PEER_NKEYS, T), _BF)
    tab_spec = pl.BlockSpec((PEER_HEADS, PEER_NKEYS, tm), lambda i: (0, 0, i))
    return pl.pallas_call(
        _peer_route_kernel,
        grid=(T // tm,),
        in_specs=[pl.BlockSpec((tm, D), lambda i: (i, 0)),
                  pl.BlockSpec(wq.shape, lambda i: (0, 0)),
                  pl.BlockSpec(k1.shape, lambda i: (0, 0, 0)),
                  pl.BlockSpec(k2.shape, lambda i: (0, 0, 0))],
        out_specs=[tab_spec] * 4,
        out_shape=[tab, tab_bf, tab, tab_bf],
        compiler_params=_params("parallel"),
        name="peer_route",
    )(x2, wq, k1, k2)


def _gelu_tanh(s):
    c1 = math.sqrt(2.0 / math.pi)
    k1 = jnp.asarray(c1, s.dtype)
    k2 = jnp.asarray(c1 * 0.044715, s.dtype)
    inner = s * (k1 + k2 * (s * s))
    half = jnp.asarray(0.5, s.dtype) * s
    return half + half * jnp.tanh(inner)


def _peer_dense_kernel(x_ref, xr_ref, u_ref, vt_ref, fa_ref, qb_ref, e1_ref, e2_ref, lg_ref, lb_ref, o_ref,
                       acc_ref, xb_ref, s_ref, c_ref, *, alpha, te, n_tiles, n_work):
    s_idx = pl.program_id(0)
    j_score = s_idx % n_tiles
    j_gate = jnp.maximum(s_idx - 1, 0) % n_tiles
    j_value = jnp.maximum(s_idx - 2, 0) % n_tiles

    @pl.when(s_idx == 0)
    def _():
        s_ref[...] = jnp.zeros_like(s_ref)
        c_ref[...] = jnp.zeros_like(c_ref)
        acc_ref[...] = jnp.zeros_like(acc_ref)

    @pl.when(jnp.logical_and(j_score == 0, s_idx < n_work))
    def _():
        xb_ref[...] = x_ref[...].astype(_BF)

    fresh = j_value == 0
    cur = s_idx % 2
    tm = s_ref.shape[2]
    tc = tm // 2
    for ck in range(tm // tc):
        cols = slice(ck * tc, (ck + 1) * tc)
        acc_ref[:, cols] = jnp.where(fresh, 0.0, acc_ref[:, cols]) + jnp.dot(
            vt_ref[...], c_ref[1 - cur, :, cols], preferred_element_type=_F32)
        s = s_ref[1 - cur, :, cols]
        act = _gelu_tanh(s)
        for r in range(te // PEER_NKEYS):
            a = j_gate * (te // PEER_NKEYS) + r
            gate = jnp.zeros((PEER_NKEYS, tc), _BF)
            for h in range(PEER_HEADS):
                fa = fa_ref[h, pl.ds(a, 1), cols].astype(_BF)
                e1 = e1_ref[h, pl.ds(a, 1), cols].astype(_BF)
                gate = gate + e1 * jnp.where(qb_ref[h, :, cols] < fa, e2_ref[h, :, cols], jnp.zeros((), _BF))
            c_ref[cur, r * PEER_NKEYS:(r + 1) * PEER_NKEYS, cols] = (
                gate * act[r * PEER_NKEYS:(r + 1) * PEER_NKEYS])
        s_ref[cur, :, cols] = _dot_nt(u_ref[...], xb_ref[cols, :]).astype(_BF)

    @pl.when(jnp.logical_and(j_value == n_tiles - 1, s_idx >= 2))
    def _():
        y = alpha * xr_ref[...] + acc_ref[...].T
        o_ref[...] = _layer_norm(y, lg_ref[...], lb_ref[...])


def _peer_dense(x2, u_all, vt_all, layer, tabs, ln_g, ln_b, alpha, tm=512):
    T, D = x2.shape
    n_tiles, te = vt_all.shape[1], vt_all.shape[3]
    n_tok = T // tm
    n_work = n_tok * n_tiles
    last = n_work - 1
    tok = lambda lag: (lambda s: (jnp.clip(s - lag, 0, last) // n_tiles, 0))
    exp_tile = lambda lag: (lambda s: jnp.clip(s - lag, 0, last) % n_tiles)
    tab_spec = pl.BlockSpec((PEER_HEADS, PEER_NKEYS, tm), lambda s: (0, 0, jnp.clip(s - 1, 0, last) // n_tiles))
    return pl.pallas_call(
        functools.partial(_peer_dense_kernel, alpha=alpha, te=te, n_tiles=n_tiles, n_work=n_work),
        grid=(n_work + 2,),
        in_specs=[pl.BlockSpec((tm, D), tok(0)),
                  pl.BlockSpec((tm, D), tok(2)),
                  pl.BlockSpec((None, te, D), lambda s: (layer, exp_tile(0)(s), 0)),
                  pl.BlockSpec((None, None, D, te), lambda s: (layer, exp_tile(2)(s), 0, 0)),
                  tab_spec, tab_spec, tab_spec, tab_spec,
                  pl.BlockSpec((1, D), lambda s: (0, 0)),
                  pl.BlockSpec((1, D), lambda s: (0, 0))],
        out_specs=pl.BlockSpec((tm, D), tok(2)),
        out_shape=jax.ShapeDtypeStruct((T, D), _F32),
        scratch_shapes=[pltpu.VMEM((D, tm), _F32), pltpu.VMEM((tm, D), _BF),
                        pltpu.VMEM((2, te, tm), _BF), pltpu.VMEM((2, te, tm), _BF)],
        compiler_params=_params("arbitrary"),
        name="peer_dense",
    )(x2, x2, u_all, vt_all, *tabs, ln_g, ln_b)


def _in_widths(d_model):
    w = N_HEADS * HEAD_DIM
    return (w, w, w, w) + (HEAD_DIM,) * 6 + (3 * N_HEADS, w, HEAD_DIM, HEAD_DIM, N_BRANCHES * d_model)


def _arrange_w_in(w_in, d_model):
    off = np.concatenate([[0], np.cumsum(_in_widths(d_model))])
    scale = HEAD_DIM ** -0.5
    seg = lambda n: w_in[:, off[n]:off[n + 1]]
    rep = lambda n: jnp.tile(seg(n), (1, N_HEADS))
    gate_cols = np.array([off[10] + h * 3 + c for c in range(3) for h in range(N_HEADS)
                          for _ in range(HEAD_DIM)])
    ckv = jnp.concatenate([seg(4), seg(5), jnp.zeros((w_in.shape[0], LANES - 2 * HEAD_DIM), w_in.dtype)], axis=1)
    blank = jnp.zeros((w_in.shape[0], LANES), w_in.dtype)
    once = lambda n: jnp.concatenate([seg(n), blank[:, HEAD_DIM:]], axis=1)
    groups = [seg(0) * scale, seg(1), seg(2), seg(3) * scale, rep(6), blank, rep(12), blank,
              once(7), rep(8), rep(9), seg(11) * scale, once(13), ckv, w_in[:, gate_cols]]
    return jnp.concatenate(groups, axis=1).astype(_BF), seg(14).astype(_BF)


def _arrange_compress(w_ck, w_cv, pe_k, pe_v):
    half = NSA_CMP_STRIDE
    wk = jnp.tile(w_ck.reshape(2, half, HEAD_DIM, HEAD_DIM), (1, 1, 1, N_HEADS))
    wv = jnp.tile(w_cv.reshape(2, half, HEAD_DIM, HEAD_DIM), (1, 1, 1, N_HEADS))
    w = jnp.zeros((2, half, LANES, 2 * LANES), _F32)
    w = w.at[:, :, :HEAD_DIM, :LANES].set(wk)
    w = w.at[:, :, HEAD_DIM:2 * HEAD_DIM, LANES:].set(wv)
    pe = jnp.zeros((2, half, LANES), _F32)
    pe = pe.at[:, :, :HEAD_DIM].set(pe_k.reshape(2, half, HEAD_DIM))
    pe = pe.at[:, :, HEAD_DIM:2 * HEAD_DIM].set(pe_v.reshape(2, half, HEAD_DIM))
    pe = jnp.broadcast_to(pe.reshape(2, 1, half * LANES), (2, 8, half * LANES))
    return w.reshape(2, half * LANES, 2 * LANES).astype(_BF), pe.astype(_BF)


def _key_features():
    out = np.zeros((2, KEY_TILE, LANES), np.float32)
    for n, (blk_len, tile) in enumerate(((NSA_SEL_LEN, NSA_KEY_TILE), (MOBA_BLOCK, MOBA_KEY_TILE))):
        c = np.arange(KEY_TILE) % tile
        out[n, np.arange(KEY_TILE), c // blk_len] = 1.0
        out[n, :, F_HI] = c // 256
        out[n, :, F_LO] = c % 256
        out[n, :, F_ONE] = 1.0
    return jnp.asarray(out, _BF)


def _importance_matrix(seq):
    ncp = seq // NSA_CMP_STRIDE
    nsel = seq // NSA_SEL_LEN
    ratio = NSA_SEL_LEN // NSA_CMP_STRIDE
    overlap = np.convolve(np.ones(ratio), np.ones(NSA_CMP_LEN // NSA_CMP_STRIDE))
    n_left = (NSA_CMP_LEN - NSA_CMP_STRIDE) // NSA_CMP_STRIDE
    m = np.zeros((ncp, -(-nsel // LANES) * LANES), np.float32)
    for j in range(nsel):
        for o, c in enumerate(overlap):
            n = ratio * j + o - n_left
            if 0 <= n < ncp - 1:
                m[n, j] += c
    return jnp.asarray(m, _BF)


def kernel(x, w_in, nsa_pe_k, nsa_pe_v, nsa_w_ck, nsa_w_cv, w_br_sb, w_br_nsa, w_br_moba, w_out, ln1_g, ln1_b, peer_wq, peer_k1, peer_k2, peer_u, peer_v, ln2_g, ln2_b):
    B, S, D = x.shape
    depth = w_in.shape[0]
    T = B * S
    assert S % MOBA_BLOCK == 0 and S >= NSA_WINDOW + Q_BLOCK
    alpha = (2.0 * depth) ** 0.25
    imp_mat = _importance_matrix(S)
    feats = _key_features()
    u_all = peer_u.astype(_BF)
    te = PEER_EXPERT_TILE
    vt_all = jnp.swapaxes(peer_v.reshape(depth, -1, te, D), 2, 3).astype(_BF)
    x2 = x.reshape(T, D)
    for l in range(depth):
        w_small, w_gate = _arrange_w_in(w_in[l], D)
        wc, pe = _arrange_compress(nsa_w_ck[l], nsa_w_cv[l], nsa_pe_k[l], nsa_pe_v[l])
        proj, ckv, ngate = _inproj(x2, w_small, feats)
        proj3 = proj.reshape(B, S, N_PROJ_GROUPS * LANES)
        gate3 = ngate.reshape(B, S, 3 * LANES)

        o_sb = _sb_attention(proj3)

        kc4, vc4 = _nsa_compress(ckv.reshape(B, S // NSA_CMP_STRIDE, NSA_CMP_STRIDE * LANES), wc, pe)
        o_cw, nsa_selm = _nsa_select(proj3, kc4, vc4, gate3, imp_mat)
        o_nsa = _blk_attention(proj3, nsa_selm, G_NQ, G_SK, G_SV, NSA_SEL_LEN, NSA_KEY_TILE, NSA_SLOPES,
                               gate3=gate3, addend=o_cw)

        mb_selm = _moba_select(proj3, _moba_mean(proj3))
        o_mb = _blk_attention(proj3, mb_selm, G_MQ, G_MK, G_MV, MOBA_BLOCK, MOBA_KEY_TILE, MOBA_SLOPES)

        w_br = jnp.stack([w_br_sb[l], w_br_nsa[l], w_br_moba[l]]).astype(_BF)
        x2 = _merge(x2, o_sb.reshape(T, LANES), o_nsa.reshape(T, LANES), o_mb.reshape(T, LANES),
                    w_gate, w_br, w_out[l].astype(_BF), ln1_g[l].reshape(1, D), ln1_b[l].reshape(1, D), alpha)

        wq = peer_wq[l].reshape(D, PEER_HEADS * PEER_QDIM).astype(_BF)
        tabs = _peer_route(x2, wq, peer_k1[l].astype(_BF), peer_k2[l].astype(_BF))
        x2 = _peer_dense(x2, u_all, vt_all, l, tabs, ln2_g[l].reshape(1, D), ln2_b[l].reshape(1, D), alpha)
    return x2.reshape(B, S, D)
```

```python
import functools
import math

import numpy as np
import jax
import jax.numpy as jnp
from jax import lax
from jax.experimental import pallas as pl
from jax.experimental.pallas import tpu as pltpu

HEAD_DIM = 32
N_HEADS = 4
Q_BLOCK = 128
NSA_CMP_LEN = 32
NSA_CMP_STRIDE = 16
NSA_SEL_LEN = 64
NSA_TOP_N = 8
NSA_WINDOW = 512
MOBA_BLOCK = 256
MOBA_TOPK = 3
PEER_HEADS = 4
PEER_NKEYS = 128
PEER_TOPK = 8
PEER_QDIM = 256
N_BRANCHES = 3
LN_EPS = 1e-5

LANES = 128
VMEM_LIMIT = 48 * 1024 * 1024

_BF = jnp.bfloat16
_F32 = jnp.float32
_NEG = -1e30

_ALIBI = [2.0 ** (-8.0 * (i + 1) / (2 * N_HEADS)) for i in range(2 * N_HEADS)]
NSA_SLOPES = tuple(_ALIBI[0::2])
MOBA_SLOPES = tuple(_ALIBI[1::2])

G_SBQ, G_SBK, G_SBV, G_NQ, G_SK, G_SKF, G_MK, G_MKF, G_SV, G_WK, G_WV, G_MQ, G_MV = range(13)
N_PROJ_GROUPS = 13
NSA_KEY_TILE = 512
MOBA_KEY_TILE = 1024
MOBA_Q_ROWS = 256
KEY_TILE = max(NSA_KEY_TILE, MOBA_KEY_TILE)
PEER_EXPERT_TILE = 1024
F_HI, F_LO, F_ONE = 125, 126, 127
MASK_BIAS = -1e30
SCORE_FLOOR = -5e29


def _dot(a, b):
    return jnp.dot(a.astype(_BF), b.astype(_BF), preferred_element_type=_F32)


def _dot_nt(a, b):
    return lax.dot_general(a.astype(_BF), b.astype(_BF), (((1,), (1,)), ((), ())),
                           preferred_element_type=_F32)


def _dot_split(a, b):
    hi = a.astype(_BF)
    lo = (a - hi.astype(_F32)).astype(_BF)
    return (jnp.dot(hi, b, preferred_element_type=_F32)
            + jnp.dot(lo, b, preferred_element_type=_F32))


def _iota(shape, dim):
    return lax.broadcasted_iota(jnp.int32, shape, dim)


def _lane_head(rows=Q_BLOCK):
    return jnp.right_shift(_iota((rows, LANES), 1), int(math.log2(HEAD_DIM)))


def _head_queries(q, lane_head):
    qf = q.astype(_F32)
    return [jnp.where(lane_head == h, qf, 0.0).astype(_BF) for h in range(N_HEADS)]


def _stack_heads(q, lane_head):
    return jnp.concatenate(_head_queries(q, lane_head), axis=0)


def _unstack_heads(acc, lane_head):
    out = jnp.zeros((Q_BLOCK, LANES), _F32)
    for h in range(N_HEADS):
        out = out + jnp.where(lane_head == h, acc[h * Q_BLOCK:(h + 1) * Q_BLOCK], 0.0)
    return out


def _first_max(x, ids, none):
    m = jnp.max(x, axis=1, keepdims=True)
    return m, jnp.min(jnp.where(x == m, ids, none), axis=1, keepdims=True)


def _params(*sem):
    return pltpu.CompilerParams(dimension_semantics=sem, vmem_limit_bytes=VMEM_LIMIT)


def _layer_norm(y, g, b):
    mu = jnp.mean(y, axis=-1, keepdims=True)
    d = y - mu
    var = jnp.mean(d * d, axis=-1, keepdims=True)
    return d * lax.rsqrt(var + LN_EPS) * g + b


def _inproj_kernel(x_ref, w_ref, feat_ref, proj_ref, ckv_ref, gate_ref):
    y = _dot(x_ref[...], w_ref[...])
    npj = N_PROJ_GROUPS * LANES
    proj_ref[...] = y[:, :npj].astype(proj_ref.dtype)
    proj_ref[:, G_SKF * LANES:(G_SKF + 1) * LANES] = feat_ref[0]
    proj_ref[:, G_MKF * LANES:(G_MKF + 1) * LANES] = feat_ref[1]
    ones = jnp.ones((y.shape[0], HEAD_DIM), proj_ref.dtype)
    proj_ref[:, G_SV * LANES + HEAD_DIM:G_SV * LANES + 2 * HEAD_DIM] = ones
    proj_ref[:, G_MV * LANES + HEAD_DIM:G_MV * LANES + 2 * HEAD_DIM] = ones
    ckv_ref[...] = y[:, npj:npj + LANES].astype(ckv_ref.dtype)
    gate_ref[...] = jax.nn.sigmoid(y[:, npj + LANES:])


def _inproj(x2, w_small, feats):
    T, D = x2.shape
    n = w_small.shape[1]
    npj = N_PROJ_GROUPS * LANES
    tm = KEY_TILE
    return pl.pallas_call(
        _inproj_kernel,
        grid=(T // tm,),
        in_specs=[pl.BlockSpec((tm, D), lambda i: (i, 0)),
                  pl.BlockSpec((D, n), lambda i: (0, 0)),
                  pl.BlockSpec((2, tm, LANES), lambda i: (0, 0, 0))],
        out_specs=[pl.BlockSpec((tm, npj), lambda i: (i, 0)),
                   pl.BlockSpec((tm, LANES), lambda i: (i, 0)),
                   pl.BlockSpec((tm, 3 * LANES), lambda i: (i, 0))],
        out_shape=[jax.ShapeDtypeStruct((T, npj), _BF),
                   jax.ShapeDtypeStruct((T, LANES), _BF),
                   jax.ShapeDtypeStruct((T, 3 * LANES), _F32)],
        compiler_params=_params("parallel"),
        name="inproj",
    )(x2, w_small, feats)


_SB_LOG_CUTOFF = -104.0


def _sb_kernel(q_ref, k_ref, v_ref, o_ref):
    i = pl.program_id(1)
    lane_head = _lane_head()
    qs = _stack_heads(q_ref[0], lane_head)
    rows = N_HEADS * Q_BLOCK
    tri = (_iota((Q_BLOCK, Q_BLOCK), 0) > _iota((Q_BLOCK, Q_BLOCK), 1)).astype(_BF)
    q_off = jnp.bitwise_and(_iota((rows, Q_BLOCK), 0), Q_BLOCK - 1)
    diag_past = _iota((rows, Q_BLOCK), 1) < q_off

    def tile(j, carry, acc, diagonal):
        start = pl.multiple_of(j * Q_BLOCK, Q_BLOCK)
        kt = k_ref[0, pl.ds(start, Q_BLOCK), :]
        vt = v_ref[0, pl.ds(start, Q_BLOCK), :]
        z = _dot_nt(qs, kt)
        ls = -(jnp.maximum(z, 0.0) + jnp.log(1.0 + jnp.exp(-jnp.abs(z))))
        if diagonal:
            ls = jnp.where(diag_past, ls, 0.0)
        excl = _dot_split(ls, tri)
        w = jnp.exp(z + ls + excl + carry)
        if diagonal:
            w = jnp.where(diag_past, w, 0.0)
        acc = acc + _dot(w, vt)
        carry = carry + jnp.sum(ls, axis=1, keepdims=True)
        return carry, acc

    carry, acc = tile(i, jnp.zeros((rows, 1), _F32), jnp.zeros((rows, LANES), _F32), True)

    def cond(st):
        j, _, _, cmax = st
        return jnp.logical_and(j >= 0, cmax > _SB_LOG_CUTOFF)

    def body(st):
        j, carry, acc, _ = st
        carry, acc = tile(j, carry, acc, False)
        return j - 1, carry, acc, jnp.max(carry)

    _, _, acc, _ = lax.while_loop(cond, body, (i - 1, carry, acc, jnp.max(carry)))
    o_ref[0] = _unstack_heads(acc, lane_head).astype(o_ref.dtype)


def _sb_attention(proj3):
    B, S, _ = proj3.shape
    return pl.pallas_call(
        _sb_kernel,
        grid=(B, S // Q_BLOCK),
        in_specs=[pl.BlockSpec((1, Q_BLOCK, LANES), lambda b, i: (b, i, G_SBQ)),
                  pl.BlockSpec((1, S, LANES), lambda b, i: (b, 0, G_SBK)),
                  pl.BlockSpec((1, S, LANES), lambda b, i: (b, 0, G_SBV))],
        out_specs=pl.BlockSpec((1, Q_BLOCK, LANES), lambda b, i: (b, i, 0)),
        out_shape=jax.ShapeDtypeStruct((B, S, LANES), _BF),
        compiler_params=_params("parallel", "arbitrary"),
        name="sb_attn",
    )(proj3, proj3, proj3)


def _nsa_compress_kernel(c_ref, w_ref, pe_ref, kc_ref, vc_ref):
    c = c_ref[0]
    a = _dot(c, w_ref[0]) + _dot(pe_ref[0], w_ref[0])[0:1]
    b = _dot(c, w_ref[1]) + _dot(pe_ref[1], w_ref[1])[0:1]
    n = a.shape[0]
    b_next = pltpu.roll(b, n - 1, 0)
    y = a + b_next
    kc_ref[0] = y[:, :LANES].astype(kc_ref.dtype)
    vc_ref[0] = y[:, LANES:].astype(vc_ref.dtype)


def _nsa_compress(ckv3, wc, pe):
    B, nchunk, width = ckv3.shape
    return pl.pallas_call(
        _nsa_compress_kernel,
        grid=(B,),
        in_specs=[pl.BlockSpec((1, nchunk, width), lambda b: (b, 0, 0)),
                  pl.BlockSpec((2, width, 2 * LANES), lambda b: (0, 0, 0)),
                  pl.BlockSpec((2, 8, width), lambda b: (0, 0, 0))],
        out_specs=[pl.BlockSpec((1, nchunk, LANES), lambda b: (b, 0, 0)),
                   pl.BlockSpec((1, nchunk, LANES), lambda b: (b, 0, 0))],
        out_shape=[jax.ShapeDtypeStruct((B, nchunk, LANES), _BF),
                   jax.ShapeDtypeStruct((B, nchunk, LANES), _BF)],
        compiler_params=_params("parallel"),
        name="nsa_compress",
    )(ckv3, wc, pe)


def _masked_exp(z, valid):
    zm = jnp.where(valid, z, MASK_BIAS)
    m = jnp.maximum(jnp.max(zm, axis=1, keepdims=True), SCORE_FLOOR)
    p = jnp.exp(zm - m)
    return p, 1.0 / jnp.maximum(jnp.sum(p, axis=1, keepdims=True), 1e-30)


def _nsa_select_kernel(q_ref, kc_ref, vc_ref, wk_ref, wv_ref, g_ref, m_ref, ocw_ref, selm_ref,
                       *, seq):
    i = pl.program_id(1)
    q0 = i * Q_BLOCK
    ncp = seq // NSA_CMP_STRIDE
    nsel = m_ref.shape[1]
    win = NSA_WINDOW + Q_BLOCK
    lane_head = _lane_head()
    qs = _stack_heads(q_ref[0], lane_head)
    head_rows = lambda a, h: a[h * Q_BLOCK:(h + 1) * Q_BLOCK]

    cend = _iota((1, ncp), 1) * NSA_CMP_STRIDE + (NSA_CMP_LEN - 1)
    valid_c = (q0 + _iota((Q_BLOCK, ncp), 0)) >= cend
    rel_c = (cend - q0).astype(_F32)
    zc = _dot_nt(qs, kc_ref[0])
    pg = jnp.zeros((Q_BLOCK, ncp), _F32)
    pcs = []
    for h in range(N_HEADS):
        p, inv = _masked_exp(head_rows(zc, h) + NSA_SLOPES[h] * rel_c, valid_c)
        p = p * inv
        pcs.append(p.astype(_BF))
        pg = pg + p
    o_cmp = _unstack_heads(_dot(jnp.concatenate(pcs, axis=0), vc_ref[0]), lane_head)

    imp = _dot_split(pg, m_ref[...])
    blk = _iota((Q_BLOCK, nsel), 1)
    cur = jnp.right_shift(q0 + _iota((Q_BLOCK, nsel), 0), int(math.log2(NSA_SEL_LEN)))
    forced = (blk == 0) | (blk == cur) | (blk == cur - 1)
    x = jnp.where(blk > cur, -jnp.inf, jnp.where(forced, jnp.inf, imp))
    blkf = blk.astype(_F32)
    sel = jnp.zeros((Q_BLOCK, nsel), _F32)
    for _ in range(min(NSA_TOP_N, seq // NSA_SEL_LEN)):
        _, idx = _first_max(x, blkf, float(nsel))
        hit = blkf == idx
        sel = jnp.where(hit, 1.0, sel)
        x = jnp.where(hit, -jnp.inf, x)
    selm_ref[0] = sel.astype(selm_ref.dtype)

    ws = pl.multiple_of(jnp.maximum(q0 - NSA_WINDOW, 0), Q_BLOCK)
    kw = wk_ref[0, pl.ds(ws, win), :]
    vw = wv_ref[0, pl.ds(ws, win), :]
    kpos = ws + _iota((1, win), 1)
    dw = (q0 + _iota((Q_BLOCK, win), 0)) - kpos
    valid_w = (dw >= 0) & (dw < NSA_WINDOW)
    rel_w = (kpos - q0).astype(_F32)
    zw = _dot_nt(qs, kw)
    pws, invs = [], []
    for h in range(N_HEADS):
        p, inv = _masked_exp(head_rows(zw, h) + NSA_SLOPES[h] * rel_w, valid_w)
        pws.append(p.astype(_BF))
        invs.append(jnp.broadcast_to(inv, (Q_BLOCK, LANES)))
    o_win = _unstack_heads(_dot(jnp.concatenate(pws, axis=0), vw) * jnp.concatenate(invs, axis=0), lane_head)

    g = g_ref[0]
    ocw_ref[0] = g[:, :LANES] * o_cmp + g[:, 2 * LANES:] * o_win


def _nsa_select(proj3, kc4, vc4, gate3, imp_mat):
    B, S, _ = proj3.shape
    ncp = S // NSA_CMP_STRIDE
    nsel = imp_mat.shape[1]
    return pl.pallas_call(
        functools.partial(_nsa_select_kernel, seq=S),
        grid=(B, S // Q_BLOCK),
        in_specs=[pl.BlockSpec((1, Q_BLOCK, LANES), lambda b, i: (b, i, G_NQ)),
                  pl.BlockSpec((1, ncp, LANES), lambda b, i: (b, 0, 0)),
                  pl.BlockSpec((1, ncp, LANES), lambda b, i: (b, 0, 0)),
                  pl.BlockSpec((1, S, LANES), lambda b, i: (b, 0, G_WK)),
                  pl.BlockSpec((1, S, LANES), lambda b, i: (b, 0, G_WV)),
                  pl.BlockSpec((1, Q_BLOCK, 3 * LANES), lambda b, i: (b, i, 0)),
                  pl.BlockSpec((ncp, nsel), lambda b, i: (0, 0))],
        out_specs=[pl.BlockSpec((1, Q_BLOCK, LANES), lambda b, i: (b, i, 0)),
                   pl.BlockSpec((1, Q_BLOCK, nsel), lambda b, i: (b, i, 0))],
        out_shape=[jax.ShapeDtypeStruct((B, S, LANES), _F32),
                   jax.ShapeDtypeStruct((B, S, nsel), _BF)],
        compiler_params=_params("parallel", "arbitrary"),
        name="nsa_select",
    )(proj3, kc4, vc4, proj3, proj3, gate3, imp_mat)


def _moba_mean_kernel(k_ref, o_ref, *, nb):
    k = k_ref[0].astype(_F32)
    o_ref[0] = jnp.zeros(o_ref.shape[1:], _F32)
    o_ref[0, :nb, :] = jnp.mean(k.reshape(nb, MOBA_BLOCK, LANES), axis=1)


def _moba_mean(proj3):
    B, S, _ = proj3.shape
    nb = S // MOBA_BLOCK
    return pl.pallas_call(
        functools.partial(_moba_mean_kernel, nb=nb),
        grid=(B,),
        in_specs=[pl.BlockSpec((1, S, LANES), lambda b: (b, 0, G_MK))],
        out_specs=pl.BlockSpec((1, LANES, LANES), lambda b: (b, 0, 0)),
        out_shape=jax.ShapeDtypeStruct((B, LANES, LANES), _F32),
        compiler_params=_params("parallel"),
        name="moba_mean",
    )(proj3)


def _moba_select_kernel(q_ref, km_ref, selm_ref, *, nb):
    i = pl.program_id(1)
    q0 = i * Q_BLOCK
    sg = _dot_nt(q_ref[0], km_ref[0])
    blk = _iota((Q_BLOCK, LANES), 1)
    cur = jnp.right_shift(q0, int(math.log2(MOBA_BLOCK)))
    x = jnp.where(blk < cur, sg, -jnp.inf)
    blkf = blk.astype(_F32)
    sel = jnp.where(blk == cur, 1.0, 0.0)
    for _ in range(min(MOBA_TOPK, nb)):
        m, idx = _first_max(x, blkf, float(LANES))
        hit = blkf == idx
        sel = jnp.where(hit & (m > -jnp.inf), 1.0, sel)
        x = jnp.where(hit, -jnp.inf, x)
    selm_ref[0] = sel.astype(selm_ref.dtype)


def _moba_select(proj3, kmean):
    B, S, _ = proj3.shape
    nb = S // MOBA_BLOCK
    assert nb <= LANES
    return pl.pallas_call(
        functools.partial(_moba_select_kernel, nb=nb),
        grid=(B, S // Q_BLOCK),
        in_specs=[pl.BlockSpec((1, Q_BLOCK, LANES), lambda b, i: (b, i, G_MQ)),
                  pl.BlockSpec((1, LANES, LANES), lambda b, i: (b, 0, 0))],
        out_specs=pl.BlockSpec((1, Q_BLOCK, LANES), lambda b, i: (b, i, 0)),
        out_shape=jax.ShapeDtypeStruct((B, S, LANES), _BF),
        compiler_params=_params("parallel", "arbitrary"),
        name="moba_select",
    )(proj3, kmean)


def _blk_attn_kernel(*refs, blk_len, tile, qb, slopes, nblk, seq, gated):
    if gated:
        q_ref, kx_ref, v_ref, selm_ref, g_ref, add_ref, o_ref = refs[:7]
    else:
        q_ref, kx_ref, v_ref, selm_ref, o_ref = refs[:5]
    qx_ref, m_ref, acc_ref, za_ref, zb_ref, flag_ref, list_ref = refs[-7:]
    rows = N_HEADS * qb
    bpt = tile // blk_len
    max_tiles = seq // tile
    i = pl.program_id(1)
    q0 = i * qb
    lane_head = _lane_head(qb)
    lane = _iota((qb, LANES), 1)

    qx_ref[:, :LANES] = _stack_heads(q_ref[0], lane_head)
    m_ref[...] = jnp.full((rows, LANES), SCORE_FLOOR, _F32)
    acc_ref[...] = jnp.zeros((rows, LANES), _F32)
    lane1 = _iota((1, LANES), 1)
    fixed_lanes = [jnp.where(lane1 == F_HI, slopes[h] * 256.0, jnp.where(lane1 == F_LO, slopes[h], 0.0))
                   for h in range(N_HEADS)]
    dist_lanes = [jnp.where(lane1 == F_ONE, slopes[h] * LANES, 0.0) for h in range(N_HEADS)]

    selm = selm_ref[0].astype(_F32)
    sneg = (1.0 - selm) * MASK_BIAS
    n_tiles = jnp.right_shift(q0 + qb + tile - 1, int(math.log2(tile)))
    dead_bit = 1 << 20

    def scores(entry, z_ref):
        j = jnp.bitwise_and(entry, dead_bit - 1)
        k0 = pl.multiple_of(j * tile, tile)
        first = j * bpt
        if nblk > LANES:
            half = jnp.right_shift(first, int(math.log2(LANES)))
            base = sneg[:, :LANES]
            for c in range(1, nblk // LANES):
                base = jnp.where(half == c, sneg[:, c * LANES:(c + 1) * LANES], base)
        else:
            base = sneg
        shift = jnp.bitwise_and(LANES - jnp.bitwise_and(first, LANES - 1), LANES - 1)
        tile_mask = pltpu.roll(base, shift, 1)
        tile_mask = jnp.where(entry >= dead_bit, MASK_BIAS, tile_mask)
        tile_mask = jnp.where(lane >= F_HI, 0.0, tile_mask)
        dist = (j * (tile // LANES) - i * (qb // LANES)).astype(_F32)
        for h in range(N_HEADS):
            ext = tile_mask + (fixed_lanes[h] + dist * dist_lanes[h])
            qx_ref[h * qb:(h + 1) * qb, LANES:] = ext.astype(_BF)
        kx = kx_ref[0, pl.ds(k0, tile), :]
        z_ref[...] = lax.dot_general(qx_ref[...], kx, (((1,), (1,)), ((), ())),
                                     preferred_element_type=_F32)

    def absorb(z_ref, entry, diagonal):
        j = jnp.bitwise_and(entry, dead_bit - 1)
        k0 = pl.multiple_of(j * tile, tile)
        vt = v_ref[0, pl.ds(k0, tile), :]
        z = z_ref[...]
        if diagonal:
            q_off = jnp.bitwise_and(_iota((rows, tile), 0), qb - 1)
            z = jnp.where(_iota((rows, tile), 1) - q_off <= q0 - k0, z, MASK_BIAS)
        m_old = m_ref[...]
        m_new = jnp.maximum(m_old, jnp.max(z, axis=1, keepdims=True))
        p = jnp.exp(z - jnp.concatenate([m_new] * (tile // LANES), axis=1))
        a = jnp.exp(m_old - m_new)
        acc_ref[...] = a * acc_ref[...] + _dot(p, vt)
        m_ref[...] = m_new

    scores(n_tiles - 1, za_ref)

    any_q = jnp.broadcast_to(jnp.max(selm, axis=0, keepdims=True), (8, nblk))
    group = (jnp.right_shift(_iota((nblk, LANES), 0), int(math.log2(bpt))) == _iota((nblk, LANES), 1))
    tile_hits = _dot(any_q, jnp.where(group, 1.0, 0.0))
    for t in range(max_tiles):
        flag_ref[t] = (tile_hits[0, t] > 0.5).astype(jnp.int32)

    def scan(t, n):
        @pl.when(flag_ref[t] > 0)
        def _():
            list_ref[n] = t
        return n + (flag_ref[t] > 0).astype(jnp.int32)

    n_before = lax.fori_loop(0, n_tiles - 1, scan, 0)
    list_ref[n_before] = dead_bit
    n_pairs = jnp.right_shift(n_before, 1)

    def body(kk, c):
        scores(list_ref[2 * kk + 1], za_ref)
        absorb(zb_ref, list_ref[2 * kk], False)
        scores(list_ref[2 * kk + 2], zb_ref)
        absorb(za_ref, list_ref[2 * kk + 1], False)
        return c

    scores(list_ref[0], zb_ref)
    absorb(za_ref, n_tiles - 1, True)
    lax.fori_loop(0, n_pairs, body, 0)
    absorb(zb_ref, list_ref[2 * n_pairs], False)

    acc = acc_ref[...]
    total = pltpu.roll(acc, LANES - HEAD_DIM, 1)
    o = acc * (1.0 / jnp.maximum(total, 1e-30))
    out = jnp.zeros((qb, LANES), _F32)
    for h in range(N_HEADS):
        o_h = o[h * qb:(h + 1) * qb]
        if h:
            o_h = pltpu.roll(o_h, h * HEAD_DIM, 1)
        out = out + jnp.where(lane_head == h, o_h, 0.0)
    if gated:
        out = add_ref[0] + g_ref[0] * out
    o_ref[0] = out.astype(o_ref.dtype)


def _blk_attention(proj3, selm, gq, gkx, gv, blk_len, tile, qb, slopes, gate3=None, addend=None):
    B, S, _ = proj3.shape
    nblk = selm.shape[-1]
    assert S % tile == 0 and gkx % 2 == 0 and nblk % LANES == 0
    gated = gate3 is not None
    rows = N_HEADS * qb
    in_specs = [pl.BlockSpec((1, qb, LANES), lambda b, i: (b, i, gq)),
                pl.BlockSpec((1, S, 2 * LANES), lambda b, i: (b, 0, gkx // 2)),
                pl.BlockSpec((1, S, LANES), lambda b, i: (b, 0, gv)),
                pl.BlockSpec((1, qb, nblk), lambda b, i: (b, i, 0))]
    args = [proj3, proj3, proj3, selm]
    if gated:
        in_specs += [pl.BlockSpec((1, qb, LANES), lambda b, i: (b, i, 1)),
                     pl.BlockSpec((1, qb, LANES), lambda b, i: (b, i, 0))]
        args += [gate3, addend]
    return pl.pallas_call(
        functools.partial(_blk_attn_kernel, blk_len=blk_len, tile=tile, qb=qb, slopes=slopes, nblk=nblk, seq=S,
                          gated=gated),
        grid=(B, S // qb),
        in_specs=in_specs,
        out_specs=pl.BlockSpec((1, qb, LANES), lambda b, i: (b, i, 0)),
        out_shape=jax.ShapeDtypeStruct((B, S, LANES), _BF),
        scratch_shapes=[pltpu.VMEM((rows, 2 * LANES), _BF),
                        pltpu.VMEM((rows, LANES), _F32),
                        pltpu.VMEM((rows, LANES), _F32),
                        pltpu.VMEM((rows, tile), _F32),
                        pltpu.VMEM((rows, tile), _F32),
                        pltpu.SMEM((S // tile,), jnp.int32),
                        pltpu.SMEM((S // tile + 2,), jnp.int32)],
        compiler_params=_params("parallel", "arbitrary"),
        name="blk_attn_%d" % blk_len,
    )(*args)


def _merge_kernel(x_ref, osb_ref, onsa_ref, omb_ref, wg_ref, wbr_ref, wo_ref, lg_ref, lb_ref, o_ref,
                  *, alpha):
    x = x_ref[...]
    d = x.shape[1]
    gates = jax.nn.sigmoid(_dot(x, wg_ref[...]))
    mix = (gates[:, :d] * _dot(osb_ref[...], wbr_ref[0])
           + gates[:, d:2 * d] * _dot(onsa_ref[...], wbr_ref[1])
           + gates[:, 2 * d:] * _dot(omb_ref[...], wbr_ref[2]))
    y = alpha * x + _dot(mix, wo_ref[...])
    o_ref[...] = _layer_norm(y, lg_ref[...], lb_ref[...])


def _merge(x2, o_sb, o_nsa, o_mb, w_gate, w_br, w_out, ln_g, ln_b, alpha, tm=256):
    T, D = x2.shape
    row = lambda i: (i, 0)
    fixed2 = lambda i: (0, 0)
    return pl.pallas_call(
        functools.partial(_merge_kernel, alpha=alpha),
        grid=(T // tm,),
        in_specs=[pl.BlockSpec((tm, D), row),
                  pl.BlockSpec((tm, LANES), row),
                  pl.BlockSpec((tm, LANES), row),
                  pl.BlockSpec((tm, LANES), row),
                  pl.BlockSpec((D, N_BRANCHES * D), fixed2),
                  pl.BlockSpec((N_BRANCHES, LANES, D), lambda i: (0, 0, 0)),
                  pl.BlockSpec((D, D), fixed2),
                  pl.BlockSpec((1, D), fixed2),
                  pl.BlockSpec((1, D), fixed2)],
        out_specs=pl.BlockSpec((tm, D), row),
        out_shape=jax.ShapeDtypeStruct((T, D), _F32),
        compiler_params=_params("parallel"),
        name="merge",
    )(x2, o_sb, o_nsa, o_mb, w_gate, w_br, w_out, ln_g, ln_b)


_NOT_RETRIEVED = 99.0


def _top_rows(s, k, exact):
    n = s.shape[0]
    rows = _iota(s.shape, 0).astype(_F32)
    rank = jnp.full(s.shape, _NOT_RETRIEVED, _F32)
    vals = []
    for p in range(k):
        m = jnp.max(s, axis=0, keepdims=True)
        hit = s == m
        if exact:
            hit = rows == jnp.min(jnp.where(hit, rows, float(n)), axis=0, keepdims=True)
        vals.append(m)
        rank = jnp.where(hit, float(p), rank)
        s = jnp.where(hit, -jnp.inf, s)
    return vals, rank


def _peer_route_kernel(x_ref, wq_ref, k1_ref, k2_ref, fa_ref, qb_ref, e1_ref, e2_ref):
    half = PEER_QDIM // 2
    qf = _dot(x_ref[...], wq_ref[...]).astype(_BF)
    tm = qf.shape[0]
    rows_k = _iota((PEER_TOPK, tm), 0)
    ncand = PEER_TOPK * PEER_TOPK
    pos = _iota((ncand, tm), 0).astype(_F32)

    def route(exact):
        most = jnp.zeros((1, tm), _F32)
        for h in range(PEER_HEADS):
            s1 = _dot_nt(k1_ref[h], qf[:, h * PEER_QDIM:h * PEER_QDIM + half])
            s2 = _dot_nt(k2_ref[h], qf[:, h * PEER_QDIM + half:(h + 1) * PEER_QDIM])
            v1, r1 = _top_rows(s1, PEER_TOPK, exact)
            v2, r2 = _top_rows(s2, PEER_TOPK, exact)
            for r in (r1, r2):
                most = jnp.maximum(most, jnp.sum(jnp.where(r < _NOT_RETRIEVED, 1.0, 0.0), axis=0, keepdims=True))
            v2_all = jnp.zeros((PEER_TOPK, tm), _F32)
            for qi in range(PEER_TOPK):
                v2_all = jnp.where(rows_k == qi, v2[qi], v2_all)
            cand = jnp.concatenate([v1[p] + v2_all for p in range(PEER_TOPK)], axis=0)
            c = cand
            pickf = jnp.zeros(cand.shape, _F32)
            for _ in range(PEER_TOPK):
                m = jnp.max(c, axis=0, keepdims=True)
                idx = jnp.min(jnp.where(c == m, pos, float(ncand)), axis=0, keepdims=True)
                hit = pos == idx
                pickf = jnp.where(hit, 1.0, pickf)
                c = jnp.where(hit, -jnp.inf, c)
            cmax = v1[0] + v2[0]
            z = jnp.sum(pickf * jnp.exp(cand - cmax), axis=0, keepdims=True)
            fa = jnp.zeros((PEER_NKEYS, tm), _F32)
            for p in range(PEER_TOPK):
                count = jnp.sum(pickf[p * PEER_TOPK:(p + 1) * PEER_TOPK], axis=0, keepdims=True)
                fa = jnp.where(r1 == float(p), count, fa)
            fa_ref[h] = fa
            qb_ref[h] = r2.astype(qb_ref.dtype)
            e1_ref[h] = jnp.exp(s1 - v1[0]) * (1.0 / z)
            e2_ref[h] = jnp.exp(s2 - v2[0]).astype(e2_ref.dtype)
        return jnp.max(most)

    @pl.when(route(False) > PEER_TOPK)
    def _():
        route(True)


def _peer_route(x2, wq, k1, k2, tm=256):
    T, D = x2.shape
    tab = jax.ShapeDtypeStruct((PEER_HEADS, PEER_NKEYS, T), _F32)
    tab_bf = jax.ShapeDtypeStruct((PEER_HEADS, PEER_NKEYS, T), _BF)
    tab_spec = pl.BlockSpec((PEER_HEADS, PEER_NKEYS, tm), lambda i: (0, 0, i))
    return pl.pallas_call(
        _peer_route_kernel,
        grid=(T // tm,),
        in_specs=[pl.BlockSpec((tm, D), lambda i: (i, 0)),
                  pl.BlockSpec(wq.shape, lambda i: (0, 0)),
                  pl.BlockSpec(k1.shape, lambda i: (0, 0, 0)),
                  pl.BlockSpec(k2.shape, lambda i: (0, 0, 0))],
        out_specs=[tab_spec] * 4,
        out_shape=[tab, tab_bf, tab, tab_bf],
        compiler_params=_params("parallel"),
        name="peer_route",
    )(x2, wq, k1, k2)


def _gelu_tanh(s):
    c1 = math.sqrt(2.0 / math.pi)
    k1 = jnp.asarray(c1, s.dtype)
    k2 = jnp.asarray(c1 * 0.044715, s.dtype)
    inner = s * (k1 + k2 * (s * s))
    half = jnp.asarray(0.5, s.dtype) * s
    return half + half * jnp.tanh(inner)


def _peer_dense_kernel(x_ref, xr_ref, u_ref, vt_ref, fa_ref, qb_ref, e1_ref, e2_ref, lg_ref, lb_ref, o_ref,
                       acc_ref, xb_ref, s_ref, c_ref, *, alpha, te, n_tiles, n_work):
    s_idx = pl.program_id(0)
    j_score = s_idx % n_tiles
    j_gate = jnp.maximum(s_idx - 1, 0) % n_tiles
    j_value = jnp.maximum(s_idx - 2, 0) % n_tiles

    @pl.when(s_idx == 0)
    def _():
        s_ref[...] = jnp.zeros_like(s_ref)
        c_ref[...] = jnp.zeros_like(c_ref)
        acc_ref[...] = jnp.zeros_like(acc_ref)

    @pl.when(jnp.logical_and(j_score == 0, s_idx < n_work))
    def _():
        xb_ref[...] = x_ref[...].astype(_BF)

    fresh = j_value == 0
    cur = s_idx % 2
    tm = s_ref.shape[2]
    tc = tm // 2
    for ck in range(tm // tc):
        cols = slice(ck * tc, (ck + 1) * tc)
        acc_ref[:, cols] = jnp.where(fresh, 0.0, acc_ref[:, cols]) + jnp.dot(
            vt_ref[...], c_ref[1 - cur, :, cols], preferred_element_type=_F32)
        s = s_ref[1 - cur, :, cols]
        act = _gelu_tanh(s)
        for r in range(te // PEER_NKEYS):
            a = j_gate * (te // PEER_NKEYS) + r
            gate = jnp.zeros((PEER_NKEYS, tc), _BF)
            for h in range(PEER_HEADS):
                fa = fa_ref[h, pl.ds(a, 1), cols].astype(_BF)
                e1 = e1_ref[h, pl.ds(a, 1), cols].astype(_BF)
                gate = gate + e1 * jnp.where(qb_ref[h, :, cols] < fa, e2_ref[h, :, cols], jnp.zeros((), _BF))
            c_ref[cur, r * PEER_NKEYS:(r + 1) * PEER_NKEYS, cols] = (
                gate * act[r * PEER_NKEYS:(r + 1) * PEER_NKEYS])
        s_ref[cur, :, cols] = _dot_nt(u_ref[...], xb_ref[cols, :]).astype(_BF)

    @pl.when(jnp.logical_and(j_value == n_tiles - 1, s_idx >= 2))
    def _():
        y = alpha * xr_ref[...] + acc_ref[...].T
        o_ref[...] = _layer_norm(y, lg_ref[...], lb_ref[...])


def _peer_dense(x2, u_all, vt_all, layer, tabs, ln_g, ln_b, alpha, tm=512):
    T, D = x2.shape
    n_tiles, te = vt_all.shape[1], vt_all.shape[3]
    n_tok = T // tm
    n_work = n_tok * n_tiles
    last = n_work - 1
    tok = lambda lag: (lambda s: (jnp.clip(s - lag, 0, last) // n_tiles, 0))
    exp_tile = lambda lag: (lambda s: jnp.clip(s - lag, 0, last) % n_tiles)
    tab_spec = pl.BlockSpec((PEER_HEADS, PEER_NKEYS, tm), lambda s: (0, 0, jnp.clip(s - 1, 0, last) // n_tiles))
    return pl.pallas_call(
        functools.partial(_peer_dense_kernel, alpha=alpha, te=te, n_tiles=n_tiles, n_work=n_work),
        grid=(n_work + 2,),
        in_specs=[pl.BlockSpec((tm, D), tok(0)),
                  pl.BlockSpec((tm, D), tok(2)),
                  pl.BlockSpec((None, te, D), lambda s: (layer, exp_tile(0)(s), 0)),
                  pl.BlockSpec((None, None, D, te), lambda s: (layer, exp_tile(2)(s), 0, 0)),
                  tab_spec, tab_spec, tab_spec, tab_spec,
                  pl.BlockSpec((1, D), lambda s: (0, 0)),
                  pl.BlockSpec((1, D), lambda s: (0, 0))],
        out_specs=pl.BlockSpec((tm, D), tok(2)),
        out_shape=jax.ShapeDtypeStruct((T, D), _F32),
        scratch_shapes=[pltpu.VMEM((D, tm), _F32), pltpu.VMEM((tm, D), _BF),
                        pltpu.VMEM((2, te, tm), _BF), pltpu.VMEM((2, te, tm), _BF)],
        compiler_params=_params("arbitrary"),
        name="peer_dense",
    )(x2, x2, u_all, vt_all, *tabs, ln_g, ln_b)


def _in_widths(d_model):
    w = N_HEADS * HEAD_DIM
    return (w, w, w, w) + (HEAD_DIM,) * 6 + (3 * N_HEADS, w, HEAD_DIM, HEAD_DIM, N_BRANCHES * d_model)


def _arrange_w_in(w_in, d_model):
    off = np.concatenate([[0], np.cumsum(_in_widths(d_model))])
    scale = HEAD_DIM ** -0.5
    seg = lambda n: w_in[:, off[n]:off[n + 1]]
    rep = lambda n: jnp.tile(seg(n), (1, N_HEADS))
    gate_cols = np.array([off[10] + h * 3 + c for c in range(3) for h in range(N_HEADS)
                          for _ in range(HEAD_DIM)])
    ckv = jnp.concatenate([seg(4), seg(5), jnp.zeros((w_in.shape[0], LANES - 2 * HEAD_DIM), w_in.dtype)], axis=1)
    blank = jnp.zeros((w_in.shape[0], LANES), w_in.dtype)
    once = lambda n: jnp.concatenate([seg(n), blank[:, HEAD_DIM:]], axis=1)
    groups = [seg(0) * scale, seg(1), seg(2), seg(3) * scale, rep(6), blank, rep(12), blank,
              once(7), rep(8), rep(9), seg(11) * scale, once(13), ckv, w_in[:, gate_cols]]
    return jnp.concatenate(groups, axis=1).astype(_BF), seg(14).astype(_BF)


def _arrange_compress(w_ck, w_cv, pe_k, pe_v):
    half = NSA_CMP_STRIDE
    wk = jnp.tile(w_ck.reshape(2, half, HEAD_DIM, HEAD_DIM), (1, 1, 1, N_HEADS))
    wv = jnp.tile(w_cv.reshape(2, half, HEAD_DIM, HEAD_DIM), (1, 1, 1, N_HEADS))
    w = jnp.zeros((2, half, LANES, 2 * LANES), _F32)
    w = w.at[:, :, :HEAD_DIM, :LANES].set(wk)
    w = w.at[:, :, HEAD_DIM:2 * HEAD_DIM, LANES:].set(wv)
    pe = jnp.zeros((2, half, LANES), _F32)
    pe = pe.at[:, :, :HEAD_DIM].set(pe_k.reshape(2, half, HEAD_DIM))
    pe = pe.at[:, :, HEAD_DIM:2 * HEAD_DIM].set(pe_v.reshape(2, half, HEAD_DIM))
    pe = jnp.broadcast_to(pe.reshape(2, 1, half * LANES), (2, 8, half * LANES))
    return w.reshape(2, half * LANES, 2 * LANES).astype(_BF), pe.astype(_BF)


def _key_features():
    out = np.zeros((2, KEY_TILE, LANES), np.float32)
    for n, (blk_len, tile) in enumerate(((NSA_SEL_LEN, NSA_KEY_TILE), (MOBA_BLOCK, MOBA_KEY_TILE))):
        c = np.arange(KEY_TILE) % tile
        out[n, np.arange(KEY_TILE), c // blk_len] = 1.0
        out[n, :, F_HI] = c // 256
        out[n, :, F_LO] = c % 256
        out[n, :, F_ONE] = 1.0
    return jnp.asarray(out, _BF)


def _importance_matrix(seq):
    ncp = seq // NSA_CMP_STRIDE
    nsel = seq // NSA_SEL_LEN
    ratio = NSA_SEL_LEN // NSA_CMP_STRIDE
    overlap = np.convolve(np.ones(ratio), np.ones(NSA_CMP_LEN // NSA_CMP_STRIDE))
    n_left = (NSA_CMP_LEN - NSA_CMP_STRIDE) // NSA_CMP_STRIDE
    m = np.zeros((ncp, -(-nsel // LANES) * LANES), np.float32)
    for j in range(nsel):
        for o, c in enumerate(overlap):
            n = ratio * j + o - n_left
            if 0 <= n < ncp - 1:
                m[n, j] += c
    return jnp.asarray(m, _BF)


def kernel(x, w_in, nsa_pe_k, nsa_pe_v, nsa_w_ck, nsa_w_cv, w_br_sb, w_br_nsa, w_br_moba, w_out, ln1_g, ln1_b, peer_wq, peer_k1, peer_k2, peer_u, peer_v, ln2_g, ln2_b):
    B, S, D = x.shape
    depth = w_in.shape[0]
    T = B * S
    assert S % MOBA_BLOCK == 0 and S >= NSA_WINDOW + Q_BLOCK
    alpha = (2.0 * depth) ** 0.25
    imp_mat = _importance_matrix(S)
    feats = _key_features()
    u_all = peer_u.astype(_BF)
    te = PEER_EXPERT_TILE
    vt_all = jnp.swapaxes(peer_v.reshape(depth, -1, te, D), 2, 3).astype(_BF)
    x2 = x.reshape(T, D)
    for l in range(depth):
        w_small, w_gate = _arrange_w_in(w_in[l], D)
        wc, pe = _arrange_compress(nsa_w_ck[l], nsa_w_cv[l], nsa_pe_k[l], nsa_pe_v[l])
        proj, ckv, ngate = _inproj(x2, w_small, feats)
        proj3 = proj.reshape(B, S, N_PROJ_GROUPS * LANES)
        gate3 = ngate.reshape(B, S, 3 * LANES)

        o_sb = _sb_attention(proj3)

        kc4, vc4 = _nsa_compress(ckv.reshape(B, S // NSA_CMP_STRIDE, NSA_CMP_STRIDE * LANES), wc, pe)
        o_cw, nsa_selm = _nsa_select(proj3, kc4, vc4, gate3, imp_mat)
        o_nsa = _blk_attention(proj3, nsa_selm, G_NQ, G_SK, G_SV, NSA_SEL_LEN, NSA_KEY_TILE, Q_BLOCK, NSA_SLOPES,
                               gate3=gate3, addend=o_cw)

        mb_selm = _moba_select(proj3, _moba_mean(proj3))
        o_mb = _blk_attention(proj3, mb_selm, G_MQ, G_MK, G_MV, MOBA_BLOCK, MOBA_KEY_TILE, MOBA_Q_ROWS, MOBA_SLOPES)

        w_br = jnp.stack([w_br_sb[l], w_br_nsa[l], w_br_moba[l]]).astype(_BF)
        x2 = _merge(x2, o_sb.reshape(T, LANES), o_nsa.reshape(T, LANES), o_mb.reshape(T, LANES),
                    w_gate, w_br, w_out[l].astype(_BF), ln1_g[l].reshape(1, D), ln1_b[l].reshape(1, D), alpha)

        wq = peer_wq[l].reshape(D, PEER_HEADS * PEER_QDIM).astype(_BF)
        tabs = _peer_route(x2, wq, peer_k1[l].astype(_BF), peer_k2[l].astype(_BF))
        x2 = _peer_dense(x2, u_all, vt_all, l, tabs, ln2_g[l].reshape(1, D), ln2_b[l].reshape(1, D), alpha)
    return x2.reshape(B, S, D)
```

```python
import functools
import math

import numpy as np
import jax
import jax.numpy as jnp
from jax import lax
from jax.experimental import pallas as pl
from jax.experimental.pallas import tpu as pltpu

HEAD_DIM = 32
N_HEADS = 4
Q_BLOCK = 128
NSA_CMP_LEN = 32
NSA_CMP_STRIDE = 16
NSA_SEL_LEN = 64
NSA_TOP_N = 8
NSA_WINDOW = 512
MOBA_BLOCK = 256
MOBA_TOPK = 3
PEER_HEADS = 4
PEER_NKEYS = 128
PEER_TOPK = 8
PEER_QDIM = 256
N_BRANCHES = 3
LN_EPS = 1e-5

LANES = 128
VMEM_LIMIT = 48 * 1024 * 1024

_BF = jnp.bfloat16
_F32 = jnp.float32
_NEG = -1e30

_ALIBI = [2.0 ** (-8.0 * (i + 1) / (2 * N_HEADS)) for i in range(2 * N_HEADS)]
NSA_SLOPES = tuple(_ALIBI[0::2])
MOBA_SLOPES = tuple(_ALIBI[1::2])

G_SBQ, G_SBK, G_SBV, G_NQ, G_SK, G_SKF, G_MK, G_MKF, G_SV, G_WK, G_WV, G_MQ, G_MV = range(13)
N_PROJ_GROUPS = 13
NSA_KEY_TILE = 512
MOBA_KEY_TILE = 1024
MOBA_Q_ROWS = 256
KEY_TILE = max(NSA_KEY_TILE, MOBA_KEY_TILE)
PEER_EXPERT_TILE = 1024
F_HI, F_LO, F_ONE = 125, 126, 127
MASK_BIAS = -1e30
SCORE_FLOOR = -5e29


def _dot(a, b):
    return jnp.dot(a.astype(_BF), b.astype(_BF), preferred_element_type=_F32)


def _dot_nt(a, b):
    return lax.dot_general(a.astype(_BF), b.astype(_BF), (((1,), (1,)), ((), ())),
                           preferred_element_type=_F32)


def _dot_split(a, b):
    hi = a.astype(_BF)
    lo = (a - hi.astype(_F32)).astype(_BF)
    return (jnp.dot(hi, b, preferred_element_type=_F32)
            + jnp.dot(lo, b, preferred_element_type=_F32))


def _iota(shape, dim):
    return lax.broadcasted_iota(jnp.int32, shape, dim)


def _lane_head(rows=Q_BLOCK):
    return jnp.right_shift(_iota((rows, LANES), 1), int(math.log2(HEAD_DIM)))


def _head_queries(q, lane_head):
    qf = q.astype(_F32)
    return [jnp.where(lane_head == h, qf, 0.0).astype(_BF) for h in range(N_HEADS)]


def _stack_heads(q, lane_head):
    return jnp.concatenate(_head_queries(q, lane_head), axis=0)


def _unstack_heads(acc, lane_head):
    out = jnp.zeros((Q_BLOCK, LANES), _F32)
    for h in range(N_HEADS):
        out = out + jnp.where(lane_head == h, acc[h * Q_BLOCK:(h + 1) * Q_BLOCK], 0.0)
    return out


def _first_max(x, ids, none):
    m = jnp.max(x, axis=1, keepdims=True)
    return m, jnp.min(jnp.where(x == m, ids, none), axis=1, keepdims=True)


def _params(*sem):
    return pltpu.CompilerParams(dimension_semantics=sem, vmem_limit_bytes=VMEM_LIMIT)


def _layer_norm(y, g, b):
    mu = jnp.mean(y, axis=-1, keepdims=True)
    d = y - mu
    var = jnp.mean(d * d, axis=-1, keepdims=True)
    return d * lax.rsqrt(var + LN_EPS) * g + b


def _inproj_kernel(x_ref, w_ref, feat_ref, proj_ref, ckv_ref, gate_ref):
    y = _dot(x_ref[...], w_ref[...])
    npj = N_PROJ_GROUPS * LANES
    proj_ref[...] = y[:, :npj].astype(proj_ref.dtype)
    proj_ref[:, G_SKF * LANES:(G_SKF + 1) * LANES] = feat_ref[0]
    proj_ref[:, G_MKF * LANES:(G_MKF + 1) * LANES] = feat_ref[1]
    ones = jnp.ones((y.shape[0], HEAD_DIM), proj_ref.dtype)
    proj_ref[:, G_SV * LANES + HEAD_DIM:G_SV * LANES + 2 * HEAD_DIM] = ones
    proj_ref[:, G_MV * LANES + HEAD_DIM:G_MV * LANES + 2 * HEAD_DIM] = ones
    ckv_ref[...] = y[:, npj:npj + LANES].astype(ckv_ref.dtype)
    gate_ref[...] = jax.nn.sigmoid(y[:, npj + LANES:])


def _inproj(x2, w_small, feats):
    T, D = x2.shape
    n = w_small.shape[1]
    npj = N_PROJ_GROUPS * LANES
    tm = KEY_TILE
    return pl.pallas_call(
        _inproj_kernel,
        grid=(T // tm,),
        in_specs=[pl.BlockSpec((tm, D), lambda i: (i, 0)),
                  pl.BlockSpec((D, n), lambda i: (0, 0)),
                  pl.BlockSpec((2, tm, LANES), lambda i: (0, 0, 0))],
        out_specs=[pl.BlockSpec((tm, npj), lambda i: (i, 0)),
                   pl.BlockSpec((tm, LANES), lambda i: (i, 0)),
                   pl.BlockSpec((tm, 3 * LANES), lambda i: (i, 0))],
        out_shape=[jax.ShapeDtypeStruct((T, npj), _BF),
                   jax.ShapeDtypeStruct((T, LANES), _BF),
                   jax.ShapeDtypeStruct((T, 3 * LANES), _F32)],
        compiler_params=_params("parallel"),
        name="inproj",
    )(x2, w_small, feats)


_SB_LOG_CUTOFF = -104.0


def _sb_kernel(q_ref, k_ref, v_ref, o_ref):
    i = pl.program_id(1)
    lane_head = _lane_head()
    qs = _stack_heads(q_ref[0], lane_head)
    rows = N_HEADS * Q_BLOCK
    tri = (_iota((Q_BLOCK, Q_BLOCK), 0) > _iota((Q_BLOCK, Q_BLOCK), 1)).astype(_BF)
    q_off = jnp.bitwise_and(_iota((rows, Q_BLOCK), 0), Q_BLOCK - 1)
    diag_past = _iota((rows, Q_BLOCK), 1) < q_off

    def tile(j, carry, acc, diagonal):
        start = pl.multiple_of(j * Q_BLOCK, Q_BLOCK)
        kt = k_ref[0, pl.ds(start, Q_BLOCK), :]
        vt = v_ref[0, pl.ds(start, Q_BLOCK), :]
        z = _dot_nt(qs, kt)
        ls = -(jnp.maximum(z, 0.0) + jnp.log(1.0 + jnp.exp(-jnp.abs(z))))
        if diagonal:
            ls = jnp.where(diag_past, ls, 0.0)
        excl = _dot_split(ls, tri)
        w = jnp.exp(z + ls + excl + carry)
        if diagonal:
            w = jnp.where(diag_past, w, 0.0)
        acc = acc + _dot(w, vt)
        carry = carry + jnp.sum(ls, axis=1, keepdims=True)
        return carry, acc

    carry, acc = tile(i, jnp.zeros((rows, 1), _F32), jnp.zeros((rows, LANES), _F32), True)

    def cond(st):
        j, _, _, cmax = st
        return jnp.logical_and(j >= 0, cmax > _SB_LOG_CUTOFF)

    def body(st):
        j, carry, acc, _ = st
        carry, acc = tile(j, carry, acc, False)
        return j - 1, carry, acc, jnp.max(carry)

    _, _, acc, _ = lax.while_loop(cond, body, (i - 1, carry, acc, jnp.max(carry)))
    o_ref[0] = _unstack_heads(acc, lane_head).astype(o_ref.dtype)


def _sb_attention(proj3):
    B, S, _ = proj3.shape
    return pl.pallas_call(
        _sb_kernel,
        grid=(B, S // Q_BLOCK),
        in_specs=[pl.BlockSpec((1, Q_BLOCK, LANES), lambda b, i: (b, i, G_SBQ)),
                  pl.BlockSpec((1, S, LANES), lambda b, i: (b, 0, G_SBK)),
                  pl.BlockSpec((1, S, LANES), lambda b, i: (b, 0, G_SBV))],
        out_specs=pl.BlockSpec((1, Q_BLOCK, LANES), lambda b, i: (b, i, 0)),
        out_shape=jax.ShapeDtypeStruct((B, S, LANES), _BF),
        compiler_params=_params("parallel", "arbitrary"),
        name="sb_attn",
    )(proj3, proj3, proj3)


def _nsa_compress_kernel(c_ref, w_ref, pe_ref, kc_ref, vc_ref):
    c = c_ref[0]
    a = _dot(c, w_ref[0]) + _dot(pe_ref[0], w_ref[0])[0:1]
    b = _dot(c, w_ref[1]) + _dot(pe_ref[1], w_ref[1])[0:1]
    n = a.shape[0]
    b_next = pltpu.roll(b, n - 1, 0)
    y = a + b_next
    kc_ref[0] = y[:, :LANES].astype(kc_ref.dtype)
    vc_ref[0] = y[:, LANES:].astype(vc_ref.dtype)


def _nsa_compress(ckv3, wc, pe):
    B, nchunk, width = ckv3.shape
    return pl.pallas_call(
        _nsa_compress_kernel,
        grid=(B,),
        in_specs=[pl.BlockSpec((1, nchunk, width), lambda b: (b, 0, 0)),
                  pl.BlockSpec((2, width, 2 * LANES), lambda b: (0, 0, 0)),
                  pl.BlockSpec((2, 8, width), lambda b: (0, 0, 0))],
        out_specs=[pl.BlockSpec((1, nchunk, LANES), lambda b: (b, 0, 0)),
                   pl.BlockSpec((1, nchunk, LANES), lambda b: (b, 0, 0))],
        out_shape=[jax.ShapeDtypeStruct((B, nchunk, LANES), _BF),
                   jax.ShapeDtypeStruct((B, nchunk, LANES), _BF)],
        compiler_params=_params("parallel"),
        name="nsa_compress",
    )(ckv3, wc, pe)


def _masked_exp(z, valid):
    zm = jnp.where(valid, z, MASK_BIAS)
    m = jnp.maximum(jnp.max(zm, axis=1, keepdims=True), SCORE_FLOOR)
    p = jnp.exp(zm - m)
    return p, 1.0 / jnp.maximum(jnp.sum(p, axis=1, keepdims=True), 1e-30)


def _nsa_select_kernel(q_ref, kc_ref, vc_ref, wk_ref, wv_ref, g_ref, m_ref, ocw_ref, selm_ref,
                       ocmp_ref, imp_ref, *, seq):
    i = pl.program_id(1)
    q0 = i * Q_BLOCK
    ncp = seq // NSA_CMP_STRIDE
    nsel = m_ref.shape[1]
    win = NSA_WINDOW + Q_BLOCK
    lane_head = _lane_head()
    qs = _stack_heads(q_ref[0], lane_head)
    head_rows = lambda a, h: a[h * Q_BLOCK:(h + 1) * Q_BLOCK]

    def compressed(width):
        cend = _iota((1, width), 1) * NSA_CMP_STRIDE + (NSA_CMP_LEN - 1)
        valid_c = (q0 + _iota((Q_BLOCK, width), 0)) >= cend
        rel_c = (cend - q0).astype(_F32)
        zc = _dot_nt(qs, kc_ref[0, :width, :])
        pg = jnp.zeros((Q_BLOCK, width), _F32)
        pcs = []
        for h in range(N_HEADS):
            p, inv = _masked_exp(head_rows(zc, h) + NSA_SLOPES[h] * rel_c, valid_c)
            p = p * inv
            pcs.append(p.astype(_BF))
            pg = pg + p
        ocmp_ref[...] = _unstack_heads(_dot(jnp.concatenate(pcs, axis=0), vc_ref[0, :width, :]), lane_head)
        imp_ref[...] = _dot_split(pg, m_ref[:width, :])

    group = min(ncp, 2 * LANES)
    n_valid = jnp.right_shift(q0 + Q_BLOCK - NSA_CMP_LEN, int(math.log2(NSA_CMP_STRIDE))) + 1
    groups = jnp.minimum((n_valid + group - 1) // group, ncp // group)
    for k in range(1, ncp // group + 1):
        pl.when(groups == k)(functools.partial(compressed, k * group))
    o_cmp = ocmp_ref[...]

    imp = imp_ref[...]
    blk = _iota((Q_BLOCK, nsel), 1)
    cur = jnp.right_shift(q0 + _iota((Q_BLOCK, nsel), 0), int(math.log2(NSA_SEL_LEN)))
    forced = (blk == 0) | (blk == cur) | (blk == cur - 1)
    x = jnp.where(blk > cur, -jnp.inf, jnp.where(forced, jnp.inf, imp))
    blkf = blk.astype(_F32)
    sel = jnp.zeros((Q_BLOCK, nsel), _F32)
    for _ in range(min(NSA_TOP_N, seq // NSA_SEL_LEN)):
        _, idx = _first_max(x, blkf, float(nsel))
        hit = blkf == idx
        sel = jnp.where(hit, 1.0, sel)
        x = jnp.where(hit, -jnp.inf, x)
    selm_ref[0] = sel.astype(selm_ref.dtype)

    ws = pl.multiple_of(jnp.maximum(q0 - NSA_WINDOW, 0), Q_BLOCK)
    kw = wk_ref[0, pl.ds(ws, win), :]
    vw = wv_ref[0, pl.ds(ws, win), :]
    kpos = ws + _iota((1, win), 1)
    dw = (q0 + _iota((Q_BLOCK, win), 0)) - kpos
    valid_w = (dw >= 0) & (dw < NSA_WINDOW)
    rel_w = (kpos - q0).astype(_F32)
    zw = _dot_nt(qs, kw)
    pws, invs = [], []
    for h in range(N_HEADS):
        p, inv = _masked_exp(head_rows(zw, h) + NSA_SLOPES[h] * rel_w, valid_w)
        pws.append(p.astype(_BF))
        invs.append(jnp.broadcast_to(inv, (Q_BLOCK, LANES)))
    o_win = _unstack_heads(_dot(jnp.concatenate(pws, axis=0), vw) * jnp.concatenate(invs, axis=0), lane_head)

    g = g_ref[0]
    ocw_ref[0] = g[:, :LANES] * o_cmp + g[:, 2 * LANES:] * o_win


def _nsa_select(proj3, kc4, vc4, gate3, imp_mat):
    B, S, _ = proj3.shape
    ncp = S // NSA_CMP_STRIDE
    nsel = imp_mat.shape[1]
    return pl.pallas_call(
        functools.partial(_nsa_select_kernel, seq=S),
        grid=(B, S // Q_BLOCK),
        in_specs=[pl.BlockSpec((1, Q_BLOCK, LANES), lambda b, i: (b, i, G_NQ)),
                  pl.BlockSpec((1, ncp, LANES), lambda b, i: (b, 0, 0)),
                  pl.BlockSpec((1, ncp, LANES), lambda b, i: (b, 0, 0)),
                  pl.BlockSpec((1, S, LANES), lambda b, i: (b, 0, G_WK)),
                  pl.BlockSpec((1, S, LANES), lambda b, i: (b, 0, G_WV)),
                  pl.BlockSpec((1, Q_BLOCK, 3 * LANES), lambda b, i: (b, i, 0)),
                  pl.BlockSpec((ncp, nsel), lambda b, i: (0, 0))],
        out_specs=[pl.BlockSpec((1, Q_BLOCK, LANES), lambda b, i: (b, i, 0)),
                   pl.BlockSpec((1, Q_BLOCK, nsel), lambda b, i: (b, i, 0))],
        out_shape=[jax.ShapeDtypeStruct((B, S, LANES), _F32),
                   jax.ShapeDtypeStruct((B, S, nsel), _BF)],
        scratch_shapes=[pltpu.VMEM((Q_BLOCK, LANES), _F32), pltpu.VMEM((Q_BLOCK, nsel), _F32)],
        compiler_params=_params("parallel", "arbitrary"),
        name="nsa_select",
    )(proj3, kc4, vc4, proj3, proj3, gate3, imp_mat)


def _moba_mean_kernel(k_ref, o_ref, *, nb):
    k = k_ref[0].astype(_F32)
    o_ref[0] = jnp.zeros(o_ref.shape[1:], _F32)
    o_ref[0, :nb, :] = jnp.mean(k.reshape(nb, MOBA_BLOCK, LANES), axis=1)


def _moba_mean(proj3):
    B, S, _ = proj3.shape
    nb = S // MOBA_BLOCK
    return pl.pallas_call(
        functools.partial(_moba_mean_kernel, nb=nb),
        grid=(B,),
        in_specs=[pl.BlockSpec((1, S, LANES), lambda b: (b, 0, G_MK))],
        out_specs=pl.BlockSpec((1, LANES, LANES), lambda b: (b, 0, 0)),
        out_shape=jax.ShapeDtypeStruct((B, LANES, LANES), _F32),
        compiler_params=_params("parallel"),
        name="moba_mean",
    )(proj3)


def _moba_select_kernel(q_ref, km_ref, selm_ref, *, nb):
    i = pl.program_id(1)
    q0 = i * Q_BLOCK
    sg = _dot_nt(q_ref[0], km_ref[0])
    blk = _iota((Q_BLOCK, LANES), 1)
    cur = jnp.right_shift(q0, int(math.log2(MOBA_BLOCK)))
    x = jnp.where(blk < cur, sg, -jnp.inf)
    blkf = blk.astype(_F32)
    sel = jnp.where(blk == cur, 1.0, 0.0)
    for _ in range(min(MOBA_TOPK, nb)):
        m, idx = _first_max(x, blkf, float(LANES))
        hit = blkf == idx
        sel = jnp.where(hit & (m > -jnp.inf), 1.0, sel)
        x = jnp.where(hit, -jnp.inf, x)
    selm_ref[0] = sel.astype(selm_ref.dtype)


def _moba_select(proj3, kmean):
    B, S, _ = proj3.shape
    nb = S // MOBA_BLOCK
    assert nb <= LANES
    return pl.pallas_call(
        functools.partial(_moba_select_kernel, nb=nb),
        grid=(B, S // Q_BLOCK),
        in_specs=[pl.BlockSpec((1, Q_BLOCK, LANES), lambda b, i: (b, i, G_MQ)),
                  pl.BlockSpec((1, LANES, LANES), lambda b, i: (b, 0, 0))],
        out_specs=pl.BlockSpec((1, Q_BLOCK, LANES), lambda b, i: (b, i, 0)),
        out_shape=jax.ShapeDtypeStruct((B, S, LANES), _BF),
        compiler_params=_params("parallel", "arbitrary"),
        name="moba_select",
    )(proj3, kmean)


def _blk_attn_kernel(*refs, blk_len, tile, qb, slopes, nblk, seq, gated):
    if gated:
        q_ref, kx_ref, v_ref, selm_ref, g_ref, add_ref, o_ref = refs[:7]
    else:
        q_ref, kx_ref, v_ref, selm_ref, o_ref = refs[:5]
    qx_ref, m_ref, acc_ref, za_ref, zb_ref, flag_ref, list_ref = refs[-7:]
    rows = N_HEADS * qb
    bpt = tile // blk_len
    max_tiles = seq // tile
    i = pl.program_id(1)
    q0 = i * qb
    lane_head = _lane_head(qb)
    lane = _iota((qb, LANES), 1)

    qx_ref[:, :LANES] = _stack_heads(q_ref[0], lane_head)
    m_ref[...] = jnp.full((rows, LANES), SCORE_FLOOR, _F32)
    acc_ref[...] = jnp.zeros((rows, LANES), _F32)
    lane1 = _iota((1, LANES), 1)
    fixed_lanes = [jnp.where(lane1 == F_HI, slopes[h] * 256.0, jnp.where(lane1 == F_LO, slopes[h], 0.0))
                   for h in range(N_HEADS)]
    dist_lanes = [jnp.where(lane1 == F_ONE, slopes[h] * LANES, 0.0) for h in range(N_HEADS)]

    selm = selm_ref[0].astype(_F32)
    sneg = (1.0 - selm) * MASK_BIAS
    n_tiles = jnp.right_shift(q0 + qb + tile - 1, int(math.log2(tile)))

    def scores(j, z_ref):
        k0 = pl.multiple_of(j * tile, tile)
        first = j * bpt
        if nblk > LANES:
            half = jnp.right_shift(first, int(math.log2(LANES)))
            base = sneg[:, :LANES]
            for c in range(1, nblk // LANES):
                base = jnp.where(half == c, sneg[:, c * LANES:(c + 1) * LANES], base)
        else:
            base = sneg
        shift = jnp.bitwise_and(LANES - jnp.bitwise_and(first, LANES - 1), LANES - 1)
        tile_mask = pltpu.roll(base, shift, 1)
        tile_mask = jnp.where(lane >= F_HI, 0.0, tile_mask)
        dist = (j * (tile // LANES) - i * (qb // LANES)).astype(_F32)
        for h in range(N_HEADS):
            ext = tile_mask + (fixed_lanes[h] + dist * dist_lanes[h])
            qx_ref[h * qb:(h + 1) * qb, LANES:] = ext.astype(_BF)
        kx = kx_ref[0, pl.ds(k0, tile), :]
        z_ref[...] = lax.dot_general(qx_ref[...], kx, (((1,), (1,)), ((), ())),
                                     preferred_element_type=_F32)

    def absorb(z_ref, j, diagonal):
        k0 = pl.multiple_of(j * tile, tile)
        vt = v_ref[0, pl.ds(k0, tile), :]
        z = z_ref[...]
        if diagonal:
            q_off = jnp.bitwise_and(_iota((rows, tile), 0), qb - 1)
            z = jnp.where(_iota((rows, tile), 1) - q_off <= q0 - k0, z, MASK_BIAS)
        m_old = m_ref[...]
        m_new = jnp.maximum(m_old, jnp.max(z, axis=1, keepdims=True))
        p = jnp.exp(z - jnp.concatenate([m_new] * (tile // LANES), axis=1))
        a = jnp.exp(m_old - m_new)
        acc_ref[...] = a * acc_ref[...] + _dot(p, vt)
        m_ref[...] = m_new

    scores(n_tiles - 1, za_ref)

    any_q = jnp.broadcast_to(jnp.max(selm, axis=0, keepdims=True), (8, nblk))
    group = (jnp.right_shift(_iota((nblk, LANES), 0), int(math.log2(bpt))) == _iota((nblk, LANES), 1))
    tile_hits = _dot(any_q, jnp.where(group, 1.0, 0.0))
    for t in range(max_tiles):
        flag_ref[t] = (tile_hits[0, t] > 0.5).astype(jnp.int32)

    def scan(t, n):
        @pl.when(flag_ref[t] > 0)
        def _():
            list_ref[n] = t
        return n + (flag_ref[t] > 0).astype(jnp.int32)

    n_list = lax.fori_loop(0, n_tiles - 1, scan, 0)

    @pl.when(n_list == 0)
    def _():
        absorb(za_ref, n_tiles - 1, True)

    @pl.when(n_list > 0)
    def _():
        scores(list_ref[0], zb_ref)
        absorb(za_ref, n_tiles - 1, True)

    def body(kk, c):
        scores(list_ref[2 * kk + 1], za_ref)
        absorb(zb_ref, list_ref[2 * kk], False)
        scores(list_ref[2 * kk + 2], zb_ref)
        absorb(za_ref, list_ref[2 * kk + 1], False)
        return c

    n_pairs = jnp.right_shift(n_list - 1, 1)
    lax.fori_loop(0, n_pairs, body, 0)
    left = n_list - 1 - 2 * n_pairs

    @pl.when(jnp.logical_and(n_list > 0, left == 0))
    def _():
        absorb(zb_ref, list_ref[n_list - 1], False)

    @pl.when(jnp.logical_and(n_list > 0, left == 1))
    def _():
        scores(list_ref[n_list - 1], za_ref)
        absorb(zb_ref, list_ref[n_list - 2], False)
        absorb(za_ref, list_ref[n_list - 1], False)

    acc = acc_ref[...]
    total = pltpu.roll(acc, LANES - HEAD_DIM, 1)
    o = acc * (1.0 / jnp.maximum(total, 1e-30))
    out = jnp.zeros((qb, LANES), _F32)
    for h in range(N_HEADS):
        o_h = o[h * qb:(h + 1) * qb]
        if h:
            o_h = pltpu.roll(o_h, h * HEAD_DIM, 1)
        out = out + jnp.where(lane_head == h, o_h, 0.0)
    if gated:
        out = add_ref[0] + g_ref[0] * out
    o_ref[0] = out.astype(o_ref.dtype)


def _blk_attention(proj3, selm, gq, gkx, gv, blk_len, tile, qb, slopes, gate3=None, addend=None):
    B, S, _ = proj3.shape
    nblk = selm.shape[-1]
    assert S % tile == 0 and gkx % 2 == 0 and nblk % LANES == 0
    gated = gate3 is not None
    rows = N_HEADS * qb
    in_specs = [pl.BlockSpec((1, qb, LANES), lambda b, i: (b, i, gq)),
                pl.BlockSpec((1, S, 2 * LANES), lambda b, i: (b, 0, gkx // 2)),
                pl.BlockSpec((1, S, LANES), lambda b, i: (b, 0, gv)),
                pl.BlockSpec((1, qb, nblk), lambda b, i: (b, i, 0))]
    args = [proj3, proj3, proj3, selm]
    if gated:
        in_specs += [pl.BlockSpec((1, qb, LANES), lambda b, i: (b, i, 1)),
                     pl.BlockSpec((1, qb, LANES), lambda b, i: (b, i, 0))]
        args += [gate3, addend]
    return pl.pallas_call(
        functools.partial(_blk_attn_kernel, blk_len=blk_len, tile=tile, qb=qb, slopes=slopes, nblk=nblk, seq=S,
                          gated=gated),
        grid=(B, S // qb),
        in_specs=in_specs,
        out_specs=pl.BlockSpec((1, qb, LANES), lambda b, i: (b, i, 0)),
        out_shape=jax.ShapeDtypeStruct((B, S, LANES), _BF),
        scratch_shapes=[pltpu.VMEM((rows, 2 * LANES), _BF),
                        pltpu.VMEM((rows, LANES), _F32),
                        pltpu.VMEM((rows, LANES), _F32),
                        pltpu.VMEM((rows, tile), _F32),
                        pltpu.VMEM((rows, tile), _F32),
                        pltpu.SMEM((S // tile,), jnp.int32),
                        pltpu.SMEM((S // tile,), jnp.int32)],
        compiler_params=_params("parallel", "arbitrary"),
        name="blk_attn_%d" % blk_len,
    )(*args)


def _merge_kernel(x_ref, osb_ref, onsa_ref, omb_ref, wg_ref, wbr_ref, wo_ref, lg_ref, lb_ref, o_ref,
                  *, alpha):
    x = x_ref[...]
    d = x.shape[1]
    gates = jax.nn.sigmoid(_dot(x, wg_ref[...]))
    mix = (gates[:, :d] * _dot(osb_ref[...], wbr_ref[0])
           + gates[:, d:2 * d] * _dot(onsa_ref[...], wbr_ref[1])
           + gates[:, 2 * d:] * _dot(omb_ref[...], wbr_ref[2]))
    y = alpha * x + _dot(mix, wo_ref[...])
    o_ref[...] = _layer_norm(y, lg_ref[...], lb_ref[...])


def _merge(x2, o_sb, o_nsa, o_mb, w_gate, w_br, w_out, ln_g, ln_b, alpha, tm=256):
    T, D = x2.shape
    row = lambda i: (i, 0)
    fixed2 = lambda i: (0, 0)
    return pl.pallas_call(
        functools.partial(_merge_kernel, alpha=alpha),
        grid=(T // tm,),
        in_specs=[pl.BlockSpec((tm, D), row),
                  pl.BlockSpec((tm, LANES), row),
                  pl.BlockSpec((tm, LANES), row),
                  pl.BlockSpec((tm, LANES), row),
                  pl.BlockSpec((D, N_BRANCHES * D), fixed2),
                  pl.BlockSpec((N_BRANCHES, LANES, D), lambda i: (0, 0, 0)),
                  pl.BlockSpec((D, D), fixed2),
                  pl.BlockSpec((1, D), fixed2),
                  pl.BlockSpec((1, D), fixed2)],
        out_specs=pl.BlockSpec((tm, D), row),
        out_shape=jax.ShapeDtypeStruct((T, D), _F32),
        compiler_params=_params("parallel"),
        name="merge",
    )(x2, o_sb, o_nsa, o_mb, w_gate, w_br, w_out, ln_g, ln_b)


_NOT_RETRIEVED = 99.0


def _top_rows(s, k, exact):
    n = s.shape[0]
    rows = _iota(s.shape, 0).astype(_F32)
    rank = jnp.full(s.shape, _NOT_RETRIEVED, _F32)
    vals = []
    for p in range(k):
        m = jnp.max(s, axis=0, keepdims=True)
        hit = s == m
        if exact:
            hit = rows == jnp.min(jnp.where(hit, rows, float(n)), axis=0, keepdims=True)
        vals.append(m)
        rank = jnp.where(hit, float(p), rank)
        s = jnp.where(hit, -jnp.inf, s)
    return vals, rank


def _peer_route_kernel(x_ref, wq_ref, k1_ref, k2_ref, fa_ref, qb_ref, e1_ref, e2_ref):
    half = PEER_QDIM // 2
    qf = _dot(x_ref[...], wq_ref[...]).astype(_BF)
    tm = qf.shape[0]
    rows_k = _iota((PEER_TOPK, tm), 0)
    ncand = PEER_TOPK * PEER_TOPK
    pos = _iota((ncand, tm), 0).astype(_F32)

    def route(exact):
        most = jnp.zeros((1, tm), _F32)
        for h in range(PEER_HEADS):
            s1 = _dot_nt(k1_ref[h], qf[:, h * PEER_QDIM:h * PEER_QDIM + half])
            s2 = _dot_nt(k2_ref[h], qf[:, h * PEER_QDIM + half:(h + 1) * PEER_QDIM])
            v1, r1 = _top_rows(s1, PEER_TOPK, exact)
            v2, r2 = _top_rows(s2, PEER_TOPK, exact)
            for r in (r1, r2):
                most = jnp.maximum(most, jnp.sum(jnp.where(r < _NOT_RETRIEVED, 1.0, 0.0), axis=0, keepdims=True))
            v2_all = jnp.zeros((PEER_TOPK, tm), _F32)
            for qi in range(PEER_TOPK):
                v2_all = jnp.where(rows_k == qi, v2[qi], v2_all)
            cand = jnp.concatenate([v1[p] + v2_all for p in range(PEER_TOPK)], axis=0)
            c = cand
            pickf = jnp.zeros(cand.shape, _F32)
            for _ in range(PEER_TOPK):
                m = jnp.max(c, axis=0, keepdims=True)
                idx = jnp.min(jnp.where(c == m, pos, float(ncand)), axis=0, keepdims=True)
                hit = pos == idx
                pickf = jnp.where(hit, 1.0, pickf)
                c = jnp.where(hit, -jnp.inf, c)
            cmax = v1[0] + v2[0]
            z = jnp.sum(pickf * jnp.exp(cand - cmax), axis=0, keepdims=True)
            fa = jnp.zeros((PEER_NKEYS, tm), _F32)
            for p in range(PEER_TOPK):
                count = jnp.sum(pickf[p * PEER_TOPK:(p + 1) * PEER_TOPK], axis=0, keepdims=True)
                fa = jnp.where(r1 == float(p), count, fa)
            fa_ref[h] = fa
            qb_ref[h] = r2.astype(qb_ref.dtype)
            e1_ref[h] = jnp.exp(s1 - v1[0]) * (1.0 / z)
            e2_ref[h] = jnp.exp(s2 - v2[0]).astype(e2_ref.dtype)
        return jnp.max(most)

    @pl.when(route(False) > PEER_TOPK)
    def _():
        route(True)


def _peer_route(x2, wq, k1, k2, tm=256):
    T, D = x2.shape
    tab = jax.ShapeDtypeStruct((PEER_HEADS, PEER_NKEYS, T), _F32)
    tab_bf = jax.ShapeDtypeStruct((PEER_HEADS, PEER_NKEYS, T), _BF)
    tab_spec = pl.BlockSpec((PEER_HEADS, PEER_NKEYS, tm), lambda i: (0, 0, i))
    return pl.pallas_call(
        _peer_route_kernel,
        grid=(T // tm,),
        in_specs=[pl.BlockSpec((tm, D), lambda i: (i, 0)),
                  pl.BlockSpec(wq.shape, lambda i: (0, 0)),
                  pl.BlockSpec(k1.shape, lambda i: (0, 0, 0)),
                  pl.BlockSpec(k2.shape, lambda i: (0, 0, 0))],
        out_specs=[tab_spec] * 4,
        out_shape=[tab, tab_bf, tab, tab_bf],
        compiler_params=_params("parallel"),
        name="peer_route",
    )(x2, wq, k1, k2)


def _gelu_tanh(s):
    c1 = math.sqrt(2.0 / math.pi)
    k1 = jnp.asarray(c1, s.dtype)
    k2 = jnp.asarray(c1 * 0.044715, s.dtype)
    inner = s * (k1 + k2 * (s * s))
    half = jnp.asarray(0.5, s.dtype) * s
    return half + half * jnp.tanh(inner)


def _peer_dense_kernel(x_ref, xr_ref, u_ref, vt_ref, fa_ref, qb_ref, e1_ref, e2_ref, lg_ref, lb_ref, o_ref,
                       acc_ref, xb_ref, s_ref, c_ref, *, alpha, te, n_tiles, n_work):
    s_idx = pl.program_id(0)
    j_score = s_idx % n_tiles
    j_gate = jnp.maximum(s_idx - 1, 0) % n_tiles
    j_value = jnp.maximum(s_idx - 2, 0) % n_tiles

    @pl.when(s_idx == 0)
    def _():
        s_ref[...] = jnp.zeros_like(s_ref)
        c_ref[...] = jnp.zeros_like(c_ref)
        acc_ref[...] = jnp.zeros_like(acc_ref)

    @pl.when(jnp.logical_and(j_score == 0, s_idx < n_work))
    def _():
        xb_ref[...] = x_ref[...].astype(_BF)

    fresh = j_value == 0
    cur = s_idx % 2
    tm = s_ref.shape[2]
    tc = tm // 2
    for ck in range(tm // tc):
        cols = slice(ck * tc, (ck + 1) * tc)
        acc_ref[:, cols] = jnp.where(fresh, 0.0, acc_ref[:, cols]) + jnp.dot(
            vt_ref[...], c_ref[1 - cur, :, cols], preferred_element_type=_F32)
        s = s_ref[1 - cur, :, cols]
        act = _gelu_tanh(s)
        for r in range(te // PEER_NKEYS):
            a = j_gate * (te // PEER_NKEYS) + r
            gate = jnp.zeros((PEER_NKEYS, tc), _BF)
            for h in range(PEER_HEADS):
                fa = fa_ref[h, pl.ds(a, 1), cols].astype(_BF)
                e1 = e1_ref[h, pl.ds(a, 1), cols].astype(_BF)
                gate = gate + e1 * jnp.where(qb_ref[h, :, cols] < fa, e2_ref[h, :, cols], jnp.zeros((), _BF))
            c_ref[cur, r * PEER_NKEYS:(r + 1) * PEER_NKEYS, cols] = (
                gate * act[r * PEER_NKEYS:(r + 1) * PEER_NKEYS])
        s_ref[cur, :, cols] = _dot_nt(u_ref[...], xb_ref[cols, :]).astype(_BF)

    @pl.when(jnp.logical_and(j_value == n_tiles - 1, s_idx >= 2))
    def _():
        y = alpha * xr_ref[...] + acc_ref[...].T
        o_ref[...] = _layer_norm(y, lg_ref[...], lb_ref[...])


def _peer_dense(x2, u_all, vt_all, layer, tabs, ln_g, ln_b, alpha, tm=512):
    T, D = x2.shape
    n_tiles, te = vt_all.shape[1], vt_all.shape[3]
    n_tok = T // tm
    n_work = n_tok * n_tiles
    last = n_work - 1
    tok = lambda lag: (lambda s: (jnp.clip(s - lag, 0, last) // n_tiles, 0))
    exp_tile = lambda lag: (lambda s: jnp.clip(s - lag, 0, last) % n_tiles)
    tab_spec = pl.BlockSpec((PEER_HEADS, PEER_NKEYS, tm), lambda s: (0, 0, jnp.clip(s - 1, 0, last) // n_tiles))
    return pl.pallas_call(
        functools.partial(_peer_dense_kernel, alpha=alpha, te=te, n_tiles=n_tiles, n_work=n_work),
        grid=(n_work + 2,),
        in_specs=[pl.BlockSpec((tm, D), tok(0)),
                  pl.BlockSpec((tm, D), tok(2)),
                  pl.BlockSpec((None, te, D), lambda s: (layer, exp_tile(0)(s), 0)),
                  pl.BlockSpec((None, None, D, te), lambda s: (layer, exp_tile(2)(s), 0, 0)),
                  tab_spec, tab_spec, tab_spec, tab_spec,
                  pl.BlockSpec((1, D), lambda s: (0, 0)),
                  pl.BlockSpec((1, D), lambda s: (0, 0))],
        out_specs=pl.BlockSpec((tm, D), tok(2)),
        out_shape=jax.ShapeDtypeStruct((T, D), _F32),
        scratch_shapes=[pltpu.VMEM((D, tm), _F32), pltpu.VMEM((tm, D), _BF),
                        pltpu.VMEM((2, te, tm), _BF), pltpu.VMEM((2, te, tm), _BF)],
        compiler_params=_params("arbitrary"),
        name="peer_dense",
    )(x2, x2, u_all, vt_all, *tabs, ln_g, ln_b)


def _in_widths(d_model):
    w = N_HEADS * HEAD_DIM
    return (w, w, w, w) + (HEAD_DIM,) * 6 + (3 * N_HEADS, w, HEAD_DIM, HEAD_DIM, N_BRANCHES * d_model)


def _arrange_w_in(w_in, d_model):
    off = np.concatenate([[0], np.cumsum(_in_widths(d_model))])
    scale = HEAD_DIM ** -0.5
    seg = lambda n: w_in[:, off[n]:off[n + 1]]
    rep = lambda n: jnp.tile(seg(n), (1, N_HEADS))
    gate_cols = np.array([off[10] + h * 3 + c for c in range(3) for h in range(N_HEADS)
                          for _ in range(HEAD_DIM)])
    ckv = jnp.concatenate([seg(4), seg(5), jnp.zeros((w_in.shape[0], LANES - 2 * HEAD_DIM), w_in.dtype)], axis=1)
    blank = jnp.zeros((w_in.shape[0], LANES), w_in.dtype)
    once = lambda n: jnp.concatenate([seg(n), blank[:, HEAD_DIM:]], axis=1)
    groups = [seg(0) * scale, seg(1), seg(2), seg(3) * scale, rep(6), blank, rep(12), blank,
              once(7), rep(8), rep(9), seg(11) * scale, once(13), ckv, w_in[:, gate_cols]]
    return jnp.concatenate(groups, axis=1).astype(_BF), seg(14).astype(_BF)


def _arrange_compress(w_ck, w_cv, pe_k, pe_v):
    half = NSA_CMP_STRIDE
    wk = jnp.tile(w_ck.reshape(2, half, HEAD_DIM, HEAD_DIM), (1, 1, 1, N_HEADS))
    wv = jnp.tile(w_cv.reshape(2, half, HEAD_DIM, HEAD_DIM), (1, 1, 1, N_HEADS))
    w = jnp.zeros((2, half, LANES, 2 * LANES), _F32)
    w = w.at[:, :, :HEAD_DIM, :LANES].set(wk)
    w = w.at[:, :, HEAD_DIM:2 * HEAD_DIM, LANES:].set(wv)
    pe = jnp.zeros((2, half, LANES), _F32)
    pe = pe.at[:, :, :HEAD_DIM].set(pe_k.reshape(2, half, HEAD_DIM))
    pe = pe.at[:, :, HEAD_DIM:2 * HEAD_DIM].set(pe_v.reshape(2, half, HEAD_DIM))
    pe = jnp.broadcast_to(pe.reshape(2, 1, half * LANES), (2, 8, half * LANES))
    return w.reshape(2, half * LANES, 2 * LANES).astype(_BF), pe.astype(_BF)


def _key_features():
    out = np.zeros((2, KEY_TILE, LANES), np.float32)
    for n, (blk_len, tile) in enumerate(((NSA_SEL_LEN, NSA_KEY_TILE), (MOBA_BLOCK, MOBA_KEY_TILE))):
        c = np.arange(KEY_TILE) % tile
        out[n, np.arange(KEY_TILE), c // blk_len] = 1.0
        out[n, :, F_HI] = c // 256
        out[n, :, F_LO] = c % 256
        out[n, :, F_ONE] = 1.0
    return jnp.asarray(out, _BF)


def _importance_matrix(seq):
    ncp = seq // NSA_CMP_STRIDE
    nsel = seq // NSA_SEL_LEN
    ratio = NSA_SEL_LEN // NSA_CMP_STRIDE
    overlap = np.convolve(np.ones(ratio), np.ones(NSA_CMP_LEN // NSA_CMP_STRIDE))
    n_left = (NSA_CMP_LEN - NSA_CMP_STRIDE) // NSA_CMP_STRIDE
    m = np.zeros((ncp, -(-nsel // LANES) * LANES), np.float32)
    for j in range(nsel):
        for o, c in enumerate(overlap):
            n = ratio * j + o - n_left
            if 0 <= n < ncp - 1:
                m[n, j] += c
    return jnp.asarray(m, _BF)


def kernel(x, w_in, nsa_pe_k, nsa_pe_v, nsa_w_ck, nsa_w_cv, w_br_sb, w_br_nsa, w_br_moba, w_out, ln1_g, ln1_b, peer_wq, peer_k1, peer_k2, peer_u, peer_v, ln2_g, ln2_b):
    B, S, D = x.shape
    depth = w_in.shape[0]
    T = B * S
    assert S % MOBA_BLOCK == 0 and S >= NSA_WINDOW + Q_BLOCK
    alpha = (2.0 * depth) ** 0.25
    imp_mat = _importance_matrix(S)
    feats = _key_features()
    u_all = peer_u.astype(_BF)
    te = PEER_EXPERT_TILE
    vt_all = jnp.swapaxes(peer_v.reshape(depth, -1, te, D), 2, 3).astype(_BF)
    x2 = x.reshape(T, D)
    for l in range(depth):
        w_small, w_gate = _arrange_w_in(w_in[l], D)
        wc, pe = _arrange_compress(nsa_w_ck[l], nsa_w_cv[l], nsa_pe_k[l], nsa_pe_v[l])
        proj, ckv, ngate = _inproj(x2, w_small, feats)
        proj3 = proj.reshape(B, S, N_PROJ_GROUPS * LANES)
        gate3 = ngate.reshape(B, S, 3 * LANES)

        o_sb = _sb_attention(proj3)

        kc4, vc4 = _nsa_compress(ckv.reshape(B, S // NSA_CMP_STRIDE, NSA_CMP_STRIDE * LANES), wc, pe)
        o_cw, nsa_selm = _nsa_select(proj3, kc4, vc4, gate3, imp_mat)
        o_nsa = _blk_attention(proj3, nsa_selm, G_NQ, G_SK, G_SV, NSA_SEL_LEN, NSA_KEY_TILE, Q_BLOCK, NSA_SLOPES,
                               gate3=gate3, addend=o_cw)

        mb_selm = _moba_select(proj3, _moba_mean(proj3))
        o_mb = _blk_attention(proj3, mb_selm, G_MQ, G_MK, G_MV, MOBA_BLOCK, MOBA_KEY_TILE, MOBA_Q_ROWS, MOBA_SLOPES)

        w_br = jnp.stack([w_br_sb[l], w_br_nsa[l], w_br_moba[l]]).astype(_BF)
        x2 = _merge(x2, o_sb.reshape(T, LANES), o_nsa.reshape(T, LANES), o_mb.reshape(T, LANES),
                    w_gate, w_br, w_out[l].astype(_BF), ln1_g[l].reshape(1, D), ln1_b[l].reshape(1, D), alpha)

        wq = peer_wq[l].reshape(D, PEER_HEADS * PEER_QDIM).astype(_BF)
        tabs = _peer_route(x2, wq, peer_k1[l].astype(_BF), peer_k2[l].astype(_BF))
        x2 = _peer_dense(x2, u_all, vt_all, l, tabs, ln2_g[l].reshape(1, D), ln2_b[l].reshape(1, D), alpha)
    return x2.reshape(B, S, D)
```

```python
import functools
import math

import numpy as np
import jax
import jax.numpy as jnp
from jax import lax
from jax.experimental import pallas as pl
from jax.experimental.pallas import tpu as pltpu

HEAD_DIM = 32
N_HEADS = 4
Q_BLOCK = 128
NSA_CMP_LEN = 32
NSA_CMP_STRIDE = 16
NSA_SEL_LEN = 64
NSA_TOP_N = 8
NSA_WINDOW = 512
MOBA_BLOCK = 256
MOBA_TOPK = 3
PEER_HEADS = 4
PEER_NKEYS = 128
PEER_TOPK = 8
PEER_QDIM = 256
N_BRANCHES = 3
LN_EPS = 1e-5

LANES = 128
VMEM_LIMIT = 48 * 1024 * 1024

_BF = jnp.bfloat16
_F32 = jnp.float32

_ALIBI = [2.0 ** (-8.0 * (i + 1) / (2 * N_HEADS)) for i in range(2 * N_HEADS)]
NSA_SLOPES = tuple(_ALIBI[0::2])
MOBA_SLOPES = tuple(_ALIBI[1::2])

G_SBQ, G_SBK, G_SBV, G_NQ, G_SK, G_SKF, G_MK, G_MKF, G_SV, G_WK, G_WV, G_MQ, G_MV = range(13)
N_PROJ_GROUPS = 13
NSA_KEY_TILE = 512
MOBA_KEY_TILE = 1024
MOBA_Q_ROWS = 256
KEY_TILE = max(NSA_KEY_TILE, MOBA_KEY_TILE)
PEER_EXPERT_TILE = 1024
F_HI, F_LO, F_ONE = 125, 126, 127
F_SPLIT = 256
MASK_BIAS = -1e30
SCORE_FLOOR = -5e29


def _dot(a, b):
    return jnp.dot(a.astype(_BF), b.astype(_BF), preferred_element_type=_F32)


def _dot_nt(a, b):
    return lax.dot_general(a.astype(_BF), b.astype(_BF), (((1,), (1,)), ((), ())),
                           preferred_element_type=_F32)


def _dot_split(a, b):
    hi = a.astype(_BF)
    lo = (a - hi.astype(_F32)).astype(_BF)
    return (jnp.dot(hi, b, preferred_element_type=_F32)
            + jnp.dot(lo, b, preferred_element_type=_F32))


def _iota(shape, dim):
    return lax.broadcasted_iota(jnp.int32, shape, dim)


def _lane_head(rows=Q_BLOCK):
    return jnp.right_shift(_iota((rows, LANES), 1), int(math.log2(HEAD_DIM)))


def _head_queries(q, lane_head):
    qf = q.astype(_F32)
    return [jnp.where(lane_head == h, qf, 0.0).astype(_BF) for h in range(N_HEADS)]


def _stack_heads(q, lane_head):
    return jnp.concatenate(_head_queries(q, lane_head), axis=0)


def _unstack_heads(acc, lane_head):
    out = jnp.zeros((Q_BLOCK, LANES), _F32)
    for h in range(N_HEADS):
        out = out + jnp.where(lane_head == h, acc[h * Q_BLOCK:(h + 1) * Q_BLOCK], 0.0)
    return out


def _first_max(x, ids, none):
    m = jnp.max(x, axis=1, keepdims=True)
    return m, jnp.min(jnp.where(x == m, ids, none), axis=1, keepdims=True)


def _params(*sem):
    return pltpu.CompilerParams(dimension_semantics=sem, vmem_limit_bytes=VMEM_LIMIT)


def _layer_norm(y, g, b):
    mu = jnp.mean(y, axis=-1, keepdims=True)
    d = y - mu
    var = jnp.mean(d * d, axis=-1, keepdims=True)
    return d * lax.rsqrt(var + LN_EPS) * g + b


def _inproj_kernel(x_ref, w_ref, feat_ref, proj_ref, ckv_ref, gate_ref):
    y = _dot(x_ref[...], w_ref[...])
    npj = N_PROJ_GROUPS * LANES
    proj_ref[...] = y[:, :npj].astype(proj_ref.dtype)
    proj_ref[:, G_SKF * LANES:(G_SKF + 1) * LANES] = feat_ref[0]
    proj_ref[:, G_MKF * LANES:(G_MKF + 1) * LANES] = feat_ref[1]
    ones = jnp.ones((y.shape[0], HEAD_DIM), proj_ref.dtype)
    proj_ref[:, G_SV * LANES + HEAD_DIM:G_SV * LANES + 2 * HEAD_DIM] = ones
    proj_ref[:, G_MV * LANES + HEAD_DIM:G_MV * LANES + 2 * HEAD_DIM] = ones
    ckv_ref[...] = y[:, npj:npj + LANES].astype(ckv_ref.dtype)
    gate_ref[...] = jax.nn.sigmoid(y[:, npj + LANES:])


def _inproj(x2, w_small, feats):
    T, D = x2.shape
    n = w_small.shape[1]
    npj = N_PROJ_GROUPS * LANES
    tm = KEY_TILE
    return pl.pallas_call(
        _inproj_kernel,
        grid=(T // tm,),
        in_specs=[pl.BlockSpec((tm, D), lambda i: (i, 0)),
                  pl.BlockSpec((D, n), lambda i: (0, 0)),
                  pl.BlockSpec((2, tm, LANES), lambda i: (0, 0, 0))],
        out_specs=[pl.BlockSpec((tm, npj), lambda i: (i, 0)),
                   pl.BlockSpec((tm, LANES), lambda i: (i, 0)),
                   pl.BlockSpec((tm, 3 * LANES), lambda i: (i, 0))],
        out_shape=[jax.ShapeDtypeStruct((T, npj), _BF),
                   jax.ShapeDtypeStruct((T, LANES), _BF),
                   jax.ShapeDtypeStruct((T, 3 * LANES), _F32)],
        compiler_params=_params("parallel"),
        name="inproj",
    )(x2, w_small, feats)


_SB_LOG_CUTOFF = -104.0
SB_BLOCKS = 2


def _sb_kernel(q_ref, k_ref, v_ref, o_ref):
    g = pl.program_id(1)
    lane_head = _lane_head()
    rows = N_HEADS * Q_BLOCK
    tri = (_iota((Q_BLOCK, Q_BLOCK), 0) > _iota((Q_BLOCK, Q_BLOCK), 1)).astype(_BF)
    q_off = jnp.bitwise_and(_iota((rows, Q_BLOCK), 0), Q_BLOCK - 1)
    diag_past = _iota((rows, Q_BLOCK), 1) < q_off
    blocks = [g * SB_BLOCKS + n for n in range(SB_BLOCKS)]
    qs = [_stack_heads(q_ref[0, n * Q_BLOCK:(n + 1) * Q_BLOCK, :], lane_head) for n in range(SB_BLOCKS)]

    def tile(q, j, carry, acc, diagonal=False, live=None):
        start = pl.multiple_of(jnp.maximum(j, 0) * Q_BLOCK, Q_BLOCK)
        kt = k_ref[0, pl.ds(start, Q_BLOCK), :]
        vt = v_ref[0, pl.ds(start, Q_BLOCK), :]
        z = _dot_nt(q, kt)
        ls = -(jnp.maximum(z, 0.0) + jnp.log(1.0 + jnp.exp(-jnp.abs(z))))
        if diagonal:
            ls = jnp.where(diag_past, ls, 0.0)
        if live is not None:
            ls = jnp.where(live, ls, 0.0)
        excl = _dot_split(ls, tri)
        w = jnp.exp(z + ls + excl + carry)
        if diagonal:
            w = jnp.where(diag_past, w, 0.0)
        if live is not None:
            w = jnp.where(live, w, 0.0)
        acc = acc + _dot(w, vt)
        carry = carry + jnp.sum(ls, axis=1, keepdims=True)
        return carry, acc

    zero_c, zero_a = jnp.zeros((rows, 1), _F32), jnp.zeros((rows, LANES), _F32)
    state = [tile(qs[n], blocks[n], zero_c, zero_a, diagonal=True) for n in range(SB_BLOCKS)]

    def furthest(t, carries):
        m = jnp.full((), -jnp.inf, _F32)
        for n in range(SB_BLOCKS):
            m = jnp.maximum(m, jnp.where(blocks[n] - 1 - t >= 0, jnp.max(carries[n]), -jnp.inf))
        return m

    def cond(st):
        t, _, cmax = st
        return jnp.logical_and(blocks[-1] - 1 - t >= 0, cmax > _SB_LOG_CUTOFF)

    def body(st):
        t, state, _ = st
        new = []
        for n in range(SB_BLOCKS):
            j = blocks[n] - 1 - t
            new.append(tile(qs[n], j, state[n][0], state[n][1], live=j >= 0))
        return t + 1, new, furthest(t + 1, [c for c, _ in new])

    _, state, _ = lax.while_loop(cond, body, (0, state, furthest(0, [c for c, _ in state])))
    for n in range(SB_BLOCKS):
        o_ref[0, n * Q_BLOCK:(n + 1) * Q_BLOCK, :] = _unstack_heads(state[n][1], lane_head).astype(o_ref.dtype)


def _sb_attention(proj3):
    B, S, _ = proj3.shape
    span = SB_BLOCKS * Q_BLOCK
    return pl.pallas_call(
        _sb_kernel,
        grid=(B, S // span),
        in_specs=[pl.BlockSpec((1, span, LANES), lambda b, i: (b, i, G_SBQ)),
                  pl.BlockSpec((1, S, LANES), lambda b, i: (b, 0, G_SBK)),
                  pl.BlockSpec((1, S, LANES), lambda b, i: (b, 0, G_SBV))],
        out_specs=pl.BlockSpec((1, span, LANES), lambda b, i: (b, i, 0)),
        out_shape=jax.ShapeDtypeStruct((B, S, LANES), _BF),
        compiler_params=_params("parallel", "arbitrary"),
        name="sb_attn",
    )(proj3, proj3, proj3)


def _nsa_compress_kernel(c_ref, w_ref, pe_ref, kc_ref, vc_ref):
    c = c_ref[0]
    a = _dot(c, w_ref[0]) + _dot(pe_ref[0], w_ref[0])[0:1]
    b = _dot(c, w_ref[1]) + _dot(pe_ref[1], w_ref[1])[0:1]
    n = a.shape[0]
    b_next = pltpu.roll(b, n - 1, 0)
    y = a + b_next
    kc_ref[0] = y[:, :LANES].astype(kc_ref.dtype)
    vc_ref[0] = y[:, LANES:].astype(vc_ref.dtype)


def _nsa_compress(ckv3, wc, pe):
    B, nchunk, width = ckv3.shape
    return pl.pallas_call(
        _nsa_compress_kernel,
        grid=(B,),
        in_specs=[pl.BlockSpec((1, nchunk, width), lambda b: (b, 0, 0)),
                  pl.BlockSpec((2, width, 2 * LANES), lambda b: (0, 0, 0)),
                  pl.BlockSpec((2, 8, width), lambda b: (0, 0, 0))],
        out_specs=[pl.BlockSpec((1, nchunk, LANES), lambda b: (b, 0, 0)),
                   pl.BlockSpec((1, nchunk, LANES), lambda b: (b, 0, 0))],
        out_shape=[jax.ShapeDtypeStruct((B, nchunk, LANES), _BF),
                   jax.ShapeDtypeStruct((B, nchunk, LANES), _BF)],
        compiler_params=_params("parallel"),
        name="nsa_compress",
    )(ckv3, wc, pe)


def _masked_exp(z, valid):
    zm = jnp.where(valid, z, MASK_BIAS)
    m = jnp.maximum(jnp.max(zm, axis=1, keepdims=True), SCORE_FLOOR)
    p = jnp.exp(zm - m)
    return p, 1.0 / jnp.maximum(jnp.sum(p, axis=1, keepdims=True), 1e-30)


def _nsa_select_kernel(q_ref, kc_ref, vc_ref, wk_ref, wv_ref, g_ref, m_ref, ocw_ref, selm_ref,
                       ocmp_ref, imp_ref, *, seq):
    i = pl.program_id(1)
    q0 = i * Q_BLOCK
    ncp = seq // NSA_CMP_STRIDE
    nsel = m_ref.shape[1]
    win = NSA_WINDOW + Q_BLOCK
    lane_head = _lane_head()
    qs = _stack_heads(q_ref[0], lane_head)
    head_rows = lambda a, h: a[h * Q_BLOCK:(h + 1) * Q_BLOCK]

    def compressed(width):
        cend = _iota((1, width), 1) * NSA_CMP_STRIDE + (NSA_CMP_LEN - 1)
        valid_c = (q0 + _iota((Q_BLOCK, width), 0)) >= cend
        rel_c = (cend - q0).astype(_F32)
        zc = _dot_nt(qs, kc_ref[0, :width, :])
        pg = jnp.zeros((Q_BLOCK, width), _F32)
        pcs = []
        for h in range(N_HEADS):
            p, inv = _masked_exp(head_rows(zc, h) + NSA_SLOPES[h] * rel_c, valid_c)
            p = p * inv
            pcs.append(p.astype(_BF))
            pg = pg + p
        ocmp_ref[...] = _unstack_heads(_dot(jnp.concatenate(pcs, axis=0), vc_ref[0, :width, :]), lane_head)
        imp_ref[...] = _dot_split(pg, m_ref[:width, :])

    group = min(ncp, 2 * LANES)
    n_valid = jnp.right_shift(q0 + Q_BLOCK - NSA_CMP_LEN, int(math.log2(NSA_CMP_STRIDE))) + 1
    groups = jnp.minimum((n_valid + group - 1) // group, ncp // group)
    for k in range(1, ncp // group + 1):
        pl.when(groups == k)(functools.partial(compressed, k * group))
    o_cmp = ocmp_ref[...]

    imp = imp_ref[...]
    blk = _iota((Q_BLOCK, nsel), 1)
    cur = jnp.right_shift(q0 + _iota((Q_BLOCK, nsel), 0), int(math.log2(NSA_SEL_LEN)))
    forced = (blk == 0) | (blk == cur) | (blk == cur - 1)
    x = jnp.where(blk > cur, -jnp.inf, jnp.where(forced, jnp.inf, imp))
    blkf = blk.astype(_F32)
    sel = jnp.zeros((Q_BLOCK, nsel), _F32)
    for _ in range(min(NSA_TOP_N, seq // NSA_SEL_LEN)):
        _, idx = _first_max(x, blkf, float(nsel))
        hit = blkf == idx
        sel = jnp.where(hit, 1.0, sel)
        x = jnp.where(hit, -jnp.inf, x)
    selm_ref[0] = sel.astype(selm_ref.dtype)

    ws = pl.multiple_of(jnp.maximum(q0 - NSA_WINDOW, 0), Q_BLOCK)
    kw = wk_ref[0, pl.ds(ws, win), :]
    vw = wv_ref[0, pl.ds(ws, win), :]
    kpos = ws + _iota((1, win), 1)
    dw = (q0 + _iota((Q_BLOCK, win), 0)) - kpos
    valid_w = (dw >= 0) & (dw < NSA_WINDOW)
    rel_w = (kpos - q0).astype(_F32)
    zw = _dot_nt(qs, kw)
    pws, invs = [], []
    for h in range(N_HEADS):
        p, inv = _masked_exp(head_rows(zw, h) + NSA_SLOPES[h] * rel_w, valid_w)
        pws.append(p.astype(_BF))
        invs.append(jnp.broadcast_to(inv, (Q_BLOCK, LANES)))
    o_win = _unstack_heads(_dot(jnp.concatenate(pws, axis=0), vw) * jnp.concatenate(invs, axis=0), lane_head)

    g = g_ref[0]
    ocw_ref[0] = g[:, :LANES] * o_cmp + g[:, 2 * LANES:] * o_win


def _nsa_select(proj3, kc4, vc4, gate3, imp_mat):
    B, S, _ = proj3.shape
    ncp = S // NSA_CMP_STRIDE
    nsel = imp_mat.shape[1]
    return pl.pallas_call(
        functools.partial(_nsa_select_kernel, seq=S),
        grid=(B, S // Q_BLOCK),
        in_specs=[pl.BlockSpec((1, Q_BLOCK, LANES), lambda b, i: (b, i, G_NQ)),
                  pl.BlockSpec((1, ncp, LANES), lambda b, i: (b, 0, 0)),
                  pl.BlockSpec((1, ncp, LANES), lambda b, i: (b, 0, 0)),
                  pl.BlockSpec((1, S, LANES), lambda b, i: (b, 0, G_WK)),
                  pl.BlockSpec((1, S, LANES), lambda b, i: (b, 0, G_WV)),
                  pl.BlockSpec((1, Q_BLOCK, 3 * LANES), lambda b, i: (b, i, 0)),
                  pl.BlockSpec((ncp, nsel), lambda b, i: (0, 0))],
        out_specs=[pl.BlockSpec((1, Q_BLOCK, LANES), lambda b, i: (b, i, 0)),
                   pl.BlockSpec((1, Q_BLOCK, nsel), lambda b, i: (b, i, 0))],
        out_shape=[jax.ShapeDtypeStruct((B, S, LANES), _F32),
                   jax.ShapeDtypeStruct((B, S, nsel), _BF)],
        scratch_shapes=[pltpu.VMEM((Q_BLOCK, LANES), _F32), pltpu.VMEM((Q_BLOCK, nsel), _F32)],
        compiler_params=_params("parallel", "arbitrary"),
        name="nsa_select",
    )(proj3, kc4, vc4, proj3, proj3, gate3, imp_mat)


def _moba_mean_kernel(k_ref, o_ref, *, nb):
    k = k_ref[0].astype(_F32)
    o_ref[0] = jnp.zeros(o_ref.shape[1:], _F32)
    o_ref[0, :nb, :] = jnp.mean(k.reshape(nb, MOBA_BLOCK, LANES), axis=1)


def _moba_mean(proj3):
    B, S, _ = proj3.shape
    nb = S // MOBA_BLOCK
    return pl.pallas_call(
        functools.partial(_moba_mean_kernel, nb=nb),
        grid=(B,),
        in_specs=[pl.BlockSpec((1, S, LANES), lambda b: (b, 0, G_MK))],
        out_specs=pl.BlockSpec((1, LANES, LANES), lambda b: (b, 0, 0)),
        out_shape=jax.ShapeDtypeStruct((B, LANES, LANES), _F32),
        compiler_params=_params("parallel"),
        name="moba_mean",
    )(proj3)


def _moba_select_kernel(q_ref, km_ref, selm_ref, *, nb):
    i = pl.program_id(1)
    q0 = i * Q_BLOCK
    sg = _dot_nt(q_ref[0], km_ref[0])
    blk = _iota((Q_BLOCK, LANES), 1)
    cur = jnp.right_shift(q0, int(math.log2(MOBA_BLOCK)))
    x = jnp.where(blk < cur, sg, -jnp.inf)
    blkf = blk.astype(_F32)
    sel = jnp.where(blk == cur, 1.0, 0.0)
    for _ in range(min(MOBA_TOPK, nb)):
        m, idx = _first_max(x, blkf, float(LANES))
        hit = blkf == idx
        sel = jnp.where(hit & (m > -jnp.inf), 1.0, sel)
        x = jnp.where(hit, -jnp.inf, x)
    selm_ref[0] = sel.astype(selm_ref.dtype)


def _moba_select(proj3, kmean):
    B, S, _ = proj3.shape
    nb = S // MOBA_BLOCK
    assert nb <= LANES
    return pl.pallas_call(
        functools.partial(_moba_select_kernel, nb=nb),
        grid=(B, S // Q_BLOCK),
        in_specs=[pl.BlockSpec((1, Q_BLOCK, LANES), lambda b, i: (b, i, G_MQ)),
                  pl.BlockSpec((1, LANES, LANES), lambda b, i: (b, 0, 0))],
        out_specs=pl.BlockSpec((1, Q_BLOCK, LANES), lambda b, i: (b, i, 0)),
        out_shape=jax.ShapeDtypeStruct((B, S, LANES), _BF),
        compiler_params=_params("parallel", "arbitrary"),
        name="moba_select",
    )(proj3, kmean)


def _blk_attn_kernel(*refs, blk_len, tile, qb, slopes, nblk, seq, gated):
    if gated:
        q_ref, kx_ref, v_ref, selm_ref, g_ref, add_ref, o_ref = refs[:7]
    else:
        q_ref, kx_ref, v_ref, selm_ref, o_ref = refs[:5]
    qx_ref, m_ref, acc_ref, za_ref, zb_ref, flag_ref, list_ref = refs[-7:]
    rows = N_HEADS * qb
    bpt = tile // blk_len
    max_tiles = seq // tile
    i = pl.program_id(1)
    q0 = i * qb
    lane_head = _lane_head(qb)
    lane = _iota((qb, LANES), 1)

    qx_ref[:, :LANES] = _stack_heads(q_ref[0], lane_head)
    m_ref[...] = jnp.full((rows, LANES), SCORE_FLOOR, _F32)
    acc_ref[...] = jnp.zeros((rows, LANES), _F32)
    lane1 = _iota((1, LANES), 1)
    fixed_lanes = [jnp.where(lane1 == F_HI, slopes[h] * F_SPLIT, jnp.where(lane1 == F_LO, slopes[h], 0.0))
                   for h in range(N_HEADS)]
    dist_lanes = [jnp.where(lane1 == F_ONE, slopes[h] * LANES, 0.0) for h in range(N_HEADS)]

    selm = selm_ref[0].astype(_F32)
    sneg = (1.0 - selm) * MASK_BIAS
    n_tiles = jnp.right_shift(q0 + qb + tile - 1, int(math.log2(tile)))

    def scores(j, z_ref):
        k0 = pl.multiple_of(j * tile, tile)
        first = j * bpt
        if nblk > LANES:
            half = jnp.right_shift(first, int(math.log2(LANES)))
            base = sneg[:, :LANES]
            for c in range(1, nblk // LANES):
                base = jnp.where(half == c, sneg[:, c * LANES:(c + 1) * LANES], base)
        else:
            base = sneg
        shift = jnp.bitwise_and(LANES - jnp.bitwise_and(first, LANES - 1), LANES - 1)
        tile_mask = pltpu.roll(base, shift, 1)
        tile_mask = jnp.where(lane >= F_HI, 0.0, tile_mask)
        dist = (j * (tile // LANES) - i * (qb // LANES)).astype(_F32)
        for h in range(N_HEADS):
            ext = tile_mask + (fixed_lanes[h] + dist * dist_lanes[h])
            qx_ref[h * qb:(h + 1) * qb, LANES:] = ext.astype(_BF)
        kx = kx_ref[0, pl.ds(k0, tile), :]
        z_ref[...] = lax.dot_general(qx_ref[...], kx, (((1,), (1,)), ((), ())),
                                     preferred_element_type=_F32)

    def absorb(z_ref, j, diagonal):
        k0 = pl.multiple_of(j * tile, tile)
        vt = v_ref[0, pl.ds(k0, tile), :]
        z = z_ref[...]
        if diagonal:
            q_off = jnp.bitwise_and(_iota((rows, tile), 0), qb - 1)
            z = jnp.where(_iota((rows, tile), 1) - q_off <= q0 - k0, z, MASK_BIAS)
        m_old = m_ref[...]
        m_new = jnp.maximum(m_old, jnp.max(z, axis=1, keepdims=True))
        p = jnp.exp(z - jnp.concatenate([m_new] * (tile // LANES), axis=1))
        a = jnp.exp(m_old - m_new)
        acc_ref[...] = a * acc_ref[...] + _dot(p, vt)
        m_ref[...] = m_new

    scores(n_tiles - 1, za_ref)

    any_q = jnp.broadcast_to(jnp.max(selm, axis=0, keepdims=True), (8, nblk))
    group = (jnp.right_shift(_iota((nblk, LANES), 0), int(math.log2(bpt))) == _iota((nblk, LANES), 1))
    tile_hits = _dot(any_q, jnp.where(group, 1.0, 0.0))
    for t in range(max_tiles):
        flag_ref[t] = (tile_hits[0, t] > 0.5).astype(jnp.int32)

    def scan(t, n):
        @pl.when(flag_ref[t] > 0)
        def _():
            list_ref[n] = t
        return n + (flag_ref[t] > 0).astype(jnp.int32)

    n_list = lax.fori_loop(0, n_tiles - 1, scan, 0)

    @pl.when(n_list == 0)
    def _():
        absorb(za_ref, n_tiles - 1, True)

    @pl.when(n_list > 0)
    def _():
        scores(list_ref[0], zb_ref)
        absorb(za_ref, n_tiles - 1, True)

    def body(kk, c):
        scores(list_ref[2 * kk + 1], za_ref)
        absorb(zb_ref, list_ref[2 * kk], False)
        scores(list_ref[2 * kk + 2], zb_ref)
        absorb(za_ref, list_ref[2 * kk + 1], False)
        return c

    n_pairs = jnp.right_shift(n_list - 1, 1)
    lax.fori_loop(0, n_pairs, body, 0)
    left = n_list - 1 - 2 * n_pairs

    @pl.when(jnp.logical_and(n_list > 0, left == 0))
    def _():
        absorb(zb_ref, list_ref[n_list - 1], False)

    @pl.when(jnp.logical_and(n_list > 0, left == 1))
    def _():
        scores(list_ref[n_list - 1], za_ref)
        absorb(zb_ref, list_ref[n_list - 2], False)
        absorb(za_ref, list_ref[n_list - 1], False)

    acc = acc_ref[...]
    total = pltpu.roll(acc, LANES - HEAD_DIM, 1)
    o = acc * (1.0 / jnp.maximum(total, 1e-30))
    out = jnp.zeros((qb, LANES), _F32)
    for h in range(N_HEADS):
        o_h = o[h * qb:(h + 1) * qb]
        if h:
            o_h = pltpu.roll(o_h, h * HEAD_DIM, 1)
        out = out + jnp.where(lane_head == h, o_h, 0.0)
    if gated:
        out = add_ref[0] + g_ref[0] * out
    o_ref[0] = out.astype(o_ref.dtype)


def _blk_attention(proj3, selm, gq, gkx, gv, blk_len, tile, qb, slopes, gate3=None, addend=None):
    B, S, _ = proj3.shape
    nblk = selm.shape[-1]
    assert S % tile == 0 and gkx % 2 == 0 and nblk % LANES == 0
    gated = gate3 is not None
    rows = N_HEADS * qb
    in_specs = [pl.BlockSpec((1, qb, LANES), lambda b, i: (b, i, gq)),
                pl.BlockSpec((1, S, 2 * LANES), lambda b, i: (b, 0, gkx // 2)),
                pl.BlockSpec((1, S, LANES), lambda b, i: (b, 0, gv)),
                pl.BlockSpec((1, qb, nblk), lambda b, i: (b, i, 0))]
    args = [proj3, proj3, proj3, selm]
    if gated:
        in_specs += [pl.BlockSpec((1, qb, LANES), lambda b, i: (b, i, 1)),
                     pl.BlockSpec((1, qb, LANES), lambda b, i: (b, i, 0))]
        args += [gate3, addend]
    return pl.pallas_call(
        functools.partial(_blk_attn_kernel, blk_len=blk_len, tile=tile, qb=qb, slopes=slopes, nblk=nblk, seq=S,
                          gated=gated),
        grid=(B, S // qb),
        in_specs=in_specs,
        out_specs=pl.BlockSpec((1, qb, LANES), lambda b, i: (b, i, 0)),
        out_shape=jax.ShapeDtypeStruct((B, S, LANES), _BF),
        scratch_shapes=[pltpu.VMEM((rows, 2 * LANES), _BF),
                        pltpu.VMEM((rows, LANES), _F32),
                        pltpu.VMEM((rows, LANES), _F32),
                        pltpu.VMEM((rows, tile), _F32),
                        pltpu.VMEM((rows, tile), _F32),
                        pltpu.SMEM((S // tile,), jnp.int32),
                        pltpu.SMEM((S // tile,), jnp.int32)],
        compiler_params=_params("parallel", "arbitrary"),
        name="blk_attn_%d" % blk_len,
    )(*args)


def _merge_kernel(x_ref, osb_ref, onsa_ref, omb_ref, wg_ref, wbr_ref, wo_ref, lg_ref, lb_ref, o_ref,
                  *, alpha):
    x = x_ref[...]
    d = x.shape[1]
    gates = jax.nn.sigmoid(_dot(x, wg_ref[...]))
    mix = (gates[:, :d] * _dot(osb_ref[...], wbr_ref[0])
           + gates[:, d:2 * d] * _dot(onsa_ref[...], wbr_ref[1])
           + gates[:, 2 * d:] * _dot(omb_ref[...], wbr_ref[2]))
    y = alpha * x + _dot(mix, wo_ref[...])
    o_ref[...] = _layer_norm(y, lg_ref[...], lb_ref[...])


def _merge(x2, o_sb, o_nsa, o_mb, w_gate, w_br, w_out, ln_g, ln_b, alpha, tm=256):
    T, D = x2.shape
    row = lambda i: (i, 0)
    fixed2 = lambda i: (0, 0)
    return pl.pallas_call(
        functools.partial(_merge_kernel, alpha=alpha),
        grid=(T // tm,),
        in_specs=[pl.BlockSpec((tm, D), row),
                  pl.BlockSpec((tm, LANES), row),
                  pl.BlockSpec((tm, LANES), row),
                  pl.BlockSpec((tm, LANES), row),
                  pl.BlockSpec((D, N_BRANCHES * D), fixed2),
                  pl.BlockSpec((N_BRANCHES, LANES, D), lambda i: (0, 0, 0)),
                  pl.BlockSpec((D, D), fixed2),
                  pl.BlockSpec((1, D), fixed2),
                  pl.BlockSpec((1, D), fixed2)],
        out_specs=pl.BlockSpec((tm, D), row),
        out_shape=jax.ShapeDtypeStruct((T, D), _F32),
        compiler_params=_params("parallel"),
        name="merge",
    )(x2, o_sb, o_nsa, o_mb, w_gate, w_br, w_out, ln_g, ln_b)


_NOT_RETRIEVED = 99.0


def _top_rows(s, k, exact):
    n = s.shape[0]
    rows = _iota(s.shape, 0).astype(_F32)
    rank = jnp.full(s.shape, _NOT_RETRIEVED, _F32)
    vals = []
    for p in range(k):
        m = jnp.max(s, axis=0, keepdims=True)
        hit = s == m
        if exact:
            hit = rows == jnp.min(jnp.where(hit, rows, float(n)), axis=0, keepdims=True)
        vals.append(m)
        rank = jnp.where(hit, float(p), rank)
        s = jnp.where(hit, -jnp.inf, s)
    return vals, rank


def _peer_route_kernel(x_ref, wq_ref, k1_ref, k2_ref, fa_ref, qb_ref, e1_ref, e2_ref):
    half = PEER_QDIM // 2
    qf = _dot(x_ref[...], wq_ref[...]).astype(_BF)
    tm = qf.shape[0]
    rows_k = _iota((PEER_TOPK, tm), 0)
    ncand = PEER_TOPK * PEER_TOPK
    pos = _iota((ncand, tm), 0).astype(_F32)

    def route(exact):
        most = jnp.zeros((1, tm), _F32)
        for h in range(PEER_HEADS):
            s1 = _dot_nt(k1_ref[h], qf[:, h * PEER_QDIM:h * PEER_QDIM + half])
            s2 = _dot_nt(k2_ref[h], qf[:, h * PEER_QDIM + half:(h + 1) * PEER_QDIM])
            v1, r1 = _top_rows(s1, PEER_TOPK, exact)
            v2, r2 = _top_rows(s2, PEER_TOPK, exact)
            for r in (r1, r2):
                most = jnp.maximum(most, jnp.sum(jnp.where(r < _NOT_RETRIEVED, 1.0, 0.0), axis=0, keepdims=True))
            v2_all = jnp.zeros((PEER_TOPK, tm), _F32)
            for qi in range(PEER_TOPK):
                v2_all = jnp.where(rows_k == qi, v2[qi], v2_all)
            cand = jnp.concatenate([v1[p] + v2_all for p in range(PEER_TOPK)], axis=0)
            c = cand
            pickf = jnp.zeros(cand.shape, _F32)
            for _ in range(PEER_TOPK):
                m = jnp.max(c, axis=0, keepdims=True)
                idx = jnp.min(jnp.where(c == m, pos, float(ncand)), axis=0, keepdims=True)
                hit = pos == idx
                pickf = jnp.where(hit, 1.0, pickf)
                c = jnp.where(hit, -jnp.inf, c)
            cmax = v1[0] + v2[0]
            z = jnp.sum(pickf * jnp.exp(cand - cmax), axis=0, keepdims=True)
            fa = jnp.zeros((PEER_NKEYS, tm), _F32)
            for p in range(PEER_TOPK):
                count = jnp.sum(pickf[p * PEER_TOPK:(p + 1) * PEER_TOPK], axis=0, keepdims=True)
                fa = jnp.where(r1 == float(p), count, fa)
            fa_ref[h] = fa
            qb_ref[h] = r2.astype(qb_ref.dtype)
            e1_ref[h] = jnp.exp(s1 - v1[0]) * (1.0 / z)
            e2_ref[h] = jnp.exp(s2 - v2[0]).astype(e2_ref.dtype)
        return jnp.max(most)

    @pl.when(route(False) > PEER_TOPK)
    def _():
        route(True)


def _peer_route(x2, wq, k1, k2, tm=256):
    T, D = x2.shape
    tab = jax.ShapeDtypeStruct((PEER_HEADS, PEER_NKEYS, T), _F32)
    tab_bf = jax.ShapeDtypeStruct((PEER_HEADS, PEER_NKEYS, T), _BF)
    tab_spec = pl.BlockSpec((PEER_HEADS, PEER_NKEYS, tm), lambda i: (0, 0, i))
    return pl.pallas_call(
        _peer_route_kernel,
        grid=(T // tm,),
        in_specs=[pl.BlockSpec((tm, D), lambda i: (i, 0)),
                  pl.BlockSpec(wq.shape, lambda i: (0, 0)),
                  pl.BlockSpec(k1.shape, lambda i: (0, 0, 0)),
                  pl.BlockSpec(k2.shape, lambda i: (0, 0, 0))],
        out_specs=[tab_spec] * 4,
        out_shape=[tab, tab_bf, tab, tab_bf],
        compiler_params=_params("parallel"),
        name="peer_route",
    )(x2, wq, k1, k2)


def _gelu_tanh(s):
    c1 = math.sqrt(2.0 / math.pi)
    k1 = jnp.asarray(c1, s.dtype)
    k2 = jnp.asarray(c1 * 0.044715, s.dtype)
    inner = s * (k1 + k2 * (s * s))
    half = jnp.asarray(0.5, s.dtype) * s
    return half + half * jnp.tanh(inner)


def _peer_dense_kernel(x_ref, xr_ref, u_ref, vt_ref, fa_ref, qb_ref, e1_ref, e2_ref, lg_ref, lb_ref, o_ref,
                       acc_ref, xb_ref, s_ref, c_ref, *, alpha, te, n_tiles, n_work):
    s_idx = pl.program_id(0)
    j_score = s_idx % n_tiles
    j_gate = jnp.maximum(s_idx - 1, 0) % n_tiles
    j_value = jnp.maximum(s_idx - 2, 0) % n_tiles

    @pl.when(s_idx == 0)
    def _():
        s_ref[...] = jnp.zeros_like(s_ref)
        c_ref[...] = jnp.zeros_like(c_ref)
        acc_ref[...] = jnp.zeros_like(acc_ref)

    @pl.when(jnp.logical_and(j_score == 0, s_idx < n_work))
    def _():
        xb_ref[...] = x_ref[...].astype(_BF)

    fresh = j_value == 0
    cur = s_idx % 2
    tm = s_ref.shape[2]
    tc = tm // 2
    for ck in range(tm // tc):
        cols = slice(ck * tc, (ck + 1) * tc)
        acc_ref[:, cols] = jnp.where(fresh, 0.0, acc_ref[:, cols]) + jnp.dot(
            vt_ref[...], c_ref[1 - cur, :, cols], preferred_element_type=_F32)
        s = s_ref[1 - cur, :, cols]
        act = _gelu_tanh(s)
        for r in range(te // PEER_NKEYS):
            a = j_gate * (te // PEER_NKEYS) + r
            gate = jnp.zeros((PEER_NKEYS, tc), _BF)
            for h in range(PEER_HEADS):
                fa = fa_ref[h, pl.ds(a, 1), cols].astype(_BF)
                e1 = e1_ref[h, pl.ds(a, 1), cols].astype(_BF)
                gate = gate + e1 * jnp.where(qb_ref[h, :, cols] < fa, e2_ref[h, :, cols], jnp.zeros((), _BF))
            c_ref[cur, r * PEER_NKEYS:(r + 1) * PEER_NKEYS, cols] = (
                gate * act[r * PEER_NKEYS:(r + 1) * PEER_NKEYS])
        s_ref[cur, :, cols] = _dot_nt(u_ref[...], xb_ref[cols, :]).astype(_BF)

    @pl.when(jnp.logical_and(j_value == n_tiles - 1, s_idx >= 2))
    def _():
        y = alpha * xr_ref[...] + acc_ref[...].T
        o_ref[...] = _layer_norm(y, lg_ref[...], lb_ref[...])


def _peer_dense(x2, u_all, vt_all, layer, tabs, ln_g, ln_b, alpha, tm=512):
    T, D = x2.shape
    n_tiles, te = vt_all.shape[1], vt_all.shape[3]
    n_tok = T // tm
    n_work = n_tok * n_tiles
    last = n_work - 1
    tok = lambda lag: (lambda s: (jnp.clip(s - lag, 0, last) // n_tiles, 0))
    exp_tile = lambda lag: (lambda s: jnp.clip(s - lag, 0, last) % n_tiles)
    tab_spec = pl.BlockSpec((PEER_HEADS, PEER_NKEYS, tm), lambda s: (0, 0, jnp.clip(s - 1, 0, last) // n_tiles))
    return pl.pallas_call(
        functools.partial(_peer_dense_kernel, alpha=alpha, te=te, n_tiles=n_tiles, n_work=n_work),
        grid=(n_work + 2,),
        in_specs=[pl.BlockSpec((tm, D), tok(0)),
                  pl.BlockSpec((tm, D), tok(2)),
                  pl.BlockSpec((None, te, D), lambda s: (layer, exp_tile(0)(s), 0)),
                  pl.BlockSpec((None, None, D, te), lambda s: (layer, exp_tile(2)(s), 0, 0)),
                  tab_spec, tab_spec, tab_spec, tab_spec,
                  pl.BlockSpec((1, D), lambda s: (0, 0)),
                  pl.BlockSpec((1, D), lambda s: (0, 0))],
        out_specs=pl.BlockSpec((tm, D), tok(2)),
        out_shape=jax.ShapeDtypeStruct((T, D), _F32),
        scratch_shapes=[pltpu.VMEM((D, tm), _F32), pltpu.VMEM((tm, D), _BF),
                        pltpu.VMEM((2, te, tm), _BF), pltpu.VMEM((2, te, tm), _BF)],
        compiler_params=_params("arbitrary"),
        name="peer_dense",
    )(x2, x2, u_all, vt_all, *tabs, ln_g, ln_b)


def _in_widths(d_model):
    w = N_HEADS * HEAD_DIM
    return (w, w, w, w) + (HEAD_DIM,) * 6 + (3 * N_HEADS, w, HEAD_DIM, HEAD_DIM, N_BRANCHES * d_model)


def _arrange_w_in(w_in, d_model):
    off = np.concatenate([[0], np.cumsum(_in_widths(d_model))])
    scale = HEAD_DIM ** -0.5
    seg = lambda n: w_in[:, off[n]:off[n + 1]]
    rep = lambda n: jnp.tile(seg(n), (1, N_HEADS))
    gate_cols = np.array([off[10] + h * 3 + c for c in range(3) for h in range(N_HEADS)
                          for _ in range(HEAD_DIM)])
    ckv = jnp.concatenate([seg(4), seg(5), jnp.zeros((w_in.shape[0], LANES - 2 * HEAD_DIM), w_in.dtype)], axis=1)
    blank = jnp.zeros((w_in.shape[0], LANES), w_in.dtype)
    once = lambda n: jnp.concatenate([seg(n), blank[:, HEAD_DIM:]], axis=1)
    groups = [seg(0) * scale, seg(1), seg(2), seg(3) * scale, rep(6), blank, rep(12), blank,
              once(7), rep(8), rep(9), seg(11) * scale, once(13), ckv, w_in[:, gate_cols]]
    return jnp.concatenate(groups, axis=1).astype(_BF), seg(14).astype(_BF)


def _arrange_compress(w_ck, w_cv, pe_k, pe_v):
    half = NSA_CMP_STRIDE
    wk = jnp.tile(w_ck.reshape(2, half, HEAD_DIM, HEAD_DIM), (1, 1, 1, N_HEADS))
    wv = jnp.tile(w_cv.reshape(2, half, HEAD_DIM, HEAD_DIM), (1, 1, 1, N_HEADS))
    w = jnp.zeros((2, half, LANES, 2 * LANES), _F32)
    w = w.at[:, :, :HEAD_DIM, :LANES].set(wk)
    w = w.at[:, :, HEAD_DIM:2 * HEAD_DIM, LANES:].set(wv)
    pe = jnp.zeros((2, half, LANES), _F32)
    pe = pe.at[:, :, :HEAD_DIM].set(pe_k.reshape(2, half, HEAD_DIM))
    pe = pe.at[:, :, HEAD_DIM:2 * HEAD_DIM].set(pe_v.reshape(2, half, HEAD_DIM))
    pe = jnp.broadcast_to(pe.reshape(2, 1, half * LANES), (2, 8, half * LANES))
    return w.reshape(2, half * LANES, 2 * LANES).astype(_BF), pe.astype(_BF)


def _key_features():
    out = np.zeros((2, KEY_TILE, LANES), np.float32)
    for n, (blk_len, tile) in enumerate(((NSA_SEL_LEN, NSA_KEY_TILE), (MOBA_BLOCK, MOBA_KEY_TILE))):
        c = np.arange(KEY_TILE) % tile
        out[n, np.arange(KEY_TILE), c // blk_len] = 1.0
        out[n, :, F_HI] = c // F_SPLIT
        out[n, :, F_LO] = c % F_SPLIT
        out[n, :, F_ONE] = 1.0
    return jnp.asarray(out, _BF)


def _importance_matrix(seq):
    ncp = seq // NSA_CMP_STRIDE
    nsel = seq // NSA_SEL_LEN
    ratio = NSA_SEL_LEN // NSA_CMP_STRIDE
    overlap = np.convolve(np.ones(ratio), np.ones(NSA_CMP_LEN // NSA_CMP_STRIDE))
    n_left = (NSA_CMP_LEN - NSA_CMP_STRIDE) // NSA_CMP_STRIDE
    m = np.zeros((ncp, -(-nsel // LANES) * LANES), np.float32)
    for j in range(nsel):
        for o, c in enumerate(overlap):
            n = ratio * j + o - n_left
            if 0 <= n < ncp - 1:
                m[n, j] += c
    return jnp.asarray(m, _BF)


def kernel(x, w_in, nsa_pe_k, nsa_pe_v, nsa_w_ck, nsa_w_cv, w_br_sb, w_br_nsa, w_br_moba, w_out, ln1_g, ln1_b, peer_wq, peer_k1, peer_k2, peer_u, peer_v, ln2_g, ln2_b):
    B, S, D = x.shape
    depth = w_in.shape[0]
    T = B * S
    assert S % MOBA_BLOCK == 0 and S >= NSA_WINDOW + Q_BLOCK
    alpha = (2.0 * depth) ** 0.25
    imp_mat = _importance_matrix(S)
    feats = _key_features()
    u_all = peer_u.astype(_BF)
    te = PEER_EXPERT_TILE
    vt_all = jnp.swapaxes(peer_v.reshape(depth, -1, te, D), 2, 3).astype(_BF)
    x2 = x.reshape(T, D)
    for l in range(depth):
        w_small, w_gate = _arrange_w_in(w_in[l], D)
        wc, pe = _arrange_compress(nsa_w_ck[l], nsa_w_cv[l], nsa_pe_k[l], nsa_pe_v[l])
        proj, ckv, ngate = _inproj(x2, w_small, feats)
        proj3 = proj.reshape(B, S, N_PROJ_GROUPS * LANES)
        gate3 = ngate.reshape(B, S, 3 * LANES)

        o_sb = _sb_attention(proj3)

        kc4, vc4 = _nsa_compress(ckv.reshape(B, S // NSA_CMP_STRIDE, NSA_CMP_STRIDE * LANES), wc, pe)
        o_cw, nsa_selm = _nsa_select(proj3, kc4, vc4, gate3, imp_mat)
        o_nsa = _blk_attention(proj3, nsa_selm, G_NQ, G_SK, G_SV, NSA_SEL_LEN, NSA_KEY_TILE, Q_BLOCK, NSA_SLOPES,
                               gate3=gate3, addend=o_cw)

        mb_selm = _moba_select(proj3, _moba_mean(proj3))
        o_mb = _blk_attention(proj3, mb_selm, G_MQ, G_MK, G_MV, MOBA_BLOCK, MOBA_KEY_TILE, MOBA_Q_ROWS, MOBA_SLOPES)

        w_br = jnp.stack([w_br_sb[l], w_br_nsa[l], w_br_moba[l]]).astype(_BF)
        x2 = _merge(x2, o_sb.reshape(T, LANES), o_nsa.reshape(T, LANES), o_mb.reshape(T, LANES),
                    w_gate, w_br, w_out[l].astype(_BF), ln1_g[l].reshape(1, D), ln1_b[l].reshape(1, D), alpha)

        wq = peer_wq[l].reshape(D, PEER_HEADS * PEER_QDIM).astype(_BF)
        tabs = _peer_route(x2, wq, peer_k1[l].astype(_BF), peer_k2[l].astype(_BF))
        x2 = _peer_dense(x2, u_all, vt_all, l, tabs, ln2_g[l].reshape(1, D), ln2_b[l].reshape(1, D), alpha)
    return x2.reshape(B, S, D)
```

```python
import functools
import math

import numpy as np
import jax
import jax.numpy as jnp
from jax import lax
from jax.experimental import pallas as pl
from jax.experimental.pallas import tpu as pltpu

HEAD_DIM = 32
N_HEADS = 4
Q_BLOCK = 128
NSA_CMP_LEN = 32
NSA_CMP_STRIDE = 16
NSA_SEL_LEN = 64
NSA_TOP_N = 8
NSA_WINDOW = 512
MOBA_BLOCK = 256
MOBA_TOPK = 3
PEER_HEADS = 4
PEER_NKEYS = 128
PEER_TOPK = 8
PEER_QDIM = 256
N_BRANCHES = 3
LN_EPS = 1e-5

LANES = 128
VMEM_LIMIT = 48 * 1024 * 1024

_BF = jnp.bfloat16
_F32 = jnp.float32

_ALIBI = [2.0 ** (-8.0 * (i + 1) / (2 * N_HEADS)) for i in range(2 * N_HEADS)]
NSA_SLOPES = tuple(_ALIBI[0::2])
MOBA_SLOPES = tuple(_ALIBI[1::2])

G_SBQ, G_SBK, G_SBV, G_NQ, G_SK, G_SKF, G_MK, G_MKF, G_SV, G_WK, G_WV, G_MQ, G_MV = range(13)
N_PROJ_GROUPS = 13
NSA_KEY_TILE = 512
MOBA_KEY_TILE = 1024
MOBA_Q_ROWS = 256
MOBA_SELECT_ROWS = 512
KEY_TILE = max(NSA_KEY_TILE, MOBA_KEY_TILE)
PEER_EXPERT_TILE = 1024
F_HI, F_LO, F_ONE = 125, 126, 127
F_SPLIT = 256
MASK_BIAS = -1e30
SCORE_FLOOR = -5e29


def _dot(a, b):
    return jnp.dot(a.astype(_BF), b.astype(_BF), preferred_element_type=_F32)


def _dot_nt(a, b):
    return lax.dot_general(a.astype(_BF), b.astype(_BF), (((1,), (1,)), ((), ())),
                           preferred_element_type=_F32)


def _dot_split(a, b):
    hi = a.astype(_BF)
    lo = (a - hi.astype(_F32)).astype(_BF)
    return (jnp.dot(hi, b, preferred_element_type=_F32)
            + jnp.dot(lo, b, preferred_element_type=_F32))


def _iota(shape, dim):
    return lax.broadcasted_iota(jnp.int32, shape, dim)


def _lane_head(rows=Q_BLOCK):
    return jnp.right_shift(_iota((rows, LANES), 1), int(math.log2(HEAD_DIM)))


def _head_queries(q, lane_head):
    qf = q.astype(_F32)
    return [jnp.where(lane_head == h, qf, 0.0).astype(_BF) for h in range(N_HEADS)]


def _stack_heads(q, lane_head):
    return jnp.concatenate(_head_queries(q, lane_head), axis=0)


def _unstack_heads(acc, lane_head):
    out = jnp.zeros((Q_BLOCK, LANES), _F32)
    for h in range(N_HEADS):
        out = out + jnp.where(lane_head == h, acc[h * Q_BLOCK:(h + 1) * Q_BLOCK], 0.0)
    return out


def _first_max(x, ids, none):
    m = jnp.max(x, axis=1, keepdims=True)
    return m, jnp.min(jnp.where(x == m, ids, none), axis=1, keepdims=True)


def _params(*sem):
    return pltpu.CompilerParams(dimension_semantics=sem, vmem_limit_bytes=VMEM_LIMIT)


def _layer_norm(y, g, b):
    mu = jnp.mean(y, axis=-1, keepdims=True)
    d = y - mu
    var = jnp.mean(d * d, axis=-1, keepdims=True)
    return d * lax.rsqrt(var + LN_EPS) * g + b


def _inproj_kernel(x_ref, w_ref, feat_ref, proj_ref, ckv_ref, gate_ref):
    y = _dot(x_ref[...], w_ref[...])
    npj = N_PROJ_GROUPS * LANES
    proj_ref[...] = y[:, :npj].astype(proj_ref.dtype)
    proj_ref[:, G_SKF * LANES:(G_SKF + 1) * LANES] = feat_ref[0]
    proj_ref[:, G_MKF * LANES:(G_MKF + 1) * LANES] = feat_ref[1]
    ones = jnp.ones((y.shape[0], HEAD_DIM), proj_ref.dtype)
    proj_ref[:, G_SV * LANES + HEAD_DIM:G_SV * LANES + 2 * HEAD_DIM] = ones
    proj_ref[:, G_MV * LANES + HEAD_DIM:G_MV * LANES + 2 * HEAD_DIM] = ones
    ckv_ref[...] = y[:, npj:npj + LANES].astype(ckv_ref.dtype)
    gate_ref[...] = jax.nn.sigmoid(y[:, npj + LANES:])


def _inproj(x2, w_small, feats):
    T, D = x2.shape
    n = w_small.shape[1]
    npj = N_PROJ_GROUPS * LANES
    tm = KEY_TILE
    return pl.pallas_call(
        _inproj_kernel,
        grid=(T // tm,),
        in_specs=[pl.BlockSpec((tm, D), lambda i: (i, 0)),
                  pl.BlockSpec((D, n), lambda i: (0, 0)),
                  pl.BlockSpec((2, tm, LANES), lambda i: (0, 0, 0))],
        out_specs=[pl.BlockSpec((tm, npj), lambda i: (i, 0)),
                   pl.BlockSpec((tm, LANES), lambda i: (i, 0)),
                   pl.BlockSpec((tm, 3 * LANES), lambda i: (i, 0))],
        out_shape=[jax.ShapeDtypeStruct((T, npj), _BF),
                   jax.ShapeDtypeStruct((T, LANES), _BF),
                   jax.ShapeDtypeStruct((T, 3 * LANES), _F32)],
        compiler_params=_params("parallel"),
        name="inproj",
    )(x2, w_small, feats)


_SB_LOG_CUTOFF = -104.0
SB_BLOCKS = 2


def _sb_kernel(q_ref, k_ref, v_ref, o_ref):
    g = pl.program_id(1)
    lane_head = _lane_head()
    rows = N_HEADS * Q_BLOCK
    tri = (_iota((Q_BLOCK, Q_BLOCK), 0) > _iota((Q_BLOCK, Q_BLOCK), 1)).astype(_BF)
    q_off = jnp.bitwise_and(_iota((rows, Q_BLOCK), 0), Q_BLOCK - 1)
    diag_past = _iota((rows, Q_BLOCK), 1) < q_off
    blocks = [g * SB_BLOCKS + n for n in range(SB_BLOCKS)]
    qs = [_stack_heads(q_ref[0, n * Q_BLOCK:(n + 1) * Q_BLOCK, :], lane_head) for n in range(SB_BLOCKS)]

    def tile(q, j, carry, acc, diagonal=False, live=None):
        start = pl.multiple_of(jnp.maximum(j, 0) * Q_BLOCK, Q_BLOCK)
        kt = k_ref[0, pl.ds(start, Q_BLOCK), :]
        vt = v_ref[0, pl.ds(start, Q_BLOCK), :]
        z = _dot_nt(q, kt)
        ls = -(jnp.maximum(z, 0.0) + jnp.log(1.0 + jnp.exp(-jnp.abs(z))))
        if diagonal:
            ls = jnp.where(diag_past, ls, 0.0)
        if live is not None:
            ls = jnp.where(live, ls, 0.0)
        excl = _dot_split(ls, tri)
        w = jnp.exp(z + ls + excl + carry)
        if diagonal:
            w = jnp.where(diag_past, w, 0.0)
        if live is not None:
            w = jnp.where(live, w, 0.0)
        acc = acc + _dot(w, vt)
        carry = carry + jnp.sum(ls, axis=1, keepdims=True)
        return carry, acc

    zero_c, zero_a = jnp.zeros((rows, 1), _F32), jnp.zeros((rows, LANES), _F32)
    state = [tile(qs[n], blocks[n], zero_c, zero_a, diagonal=True) for n in range(SB_BLOCKS)]

    def furthest(t, carries):
        m = jnp.full((), -jnp.inf, _F32)
        for n in range(SB_BLOCKS):
            m = jnp.maximum(m, jnp.where(blocks[n] - 1 - t >= 0, jnp.max(carries[n]), -jnp.inf))
        return m

    def cond(st):
        t, _, cmax = st
        return jnp.logical_and(blocks[-1] - 1 - t >= 0, cmax > _SB_LOG_CUTOFF)

    def body(st):
        t, state, _ = st
        new = []
        for n in range(SB_BLOCKS):
            j = blocks[n] - 1 - t
            new.append(tile(qs[n], j, state[n][0], state[n][1], live=j >= 0))
        return t + 1, new, furthest(t + 1, [c for c, _ in new])

    _, state, _ = lax.while_loop(cond, body, (0, state, furthest(0, [c for c, _ in state])))
    for n in range(SB_BLOCKS):
        o_ref[0, n * Q_BLOCK:(n + 1) * Q_BLOCK, :] = _unstack_heads(state[n][1], lane_head).astype(o_ref.dtype)


def _sb_attention(proj3):
    B, S, _ = proj3.shape
    span = SB_BLOCKS * Q_BLOCK
    return pl.pallas_call(
        _sb_kernel,
        grid=(B, S // span),
        in_specs=[pl.BlockSpec((1, span, LANES), lambda b, i: (b, i, G_SBQ)),
                  pl.BlockSpec((1, S, LANES), lambda b, i: (b, 0, G_SBK)),
                  pl.BlockSpec((1, S, LANES), lambda b, i: (b, 0, G_SBV))],
        out_specs=pl.BlockSpec((1, span, LANES), lambda b, i: (b, i, 0)),
        out_shape=jax.ShapeDtypeStruct((B, S, LANES), _BF),
        compiler_params=_params("parallel", "arbitrary"),
        name="sb_attn",
    )(proj3, proj3, proj3)


def _nsa_compress_kernel(c_ref, w_ref, pe_ref, kc_ref, vc_ref):
    c = c_ref[0]
    a = _dot(c, w_ref[0]) + _dot(pe_ref[0], w_ref[0])[0:1]
    b = _dot(c, w_ref[1]) + _dot(pe_ref[1], w_ref[1])[0:1]
    n = a.shape[0]
    b_next = pltpu.roll(b, n - 1, 0)
    y = a + b_next
    kc_ref[0] = y[:, :LANES].astype(kc_ref.dtype)
    vc_ref[0] = y[:, LANES:].astype(vc_ref.dtype)


def _nsa_compress(ckv3, wc, pe):
    B, nchunk, width = ckv3.shape
    return pl.pallas_call(
        _nsa_compress_kernel,
        grid=(B,),
        in_specs=[pl.BlockSpec((1, nchunk, width), lambda b: (b, 0, 0)),
                  pl.BlockSpec((2, width, 2 * LANES), lambda b: (0, 0, 0)),
                  pl.BlockSpec((2, 8, width), lambda b: (0, 0, 0))],
        out_specs=[pl.BlockSpec((1, nchunk, LANES), lambda b: (b, 0, 0)),
                   pl.BlockSpec((1, nchunk, LANES), lambda b: (b, 0, 0))],
        out_shape=[jax.ShapeDtypeStruct((B, nchunk, LANES), _BF),
                   jax.ShapeDtypeStruct((B, nchunk, LANES), _BF)],
        compiler_params=_params("parallel"),
        name="nsa_compress",
    )(ckv3, wc, pe)


def _masked_exp(z, valid):
    zm = jnp.where(valid, z, MASK_BIAS)
    m = jnp.maximum(jnp.max(zm, axis=1, keepdims=True), SCORE_FLOOR)
    p = jnp.exp(zm - m)
    return p, 1.0 / jnp.maximum(jnp.sum(p, axis=1, keepdims=True), 1e-30)


def _nsa_select_kernel(q_ref, kc_ref, vc_ref, wk_ref, wv_ref, g_ref, m_ref, ocw_ref, selm_ref,
                       ocmp_ref, imp_ref, *, seq):
    i = pl.program_id(1)
    q0 = i * Q_BLOCK
    ncp = seq // NSA_CMP_STRIDE
    nsel = m_ref.shape[1]
    win = NSA_WINDOW + Q_BLOCK
    lane_head = _lane_head()
    qs = _stack_heads(q_ref[0], lane_head)
    head_rows = lambda a, h: a[h * Q_BLOCK:(h + 1) * Q_BLOCK]

    def compressed(width):
        cend = _iota((1, width), 1) * NSA_CMP_STRIDE + (NSA_CMP_LEN - 1)
        valid_c = (q0 + _iota((Q_BLOCK, width), 0)) >= cend
        rel_c = (cend - q0).astype(_F32)
        zc = _dot_nt(qs, kc_ref[0, :width, :])
        pg = jnp.zeros((Q_BLOCK, width), _F32)
        pcs = []
        for h in range(N_HEADS):
            p, inv = _masked_exp(head_rows(zc, h) + NSA_SLOPES[h] * rel_c, valid_c)
            p = p * inv
            pcs.append(p.astype(_BF))
            pg = pg + p
        ocmp_ref[...] = _unstack_heads(_dot(jnp.concatenate(pcs, axis=0), vc_ref[0, :width, :]), lane_head)
        imp_ref[...] = _dot_split(pg, m_ref[:width, :])

    group = min(ncp, 2 * LANES)
    n_valid = jnp.right_shift(q0 + Q_BLOCK - NSA_CMP_LEN, int(math.log2(NSA_CMP_STRIDE))) + 1
    groups = jnp.minimum((n_valid + group - 1) // group, ncp // group)
    for k in range(1, ncp // group + 1):
        pl.when(groups == k)(functools.partial(compressed, k * group))
    o_cmp = ocmp_ref[...]

    imp = imp_ref[...]
    blk = _iota((Q_BLOCK, nsel), 1)
    cur = jnp.right_shift(q0 + _iota((Q_BLOCK, nsel), 0), int(math.log2(NSA_SEL_LEN)))
    forced = (blk == 0) | (blk == cur) | (blk == cur - 1)
    x = jnp.where(blk > cur, -jnp.inf, jnp.where(forced, jnp.inf, imp))
    blkf = blk.astype(_F32)
    sel = jnp.zeros((Q_BLOCK, nsel), _F32)
    for _ in range(min(NSA_TOP_N, seq // NSA_SEL_LEN)):
        _, idx = _first_max(x, blkf, float(nsel))
        hit = blkf == idx
        sel = jnp.where(hit, 1.0, sel)
        x = jnp.where(hit, -jnp.inf, x)
    selm_ref[0] = sel.astype(selm_ref.dtype)

    ws = pl.multiple_of(jnp.maximum(q0 - NSA_WINDOW, 0), Q_BLOCK)
    kw = wk_ref[0, pl.ds(ws, win), :]
    vw = wv_ref[0, pl.ds(ws, win), :]
    kpos = ws + _iota((1, win), 1)
    dw = (q0 + _iota((Q_BLOCK, win), 0)) - kpos
    valid_w = (dw >= 0) & (dw < NSA_WINDOW)
    rel_w = (kpos - q0).astype(_F32)
    zw = _dot_nt(qs, kw)
    pws, invs = [], []
    for h in range(N_HEADS):
        p, inv = _masked_exp(head_rows(zw, h) + NSA_SLOPES[h] * rel_w, valid_w)
        pws.append(p.astype(_BF))
        invs.append(jnp.broadcast_to(inv, (Q_BLOCK, LANES)))
    o_win = _unstack_heads(_dot(jnp.concatenate(pws, axis=0), vw) * jnp.concatenate(invs, axis=0), lane_head)

    g = g_ref[0]
    ocw_ref[0] = g[:, :LANES] * o_cmp + g[:, 2 * LANES:] * o_win


def _nsa_select(proj3, kc4, vc4, gate3, imp_mat):
    B, S, _ = proj3.shape
    ncp = S // NSA_CMP_STRIDE
    nsel = imp_mat.shape[1]
    return pl.pallas_call(
        functools.partial(_nsa_select_kernel, seq=S),
        grid=(B, S // Q_BLOCK),
        in_specs=[pl.BlockSpec((1, Q_BLOCK, LANES), lambda b, i: (b, i, G_NQ)),
                  pl.BlockSpec((1, ncp, LANES), lambda b, i: (b, 0, 0)),
                  pl.BlockSpec((1, ncp, LANES), lambda b, i: (b, 0, 0)),
                  pl.BlockSpec((1, S, LANES), lambda b, i: (b, 0, G_WK)),
                  pl.BlockSpec((1, S, LANES), lambda b, i: (b, 0, G_WV)),
                  pl.BlockSpec((1, Q_BLOCK, 3 * LANES), lambda b, i: (b, i, 0)),
                  pl.BlockSpec((ncp, nsel), lambda b, i: (0, 0))],
        out_specs=[pl.BlockSpec((1, Q_BLOCK, LANES), lambda b, i: (b, i, 0)),
                   pl.BlockSpec((1, Q_BLOCK, nsel), lambda b, i: (b, i, 0))],
        out_shape=[jax.ShapeDtypeStruct((B, S, LANES), _F32),
                   jax.ShapeDtypeStruct((B, S, nsel), _BF)],
        scratch_shapes=[pltpu.VMEM((Q_BLOCK, LANES), _F32), pltpu.VMEM((Q_BLOCK, nsel), _F32)],
        compiler_params=_params("parallel", "arbitrary"),
        name="nsa_select",
    )(proj3, kc4, vc4, proj3, proj3, gate3, imp_mat)


def _moba_mean_kernel(k_ref, o_ref, *, nb):
    k = k_ref[0].astype(_F32)
    o_ref[0] = jnp.zeros(o_ref.shape[1:], _F32)
    o_ref[0, :nb, :] = jnp.mean(k.reshape(nb, MOBA_BLOCK, LANES), axis=1)


def _moba_mean(proj3):
    B, S, _ = proj3.shape
    nb = S // MOBA_BLOCK
    return pl.pallas_call(
        functools.partial(_moba_mean_kernel, nb=nb),
        grid=(B,),
        in_specs=[pl.BlockSpec((1, S, LANES), lambda b: (b, 0, G_MK))],
        out_specs=pl.BlockSpec((1, LANES, LANES), lambda b: (b, 0, 0)),
        out_shape=jax.ShapeDtypeStruct((B, LANES, LANES), _F32),
        compiler_params=_params("parallel"),
        name="moba_mean",
    )(proj3)


def _moba_select_kernel(q_ref, km_ref, selm_ref, *, nb):
    rows = q_ref.shape[1]
    q0 = pl.program_id(1) * rows
    sg = _dot_nt(q_ref[0], km_ref[0])
    blk = _iota((rows, LANES), 1)
    cur = jnp.right_shift(q0 + _iota((rows, LANES), 0), int(math.log2(MOBA_BLOCK)))
    x = jnp.where(blk < cur, sg, -jnp.inf)
    blkf = blk.astype(_F32)
    sel = jnp.where(blk == cur, 1.0, 0.0)
    for _ in range(min(MOBA_TOPK, nb)):
        m, idx = _first_max(x, blkf, float(LANES))
        hit = blkf == idx
        sel = jnp.where(hit & (m > -jnp.inf), 1.0, sel)
        x = jnp.where(hit, -jnp.inf, x)
    selm_ref[0] = sel.astype(selm_ref.dtype)


def _moba_select(proj3, kmean):
    B, S, _ = proj3.shape
    nb = S // MOBA_BLOCK
    rows = MOBA_SELECT_ROWS
    assert nb <= LANES and S % rows == 0
    return pl.pallas_call(
        functools.partial(_moba_select_kernel, nb=nb),
        grid=(B, S // rows),
        in_specs=[pl.BlockSpec((1, rows, LANES), lambda b, i: (b, i, G_MQ)),
                  pl.BlockSpec((1, LANES, LANES), lambda b, i: (b, 0, 0))],
        out_specs=pl.BlockSpec((1, rows, LANES), lambda b, i: (b, i, 0)),
        out_shape=jax.ShapeDtypeStruct((B, S, LANES), _BF),
        compiler_params=_params("parallel", "arbitrary"),
        name="moba_select",
    )(proj3, kmean)


def _blk_attn_kernel(*refs, blk_len, tile, qb, slopes, nblk, seq, gated):
    if gated:
        q_ref, kx_ref, v_ref, selm_ref, g_ref, add_ref, o_ref = refs[:7]
    else:
        q_ref, kx_ref, v_ref, selm_ref, o_ref = refs[:5]
    qx_ref, m_ref, acc_ref, za_ref, zb_ref, flag_ref, list_ref = refs[-7:]
    rows = N_HEADS * qb
    bpt = tile // blk_len
    max_tiles = seq // tile
    i = pl.program_id(1)
    q0 = i * qb
    lane_head = _lane_head(qb)
    lane = _iota((qb, LANES), 1)

    qx_ref[:, :LANES] = _stack_heads(q_ref[0], lane_head)
    m_ref[...] = jnp.full((rows, LANES), SCORE_FLOOR, _F32)
    acc_ref[...] = jnp.zeros((rows, LANES), _F32)
    lane1 = _iota((1, LANES), 1)
    fixed_lanes = [jnp.where(lane1 == F_HI, slopes[h] * F_SPLIT, jnp.where(lane1 == F_LO, slopes[h], 0.0))
                   for h in range(N_HEADS)]
    dist_lanes = [jnp.where(lane1 == F_ONE, slopes[h] * LANES, 0.0) for h in range(N_HEADS)]

    selm = selm_ref[0].astype(_F32)
    sneg = (1.0 - selm) * MASK_BIAS
    n_tiles = jnp.right_shift(q0 + qb + tile - 1, int(math.log2(tile)))

    def scores(j, z_ref):
        k0 = pl.multiple_of(j * tile, tile)
        first = j * bpt
        if nblk > LANES:
            half = jnp.right_shift(first, int(math.log2(LANES)))
            base = sneg[:, :LANES]
            for c in range(1, nblk // LANES):
                base = jnp.where(half == c, sneg[:, c * LANES:(c + 1) * LANES], base)
        else:
            base = sneg
        shift = jnp.bitwise_and(LANES - jnp.bitwise_and(first, LANES - 1), LANES - 1)
        tile_mask = pltpu.roll(base, shift, 1)
        tile_mask = jnp.where(lane >= F_HI, 0.0, tile_mask)
        dist = (j * (tile // LANES) - i * (qb // LANES)).astype(_F32)
        for h in range(N_HEADS):
            ext = tile_mask + (fixed_lanes[h] + dist * dist_lanes[h])
            qx_ref[h * qb:(h + 1) * qb, LANES:] = ext.astype(_BF)
        kx = kx_ref[0, pl.ds(k0, tile), :]
        z_ref[...] = lax.dot_general(qx_ref[...], kx, (((1,), (1,)), ((), ())),
                                     preferred_element_type=_F32)

    def absorb(z_ref, j, diagonal):
        k0 = pl.multiple_of(j * tile, tile)
        vt = v_ref[0, pl.ds(k0, tile), :]
        z = z_ref[...]
        if diagonal:
            q_off = jnp.bitwise_and(_iota((rows, tile), 0), qb - 1)
            z = jnp.where(_iota((rows, tile), 1) - q_off <= q0 - k0, z, MASK_BIAS)
        m_old = m_ref[...]
        m_new = jnp.maximum(m_old, jnp.max(z, axis=1, keepdims=True))
        p = jnp.exp(z - jnp.concatenate([m_new] * (tile // LANES), axis=1))
        a = jnp.exp(m_old - m_new)
        acc_ref[...] = a * acc_ref[...] + _dot(p, vt)
        m_ref[...] = m_new

    scores(n_tiles - 1, za_ref)

    any_q = jnp.broadcast_to(jnp.max(selm, axis=0, keepdims=True), (8, nblk))
    group = (jnp.right_shift(_iota((nblk, LANES), 0), int(math.log2(bpt))) == _iota((nblk, LANES), 1))
    tile_hits = _dot(any_q, jnp.where(group, 1.0, 0.0))
    for t in range(max_tiles):
        flag_ref[t] = (tile_hits[0, t] > 0.5).astype(jnp.int32)

    def scan(t, n):
        @pl.when(flag_ref[t] > 0)
        def _():
            list_ref[n] = t
        return n + (flag_ref[t] > 0).astype(jnp.int32)

    n_list = lax.fori_loop(0, n_tiles - 1, scan, 0)

    @pl.when(n_list == 0)
    def _():
        absorb(za_ref, n_tiles - 1, True)

    @pl.when(n_list > 0)
    def _():
        scores(list_ref[0], zb_ref)
        absorb(za_ref, n_tiles - 1, True)

    def body(kk, c):
        scores(list_ref[2 * kk + 1], za_ref)
        absorb(zb_ref, list_ref[2 * kk], False)
        scores(list_ref[2 * kk + 2], zb_ref)
        absorb(za_ref, list_ref[2 * kk + 1], False)
        return c

    n_pairs = jnp.right_shift(n_list - 1, 1)
    lax.fori_loop(0, n_pairs, body, 0)
    left = n_list - 1 - 2 * n_pairs

    @pl.when(jnp.logical_and(n_list > 0, left == 0))
    def _():
        absorb(zb_ref, list_ref[n_list - 1], False)

    @pl.when(jnp.logical_and(n_list > 0, left == 1))
    def _():
        scores(list_ref[n_list - 1], za_ref)
        absorb(zb_ref, list_ref[n_list - 2], False)
        absorb(za_ref, list_ref[n_list - 1], False)

    acc = acc_ref[...]
    total = pltpu.roll(acc, LANES - HEAD_DIM, 1)
    o = acc * (1.0 / jnp.maximum(total, 1e-30))
    out = jnp.zeros((qb, LANES), _F32)
    for h in range(N_HEADS):
        o_h = o[h * qb:(h + 1) * qb]
        if h:
            o_h = pltpu.roll(o_h, h * HEAD_DIM, 1)
        out = out + jnp.where(lane_head == h, o_h, 0.0)
    if gated:
        out = add_ref[0] + g_ref[0] * out
    o_ref[0] = out.astype(o_ref.dtype)


def _blk_attention(proj3, selm, gq, gkx, gv, blk_len, tile, qb, slopes, gate3=None, addend=None):
    B, S, _ = proj3.shape
    nblk = selm.shape[-1]
    assert S % tile == 0 and gkx % 2 == 0 and nblk % LANES == 0
    gated = gate3 is not None
    rows = N_HEADS * qb
    in_specs = [pl.BlockSpec((1, qb, LANES), lambda b, i: (b, i, gq)),
                pl.BlockSpec((1, S, 2 * LANES), lambda b, i: (b, 0, gkx // 2)),
                pl.BlockSpec((1, S, LANES), lambda b, i: (b, 0, gv)),
                pl.BlockSpec((1, qb, nblk), lambda b, i: (b, i, 0))]
    args = [proj3, proj3, proj3, selm]
    if gated:
        in_specs += [pl.BlockSpec((1, qb, LANES), lambda b, i: (b, i, 1)),
                     pl.BlockSpec((1, qb, LANES), lambda b, i: (b, i, 0))]
        args += [gate3, addend]
    return pl.pallas_call(
        functools.partial(_blk_attn_kernel, blk_len=blk_len, tile=tile, qb=qb, slopes=slopes, nblk=nblk, seq=S,
                          gated=gated),
        grid=(B, S // qb),
        in_specs=in_specs,
        out_specs=pl.BlockSpec((1, qb, LANES), lambda b, i: (b, i, 0)),
        out_shape=jax.ShapeDtypeStruct((B, S, LANES), _BF),
        scratch_shapes=[pltpu.VMEM((rows, 2 * LANES), _BF),
                        pltpu.VMEM((rows, LANES), _F32),
                        pltpu.VMEM((rows, LANES), _F32),
                        pltpu.VMEM((rows, tile), _F32),
                        pltpu.VMEM((rows, tile), _F32),
                        pltpu.SMEM((S // tile,), jnp.int32),
                        pltpu.SMEM((S // tile,), jnp.int32)],
        compiler_params=_params("parallel", "arbitrary"),
        name="blk_attn_%d" % blk_len,
    )(*args)


def _merge_kernel(x_ref, osb_ref, onsa_ref, omb_ref, wg_ref, wbr_ref, wo_ref, lg_ref, lb_ref, o_ref,
                  *, alpha):
    x = x_ref[...]
    d = x.shape[1]
    gates = jax.nn.sigmoid(_dot(x, wg_ref[...]))
    mix = (gates[:, :d] * _dot(osb_ref[...], wbr_ref[0])
           + gates[:, d:2 * d] * _dot(onsa_ref[...], wbr_ref[1])
           + gates[:, 2 * d:] * _dot(omb_ref[...], wbr_ref[2]))
    y = alpha * x + _dot(mix, wo_ref[...])
    o_ref[...] = _layer_norm(y, lg_ref[...], lb_ref[...])


def _merge(x2, o_sb, o_nsa, o_mb, w_gate, w_br, w_out, ln_g, ln_b, alpha, tm=256):
    T, D = x2.shape
    row = lambda i: (i, 0)
    fixed2 = lambda i: (0, 0)
    return pl.pallas_call(
        functools.partial(_merge_kernel, alpha=alpha),
        grid=(T // tm,),
        in_specs=[pl.BlockSpec((tm, D), row),
                  pl.BlockSpec((tm, LANES), row),
                  pl.BlockSpec((tm, LANES), row),
                  pl.BlockSpec((tm, LANES), row),
                  pl.BlockSpec((D, N_BRANCHES * D), fixed2),
                  pl.BlockSpec((N_BRANCHES, LANES, D), lambda i: (0, 0, 0)),
                  pl.BlockSpec((D, D), fixed2),
                  pl.BlockSpec((1, D), fixed2),
                  pl.BlockSpec((1, D), fixed2)],
        out_specs=pl.BlockSpec((tm, D), row),
        out_shape=jax.ShapeDtypeStruct((T, D), _F32),
        compiler_params=_params("parallel"),
        name="merge",
    )(x2, o_sb, o_nsa, o_mb, w_gate, w_br, w_out, ln_g, ln_b)


_NOT_RETRIEVED = 99.0


def _top_rows(s, k, exact):
    n = s.shape[0]
    rows = _iota(s.shape, 0).astype(_F32)
    rank = jnp.full(s.shape, _NOT_RETRIEVED, _F32)
    vals = []
    for p in range(k):
        m = jnp.max(s, axis=0, keepdims=True)
        hit = s == m
        if exact:
            hit = rows == jnp.min(jnp.where(hit, rows, float(n)), axis=0, keepdims=True)
        vals.append(m)
        rank = jnp.where(hit, float(p), rank)
        s = jnp.where(hit, -jnp.inf, s)
    return vals, rank


def _peer_route_kernel(x_ref, wq_ref, k1_ref, k2_ref, fa_ref, qb_ref, e1_ref, e2_ref):
    half = PEER_QDIM // 2
    qf = _dot(x_ref[...], wq_ref[...]).astype(_BF)
    tm = qf.shape[0]
    rows_k = _iota((PEER_TOPK, tm), 0)
    ncand = PEER_TOPK * PEER_TOPK
    pos = _iota((ncand, tm), 0).astype(_F32)

    def route(exact):
        most = jnp.zeros((1, tm), _F32)
        for h in range(PEER_HEADS):
            s1 = _dot_nt(k1_ref[h], qf[:, h * PEER_QDIM:h * PEER_QDIM + half])
            s2 = _dot_nt(k2_ref[h], qf[:, h * PEER_QDIM + half:(h + 1) * PEER_QDIM])
            v1, r1 = _top_rows(s1, PEER_TOPK, exact)
            v2, r2 = _top_rows(s2, PEER_TOPK, exact)
            for r in (r1, r2):
                most = jnp.maximum(most, jnp.sum(jnp.where(r < _NOT_RETRIEVED, 1.0, 0.0), axis=0, keepdims=True))
            v2_all = jnp.zeros((PEER_TOPK, tm), _F32)
            for qi in range(PEER_TOPK):
                v2_all = jnp.where(rows_k == qi, v2[qi], v2_all)
            cand = jnp.concatenate([v1[p] + v2_all for p in range(PEER_TOPK)], axis=0)
            c = cand
            pickf = jnp.zeros(cand.shape, _F32)
            for _ in range(PEER_TOPK):
                m = jnp.max(c, axis=0, keepdims=True)
                idx = jnp.min(jnp.where(c == m, pos, float(ncand)), axis=0, keepdims=True)
                hit = pos == idx
                pickf = jnp.where(hit, 1.0, pickf)
                c = jnp.where(hit, -jnp.inf, c)
            cmax = v1[0] + v2[0]
            z = jnp.sum(pickf * jnp.exp(cand - cmax), axis=0, keepdims=True)
            fa = jnp.zeros((PEER_NKEYS, tm), _F32)
            for p in range(PEER_TOPK):
                count = jnp.sum(pickf[p * PEER_TOPK:(p + 1) * PEER_TOPK], axis=0, keepdims=True)
                fa = jnp.where(r1 == float(p), count, fa)
            fa_ref[h] = fa
            qb_ref[h] = r2.astype(qb_ref.dtype)
            e1_ref[h] = jnp.exp(s1 - v1[0]) * (1.0 / z)
            e2_ref[h] = jnp.exp(s2 - v2[0]).astype(e2_ref.dtype)
        return jnp.max(most)

    @pl.when(route(False) > PEER_TOPK)
    def _():
        route(True)


def _peer_route(x2, wq, k1, k2, tm=256):
    T, D = x2.shape
    tab = jax.ShapeDtypeStruct((PEER_HEADS, PEER_NKEYS, T), _F32)
    tab_bf = jax.ShapeDtypeStruct((PEER_HEADS, PEER_NKEYS, T), _BF)
    tab_spec = pl.BlockSpec((PEER_HEADS, PEER_NKEYS, tm), lambda i: (0, 0, i))
    return pl.pallas_call(
        _peer_route_kernel,
        grid=(T // tm,),
        in_specs=[pl.BlockSpec((tm, D), lambda i: (i, 0)),
                  pl.BlockSpec(wq.shape, lambda i: (0, 0)),
                  pl.BlockSpec(k1.shape, lambda i: (0, 0, 0)),
                  pl.BlockSpec(k2.shape, lambda i: (0, 0, 0))],
        out_specs=[tab_spec] * 4,
        out_shape=[tab, tab_bf, tab, tab_bf],
        compiler_params=_params("parallel"),
        name="peer_route",
    )(x2, wq, k1, k2)


def _gelu_tanh(s):
    c1 = math.sqrt(2.0 / math.pi)
    k1 = jnp.asarray(c1, s.dtype)
    k2 = jnp.asarray(c1 * 0.044715, s.dtype)
    inner = s * (k1 + k2 * (s * s))
    half = jnp.asarray(0.5, s.dtype) * s
    return half + half * jnp.tanh(inner)


def _peer_dense_kernel(x_ref, xr_ref, u_ref, vt_ref, fa_ref, qb_ref, e1_ref, e2_ref, lg_ref, lb_ref, o_ref,
                       acc_ref, xb_ref, s_ref, c_ref, *, alpha, te, n_tiles, n_work):
    s_idx = pl.program_id(0)
    j_score = s_idx % n_tiles
    j_gate = jnp.maximum(s_idx - 1, 0) % n_tiles
    j_value = jnp.maximum(s_idx - 2, 0) % n_tiles

    @pl.when(s_idx == 0)
    def _():
        s_ref[...] = jnp.zeros_like(s_ref)
        c_ref[...] = jnp.zeros_like(c_ref)
        acc_ref[...] = jnp.zeros_like(acc_ref)

    @pl.when(jnp.logical_and(j_score == 0, s_idx < n_work))
    def _():
        xb_ref[...] = x_ref[...].astype(_BF)

    fresh = j_value == 0
    cur = s_idx % 2
    tm = s_ref.shape[2]
    tc = tm // 2
    for ck in range(tm // tc):
        cols = slice(ck * tc, (ck + 1) * tc)
        acc_ref[:, cols] = jnp.where(fresh, 0.0, acc_ref[:, cols]) + jnp.dot(
            vt_ref[...], c_ref[1 - cur, :, cols], preferred_element_type=_F32)
        s = s_ref[1 - cur, :, cols]
        act = _gelu_tanh(s)
        for r in range(te // PEER_NKEYS):
            a = j_gate * (te // PEER_NKEYS) + r
            gate = jnp.zeros((PEER_NKEYS, tc), _BF)
            for h in range(PEER_HEADS):
                fa = fa_ref[h, pl.ds(a, 1), cols].astype(_BF)
                e1 = e1_ref[h, pl.ds(a, 1), cols].astype(_BF)
                gate = gate + e1 * jnp.where(qb_ref[h, :, cols] < fa, e2_ref[h, :, cols], jnp.zeros((), _BF))
            c_ref[cur, r * PEER_NKEYS:(r + 1) * PEER_NKEYS, cols] = (
                gate * act[r * PEER_NKEYS:(r + 1) * PEER_NKEYS])
        s_ref[cur, :, cols] = _dot_nt(u_ref[...], xb_ref[cols, :]).astype(_BF)

    @pl.when(jnp.logical_and(j_value == n_tiles - 1, s_idx >= 2))
    def _():
        y = alpha * xr_ref[...] + acc_ref[...].T
        o_ref[...] = _layer_norm(y, lg_ref[...], lb_ref[...])


def _peer_dense(x2, u_all, vt_all, layer, tabs, ln_g, ln_b, alpha, tm=512):
    T, D = x2.shape
    n_tiles, te = vt_all.shape[1], vt_all.shape[3]
    n_tok = T // tm
    n_work = n_tok * n_tiles
    last = n_work - 1
    tok = lambda lag: (lambda s: (jnp.clip(s - lag, 0, last) // n_tiles, 0))
    exp_tile = lambda lag: (lambda s: jnp.clip(s - lag, 0, last) % n_tiles)
    tab_spec = pl.BlockSpec((PEER_HEADS, PEER_NKEYS, tm), lambda s: (0, 0, jnp.clip(s - 1, 0, last) // n_tiles))
    return pl.pallas_call(
        functools.partial(_peer_dense_kernel, alpha=alpha, te=te, n_tiles=n_tiles, n_work=n_work),
        grid=(n_work + 2,),
        in_specs=[pl.BlockSpec((tm, D), tok(0)),
                  pl.BlockSpec((tm, D), tok(2)),
                  pl.BlockSpec((None, te, D), lambda s: (layer, exp_tile(0)(s), 0)),
                  pl.BlockSpec((None, None, D, te), lambda s: (layer, exp_tile(2)(s), 0, 0)),
                  tab_spec, tab_spec, tab_spec, tab_spec,
                  pl.BlockSpec((1, D), lambda s: (0, 0)),
                  pl.BlockSpec((1, D), lambda s: (0, 0))],
        out_specs=pl.BlockSpec((tm, D), tok(2)),
        out_shape=jax.ShapeDtypeStruct((T, D), _F32),
        scratch_shapes=[pltpu.VMEM((D, tm), _F32), pltpu.VMEM((tm, D), _BF),
                        pltpu.VMEM((2, te, tm), _BF), pltpu.VMEM((2, te, tm), _BF)],
        compiler_params=_params("arbitrary"),
        name="peer_dense",
    )(x2, x2, u_all, vt_all, *tabs, ln_g, ln_b)


def _in_widths(d_model):
    w = N_HEADS * HEAD_DIM
    return (w, w, w, w) + (HEAD_DIM,) * 6 + (3 * N_HEADS, w, HEAD_DIM, HEAD_DIM, N_BRANCHES * d_model)


def _arrange_w_in(w_in, d_model):
    off = np.concatenate([[0], np.cumsum(_in_widths(d_model))])
    scale = HEAD_DIM ** -0.5
    seg = lambda n: w_in[:, off[n]:off[n + 1]]
    rep = lambda n: jnp.tile(seg(n), (1, N_HEADS))
    gate_cols = np.array([off[10] + h * 3 + c for c in range(3) for h in range(N_HEADS)
                          for _ in range(HEAD_DIM)])
    ckv = jnp.concatenate([seg(4), seg(5), jnp.zeros((w_in.shape[0], LANES - 2 * HEAD_DIM), w_in.dtype)], axis=1)
    blank = jnp.zeros((w_in.shape[0], LANES), w_in.dtype)
    once = lambda n: jnp.concatenate([seg(n), blank[:, HEAD_DIM:]], axis=1)
    groups = [seg(0) * scale, seg(1), seg(2), seg(3) * scale, rep(6), blank, rep(12), blank,
              once(7), rep(8), rep(9), seg(11) * scale, once(13), ckv, w_in[:, gate_cols]]
    return jnp.concatenate(groups, axis=1).astype(_BF), seg(14).astype(_BF)


def _arrange_compress(w_ck, w_cv, pe_k, pe_v):
    half = NSA_CMP_STRIDE
    wk = jnp.tile(w_ck.reshape(2, half, HEAD_DIM, HEAD_DIM), (1, 1, 1, N_HEADS))
    wv = jnp.tile(w_cv.reshape(2, half, HEAD_DIM, HEAD_DIM), (1, 1, 1, N_HEADS))
    w = jnp.zeros((2, half, LANES, 2 * LANES), _F32)
    w = w.at[:, :, :HEAD_DIM, :LANES].set(wk)
    w = w.at[:, :, HEAD_DIM:2 * HEAD_DIM, LANES:].set(wv)
    pe = jnp.zeros((2, half, LANES), _F32)
    pe = pe.at[:, :, :HEAD_DIM].set(pe_k.reshape(2, half, HEAD_DIM))
    pe = pe.at[:, :, HEAD_DIM:2 * HEAD_DIM].set(pe_v.reshape(2, half, HEAD_DIM))
    pe = jnp.broadcast_to(pe.reshape(2, 1, half * LANES), (2, 8, half * LANES))
    return w.reshape(2, half * LANES, 2 * LANES).astype(_BF), pe.astype(_BF)


def _key_features():
    out = np.zeros((2, KEY_TILE, LANES), np.float32)
    for n, (blk_len, tile) in enumerate(((NSA_SEL_LEN, NSA_KEY_TILE), (MOBA_BLOCK, MOBA_KEY_TILE))):
        c = np.arange(KEY_TILE) % tile
        out[n, np.arange(KEY_TILE), c // blk_len] = 1.0
        out[n, :, F_HI] = c // F_SPLIT
        out[n, :, F_LO] = c % F_SPLIT
        out[n, :, F_ONE] = 1.0
    return jnp.asarray(out, _BF)


def _importance_matrix(seq):
    ncp = seq // NSA_CMP_STRIDE
    nsel = seq // NSA_SEL_LEN
    ratio = NSA_SEL_LEN // NSA_CMP_STRIDE
    overlap = np.convolve(np.ones(ratio), np.ones(NSA_CMP_LEN // NSA_CMP_STRIDE))
    n_left = (NSA_CMP_LEN - NSA_CMP_STRIDE) // NSA_CMP_STRIDE
    m = np.zeros((ncp, -(-nsel // LANES) * LANES), np.float32)
    for j in range(nsel):
        for o, c in enumerate(overlap):
            n = ratio * j + o - n_left
            if 0 <= n < ncp - 1:
                m[n, j] += c
    return jnp.asarray(m, _BF)


def kernel(x, w_in, nsa_pe_k, nsa_pe_v, nsa_w_ck, nsa_w_cv, w_br_sb, w_br_nsa, w_br_moba, w_out, ln1_g, ln1_b, peer_wq, peer_k1, peer_k2, peer_u, peer_v, ln2_g, ln2_b):
    B, S, D = x.shape
    depth = w_in.shape[0]
    T = B * S
    assert S % MOBA_BLOCK == 0 and S >= NSA_WINDOW + Q_BLOCK
    alpha = (2.0 * depth) ** 0.25
    imp_mat = _importance_matrix(S)
    feats = _key_features()
    u_all = peer_u.astype(_BF)
    te = PEER_EXPERT_TILE
    vt_all = jnp.swapaxes(peer_v.reshape(depth, -1, te, D), 2, 3).astype(_BF)
    x2 = x.reshape(T, D)
    for l in range(depth):
        w_small, w_gate = _arrange_w_in(w_in[l], D)
        wc, pe = _arrange_compress(nsa_w_ck[l], nsa_w_cv[l], nsa_pe_k[l], nsa_pe_v[l])
        proj, ckv, ngate = _inproj(x2, w_small, feats)
        proj3 = proj.reshape(B, S, N_PROJ_GROUPS * LANES)
        gate3 = ngate.reshape(B, S, 3 * LANES)

        o_sb = _sb_attention(proj3)

        kc4, vc4 = _nsa_compress(ckv.reshape(B, S // NSA_CMP_STRIDE, NSA_CMP_STRIDE * LANES), wc, pe)
        o_cw, nsa_selm = _nsa_select(proj3, kc4, vc4, gate3, imp_mat)
        o_nsa = _blk_attention(proj3, nsa_selm, G_NQ, G_SK, G_SV, NSA_SEL_LEN, NSA_KEY_TILE, Q_BLOCK, NSA_SLOPES,
                               gate3=gate3, addend=o_cw)

        mb_selm = _moba_select(proj3, _moba_mean(proj3))
        o_mb = _blk_attention(proj3, mb_selm, G_MQ, G_MK, G_MV, MOBA_BLOCK, MOBA_KEY_TILE, MOBA_Q_ROWS, MOBA_SLOPES)

        w_br = jnp.stack([w_br_sb[l], w_br_nsa[l], w_br_moba[l]]).astype(_BF)
        x2 = _merge(x2, o_sb.reshape(T, LANES), o_nsa.reshape(T, LANES), o_mb.reshape(T, LANES),
                    w_gate, w_br, w_out[l].astype(_BF), ln1_g[l].reshape(1, D), ln1_b[l].reshape(1, D), alpha)

        wq = peer_wq[l].reshape(D, PEER_HEADS * PEER_QDIM).astype(_BF)
        tabs = _peer_route(x2, wq, peer_k1[l].astype(_BF), peer_k2[l].astype(_BF))
        x2 = _peer_dense(x2, u_all, vt_all, l, tabs, ln2_g[l].reshape(1, D), ln2_b[l].reshape(1, D), alpha)
    return x2.reshape(B, S, D)
```

```python
import functools
import math

import numpy as np
import jax
import jax.numpy as jnp
from jax import lax
from jax.experimental import pallas as pl
from jax.experimental.pallas import tpu as pltpu

HEAD_DIM = 32
N_HEADS = 4
Q_BLOCK = 128
NSA_CMP_LEN = 32
NSA_CMP_STRIDE = 16
NSA_SEL_LEN = 64
NSA_TOP_N = 8
NSA_WINDOW = 512
MOBA_BLOCK = 256
MOBA_TOPK = 3
PEER_HEADS = 4
PEER_NKEYS = 128
PEER_TOPK = 8
PEER_QDIM = 256
N_BRANCHES = 3
LN_EPS = 1e-5

LANES = 128
VMEM_LIMIT = 48 * 1024 * 1024

_BF = jnp.bfloat16
_F32 = jnp.float32

_ALIBI = [2.0 ** (-8.0 * (i + 1) / (2 * N_HEADS)) for i in range(2 * N_HEADS)]
NSA_SLOPES = tuple(_ALIBI[0::2])
MOBA_SLOPES = tuple(_ALIBI[1::2])

G_SBQ, G_SBK, G_SBV, G_NQ, G_SK, G_SKF, G_MK, G_MKF, G_SV, G_WK, G_WV, G_MQ, G_MV = range(13)
N_PROJ_GROUPS = 13
NSA_KEY_TILE = 512
MOBA_KEY_TILE = 1024
MOBA_Q_ROWS = 256
MOBA_SELECT_ROWS = 512
NSA_SELECT_ROWS = 256
KEY_TILE = max(NSA_KEY_TILE, MOBA_KEY_TILE)
PEER_EXPERT_TILE = 1024
F_HI, F_LO, F_ONE = 125, 126, 127
F_SPLIT = 256
MASK_BIAS = -1e30
SCORE_FLOOR = -5e29


def _dot(a, b):
    return jnp.dot(a.astype(_BF), b.astype(_BF), preferred_element_type=_F32)


def _dot_nt(a, b):
    return lax.dot_general(a.astype(_BF), b.astype(_BF), (((1,), (1,)), ((), ())),
                           preferred_element_type=_F32)


def _dot_split(a, b):
    hi = a.astype(_BF)
    lo = (a - hi.astype(_F32)).astype(_BF)
    return (jnp.dot(hi, b, preferred_element_type=_F32)
            + jnp.dot(lo, b, preferred_element_type=_F32))


def _iota(shape, dim):
    return lax.broadcasted_iota(jnp.int32, shape, dim)


def _lane_head(rows=Q_BLOCK):
    return jnp.right_shift(_iota((rows, LANES), 1), int(math.log2(HEAD_DIM)))


def _head_queries(q, lane_head):
    qf = q.astype(_F32)
    return [jnp.where(lane_head == h, qf, 0.0).astype(_BF) for h in range(N_HEADS)]


def _stack_heads(q, lane_head):
    return jnp.concatenate(_head_queries(q, lane_head), axis=0)


def _unstack_heads(acc, lane_head):
    rows = lane_head.shape[0]
    out = jnp.zeros((rows, LANES), _F32)
    for h in range(N_HEADS):
        out = out + jnp.where(lane_head == h, acc[h * rows:(h + 1) * rows], 0.0)
    return out


def _first_max(x, ids, none):
    m = jnp.max(x, axis=1, keepdims=True)
    return m, jnp.min(jnp.where(x == m, ids, none), axis=1, keepdims=True)


def _params(*sem):
    return pltpu.CompilerParams(dimension_semantics=sem, vmem_limit_bytes=VMEM_LIMIT)


def _layer_norm(y, g, b):
    mu = jnp.mean(y, axis=-1, keepdims=True)
    d = y - mu
    var = jnp.mean(d * d, axis=-1, keepdims=True)
    return d * lax.rsqrt(var + LN_EPS) * g + b


def _inproj_kernel(x_ref, w_ref, feat_ref, proj_ref, ckv_ref, gate_ref):
    y = _dot(x_ref[...], w_ref[...])
    npj = N_PROJ_GROUPS * LANES
    proj_ref[...] = y[:, :npj].astype(proj_ref.dtype)
    proj_ref[:, G_SKF * LANES:(G_SKF + 1) * LANES] = feat_ref[0]
    proj_ref[:, G_MKF * LANES:(G_MKF + 1) * LANES] = feat_ref[1]
    ones = jnp.ones((y.shape[0], HEAD_DIM), proj_ref.dtype)
    proj_ref[:, G_SV * LANES + HEAD_DIM:G_SV * LANES + 2 * HEAD_DIM] = ones
    proj_ref[:, G_MV * LANES + HEAD_DIM:G_MV * LANES + 2 * HEAD_DIM] = ones
    ckv_ref[...] = y[:, npj:npj + LANES].astype(ckv_ref.dtype)
    gate_ref[...] = jax.nn.sigmoid(y[:, npj + LANES:])


def _inproj(x2, w_small, feats):
    T, D = x2.shape
    n = w_small.shape[1]
    npj = N_PROJ_GROUPS * LANES
    tm = KEY_TILE
    return pl.pallas_call(
        _inproj_kernel,
        grid=(T // tm,),
        in_specs=[pl.BlockSpec((tm, D), lambda i: (i, 0)),
                  pl.BlockSpec((D, n), lambda i: (0, 0)),
                  pl.BlockSpec((2, tm, LANES), lambda i: (0, 0, 0))],
        out_specs=[pl.BlockSpec((tm, npj), lambda i: (i, 0)),
                   pl.BlockSpec((tm, LANES), lambda i: (i, 0)),
                   pl.BlockSpec((tm, 3 * LANES), lambda i: (i, 0))],
        out_shape=[jax.ShapeDtypeStruct((T, npj), _BF),
                   jax.ShapeDtypeStruct((T, LANES), _BF),
                   jax.ShapeDtypeStruct((T, 3 * LANES), _F32)],
        compiler_params=_params("parallel"),
        name="inproj",
    )(x2, w_small, feats)


_SB_LOG_CUTOFF = -104.0
SB_BLOCKS = 2


def _sb_kernel(q_ref, k_ref, v_ref, o_ref):
    g = pl.program_id(1)
    lane_head = _lane_head()
    rows = N_HEADS * Q_BLOCK
    tri = (_iota((Q_BLOCK, Q_BLOCK), 0) > _iota((Q_BLOCK, Q_BLOCK), 1)).astype(_BF)
    q_off = jnp.bitwise_and(_iota((rows, Q_BLOCK), 0), Q_BLOCK - 1)
    diag_past = _iota((rows, Q_BLOCK), 1) < q_off
    blocks = [g * SB_BLOCKS + n for n in range(SB_BLOCKS)]
    qs = [_stack_heads(q_ref[0, n * Q_BLOCK:(n + 1) * Q_BLOCK, :], lane_head) for n in range(SB_BLOCKS)]

    def tile(q, j, carry, acc, diagonal=False, live=None):
        start = pl.multiple_of(jnp.maximum(j, 0) * Q_BLOCK, Q_BLOCK)
        kt = k_ref[0, pl.ds(start, Q_BLOCK), :]
        vt = v_ref[0, pl.ds(start, Q_BLOCK), :]
        z = _dot_nt(q, kt)
        ls = -(jnp.maximum(z, 0.0) + jnp.log(1.0 + jnp.exp(-jnp.abs(z))))
        if diagonal:
            ls = jnp.where(diag_past, ls, 0.0)
        if live is not None:
            ls = jnp.where(live, ls, 0.0)
        excl = _dot_split(ls, tri)
        w = jnp.exp(z + ls + excl + carry)
        if diagonal:
            w = jnp.where(diag_past, w, 0.0)
        if live is not None:
            w = jnp.where(live, w, 0.0)
        acc = acc + _dot(w, vt)
        carry = carry + jnp.sum(ls, axis=1, keepdims=True)
        return carry, acc

    zero_c, zero_a = jnp.zeros((rows, 1), _F32), jnp.zeros((rows, LANES), _F32)
    state = [tile(qs[n], blocks[n], zero_c, zero_a, diagonal=True) for n in range(SB_BLOCKS)]

    def furthest(t, carries):
        m = jnp.full((), -jnp.inf, _F32)
        for n in range(SB_BLOCKS):
            m = jnp.maximum(m, jnp.where(blocks[n] - 1 - t >= 0, jnp.max(carries[n]), -jnp.inf))
        return m

    def cond(st):
        t, _, cmax = st
        return jnp.logical_and(blocks[-1] - 1 - t >= 0, cmax > _SB_LOG_CUTOFF)

    def body(st):
        t, state, _ = st
        new = []
        for n in range(SB_BLOCKS):
            j = blocks[n] - 1 - t
            new.append(tile(qs[n], j, state[n][0], state[n][1], live=j >= 0))
        return t + 1, new, furthest(t + 1, [c for c, _ in new])

    _, state, _ = lax.while_loop(cond, body, (0, state, furthest(0, [c for c, _ in state])))
    for n in range(SB_BLOCKS):
        o_ref[0, n * Q_BLOCK:(n + 1) * Q_BLOCK, :] = _unstack_heads(state[n][1], lane_head).astype(o_ref.dtype)


def _sb_attention(proj3):
    B, S, _ = proj3.shape
    span = SB_BLOCKS * Q_BLOCK
    return pl.pallas_call(
        _sb_kernel,
        grid=(B, S // span),
        in_specs=[pl.BlockSpec((1, span, LANES), lambda b, i: (b, i, G_SBQ)),
                  pl.BlockSpec((1, S, LANES), lambda b, i: (b, 0, G_SBK)),
                  pl.BlockSpec((1, S, LANES), lambda b, i: (b, 0, G_SBV))],
        out_specs=pl.BlockSpec((1, span, LANES), lambda b, i: (b, i, 0)),
        out_shape=jax.ShapeDtypeStruct((B, S, LANES), _BF),
        compiler_params=_params("parallel", "arbitrary"),
        name="sb_attn",
    )(proj3, proj3, proj3)


def _nsa_compress_kernel(c_ref, w_ref, pe_ref, kc_ref, vc_ref):
    c = c_ref[0]
    a = _dot(c, w_ref[0]) + _dot(pe_ref[0], w_ref[0])[0:1]
    b = _dot(c, w_ref[1]) + _dot(pe_ref[1], w_ref[1])[0:1]
    n = a.shape[0]
    b_next = pltpu.roll(b, n - 1, 0)
    y = a + b_next
    kc_ref[0] = y[:, :LANES].astype(kc_ref.dtype)
    vc_ref[0] = y[:, LANES:].astype(vc_ref.dtype)


def _nsa_compress(ckv3, wc, pe):
    B, nchunk, width = ckv3.shape
    return pl.pallas_call(
        _nsa_compress_kernel,
        grid=(B,),
        in_specs=[pl.BlockSpec((1, nchunk, width), lambda b: (b, 0, 0)),
                  pl.BlockSpec((2, width, 2 * LANES), lambda b: (0, 0, 0)),
                  pl.BlockSpec((2, 8, width), lambda b: (0, 0, 0))],
        out_specs=[pl.BlockSpec((1, nchunk, LANES), lambda b: (b, 0, 0)),
                   pl.BlockSpec((1, nchunk, LANES), lambda b: (b, 0, 0))],
        out_shape=[jax.ShapeDtypeStruct((B, nchunk, LANES), _BF),
                   jax.ShapeDtypeStruct((B, nchunk, LANES), _BF)],
        compiler_params=_params("parallel"),
        name="nsa_compress",
    )(ckv3, wc, pe)


def _masked_exp(z, valid):
    zm = jnp.where(valid, z, MASK_BIAS)
    m = jnp.maximum(jnp.max(zm, axis=1, keepdims=True), SCORE_FLOOR)
    p = jnp.exp(zm - m)
    return p, 1.0 / jnp.maximum(jnp.sum(p, axis=1, keepdims=True), 1e-30)


def _nsa_select_kernel(q_ref, kc_ref, vc_ref, wk_ref, wv_ref, g_ref, m_ref, ocw_ref, selm_ref,
                       ocmp_ref, imp_ref, *, seq):
    qr = q_ref.shape[1]
    q0 = pl.program_id(1) * qr
    ncp = seq // NSA_CMP_STRIDE
    nsel = m_ref.shape[1]
    win = NSA_WINDOW + qr
    lane_head = _lane_head(qr)
    qs = _stack_heads(q_ref[0], lane_head)
    head_rows = lambda a, h: a[h * qr:(h + 1) * qr]

    def compressed(width):
        cend = _iota((1, width), 1) * NSA_CMP_STRIDE + (NSA_CMP_LEN - 1)
        valid_c = (q0 + _iota((qr, width), 0)) >= cend
        rel_c = (cend - q0).astype(_F32)
        zc = _dot_nt(qs, kc_ref[0, :width, :])
        pg = jnp.zeros((qr, width), _F32)
        pcs = []
        for h in range(N_HEADS):
            p, inv = _masked_exp(head_rows(zc, h) + NSA_SLOPES[h] * rel_c, valid_c)
            p = p * inv
            pcs.append(p.astype(_BF))
            pg = pg + p
        ocmp_ref[...] = _unstack_heads(_dot(jnp.concatenate(pcs, axis=0), vc_ref[0, :width, :]), lane_head)
        imp_ref[...] = _dot_split(pg, m_ref[:width, :])

    group = min(ncp, 2 * LANES)
    n_valid = jnp.right_shift(q0 + qr - NSA_CMP_LEN, int(math.log2(NSA_CMP_STRIDE))) + 1
    groups = jnp.minimum((n_valid + group - 1) // group, ncp // group)
    for k in range(1, ncp // group + 1):
        pl.when(groups == k)(functools.partial(compressed, k * group))
    o_cmp = ocmp_ref[...]

    imp = imp_ref[...]
    blk = _iota((qr, nsel), 1)
    cur = jnp.right_shift(q0 + _iota((qr, nsel), 0), int(math.log2(NSA_SEL_LEN)))
    forced = (blk == 0) | (blk == cur) | (blk == cur - 1)
    x = jnp.where(blk > cur, -jnp.inf, jnp.where(forced, jnp.inf, imp))
    blkf = blk.astype(_F32)
    sel = jnp.zeros((qr, nsel), _F32)
    for _ in range(min(NSA_TOP_N, seq // NSA_SEL_LEN)):
        _, idx = _first_max(x, blkf, float(nsel))
        hit = blkf == idx
        sel = jnp.where(hit, 1.0, sel)
        x = jnp.where(hit, -jnp.inf, x)
    selm_ref[0] = sel.astype(selm_ref.dtype)

    ws = pl.multiple_of(jnp.maximum(q0 - NSA_WINDOW, 0), LANES)
    kw = wk_ref[0, pl.ds(ws, win), :]
    vw = wv_ref[0, pl.ds(ws, win), :]
    kpos = ws + _iota((1, win), 1)
    dw = (q0 + _iota((qr, win), 0)) - kpos
    valid_w = (dw >= 0) & (dw < NSA_WINDOW)
    rel_w = (kpos - q0).astype(_F32)
    zw = _dot_nt(qs, kw)
    pws, invs = [], []
    for h in range(N_HEADS):
        p, inv = _masked_exp(head_rows(zw, h) + NSA_SLOPES[h] * rel_w, valid_w)
        pws.append(p.astype(_BF))
        invs.append(jnp.broadcast_to(inv, (qr, LANES)))
    o_win = _unstack_heads(_dot(jnp.concatenate(pws, axis=0), vw) * jnp.concatenate(invs, axis=0), lane_head)

    g = g_ref[0]
    ocw_ref[0] = g[:, :LANES] * o_cmp + g[:, 2 * LANES:] * o_win


def _nsa_select(proj3, kc4, vc4, gate3, imp_mat):
    B, S, _ = proj3.shape
    ncp = S // NSA_CMP_STRIDE
    nsel = imp_mat.shape[1]
    qr = NSA_SELECT_ROWS
    assert S % qr == 0 and S >= NSA_WINDOW + qr
    return pl.pallas_call(
        functools.partial(_nsa_select_kernel, seq=S),
        grid=(B, S // qr),
        in_specs=[pl.BlockSpec((1, qr, LANES), lambda b, i: (b, i, G_NQ)),
                  pl.BlockSpec((1, ncp, LANES), lambda b, i: (b, 0, 0)),
                  pl.BlockSpec((1, ncp, LANES), lambda b, i: (b, 0, 0)),
                  pl.BlockSpec((1, S, LANES), lambda b, i: (b, 0, G_WK)),
                  pl.BlockSpec((1, S, LANES), lambda b, i: (b, 0, G_WV)),
                  pl.BlockSpec((1, qr, 3 * LANES), lambda b, i: (b, i, 0)),
                  pl.BlockSpec((ncp, nsel), lambda b, i: (0, 0))],
        out_specs=[pl.BlockSpec((1, qr, LANES), lambda b, i: (b, i, 0)),
                   pl.BlockSpec((1, qr, nsel), lambda b, i: (b, i, 0))],
        out_shape=[jax.ShapeDtypeStruct((B, S, LANES), _F32),
                   jax.ShapeDtypeStruct((B, S, nsel), _BF)],
        scratch_shapes=[pltpu.VMEM((qr, LANES), _F32), pltpu.VMEM((qr, nsel), _F32)],
        compiler_params=_params("parallel", "arbitrary"),
        name="nsa_select",
    )(proj3, kc4, vc4, proj3, proj3, gate3, imp_mat)


def _moba_mean_kernel(k_ref, o_ref, *, nb):
    k = k_ref[0].astype(_F32)
    o_ref[0] = jnp.zeros(o_ref.shape[1:], _F32)
    o_ref[0, :nb, :] = jnp.mean(k.reshape(nb, MOBA_BLOCK, LANES), axis=1)


def _moba_mean(proj3):
    B, S, _ = proj3.shape
    nb = S // MOBA_BLOCK
    return pl.pallas_call(
        functools.partial(_moba_mean_kernel, nb=nb),
        grid=(B,),
        in_specs=[pl.BlockSpec((1, S, LANES), lambda b: (b, 0, G_MK))],
        out_specs=pl.BlockSpec((1, LANES, LANES), lambda b: (b, 0, 0)),
        out_shape=jax.ShapeDtypeStruct((B, LANES, LANES), _F32),
        compiler_params=_params("parallel"),
        name="moba_mean",
    )(proj3)


def _moba_select_kernel(q_ref, km_ref, selm_ref, *, nb):
    rows = q_ref.shape[1]
    q0 = pl.program_id(1) * rows
    sg = _dot_nt(q_ref[0], km_ref[0])
    blk = _iota((rows, LANES), 1)
    cur = jnp.right_shift(q0 + _iota((rows, LANES), 0), int(math.log2(MOBA_BLOCK)))
    x = jnp.where(blk < cur, sg, -jnp.inf)
    blkf = blk.astype(_F32)
    sel = jnp.where(blk == cur, 1.0, 0.0)
    for _ in range(min(MOBA_TOPK, nb)):
        m, idx = _first_max(x, blkf, float(LANES))
        hit = blkf == idx
        sel = jnp.where(hit & (m > -jnp.inf), 1.0, sel)
        x = jnp.where(hit, -jnp.inf, x)
    selm_ref[0] = sel.astype(selm_ref.dtype)


def _moba_select(proj3, kmean):
    B, S, _ = proj3.shape
    nb = S // MOBA_BLOCK
    rows = MOBA_SELECT_ROWS
    assert nb <= LANES and S % rows == 0
    return pl.pallas_call(
        functools.partial(_moba_select_kernel, nb=nb),
        grid=(B, S // rows),
        in_specs=[pl.BlockSpec((1, rows, LANES), lambda b, i: (b, i, G_MQ)),
                  pl.BlockSpec((1, LANES, LANES), lambda b, i: (b, 0, 0))],
        out_specs=pl.BlockSpec((1, rows, LANES), lambda b, i: (b, i, 0)),
        out_shape=jax.ShapeDtypeStruct((B, S, LANES), _BF),
        compiler_params=_params("parallel", "arbitrary"),
        name="moba_select",
    )(proj3, kmean)


def _blk_attn_kernel(*refs, blk_len, tile, qb, slopes, nblk, seq, gated):
    if gated:
        q_ref, kx_ref, v_ref, selm_ref, g_ref, add_ref, o_ref = refs[:7]
    else:
        q_ref, kx_ref, v_ref, selm_ref, o_ref = refs[:5]
    qx_ref, m_ref, acc_ref, za_ref, zb_ref, flag_ref, list_ref = refs[-7:]
    rows = N_HEADS * qb
    bpt = tile // blk_len
    max_tiles = seq // tile
    i = pl.program_id(1)
    q0 = i * qb
    lane_head = _lane_head(qb)
    lane = _iota((qb, LANES), 1)

    qx_ref[:, :LANES] = _stack_heads(q_ref[0], lane_head)
    m_ref[...] = jnp.full((rows, LANES), SCORE_FLOOR, _F32)
    acc_ref[...] = jnp.zeros((rows, LANES), _F32)
    lane1 = _iota((1, LANES), 1)
    fixed_lanes = [jnp.where(lane1 == F_HI, slopes[h] * F_SPLIT, jnp.where(lane1 == F_LO, slopes[h], 0.0))
                   for h in range(N_HEADS)]
    dist_lanes = [jnp.where(lane1 == F_ONE, slopes[h] * LANES, 0.0) for h in range(N_HEADS)]

    selm = selm_ref[0].astype(_F32)
    sneg = (1.0 - selm) * MASK_BIAS
    n_tiles = jnp.right_shift(q0 + qb + tile - 1, int(math.log2(tile)))

    def scores(j, z_ref):
        k0 = pl.multiple_of(j * tile, tile)
        first = j * bpt
        if nblk > LANES:
            half = jnp.right_shift(first, int(math.log2(LANES)))
            base = sneg[:, :LANES]
            for c in range(1, nblk // LANES):
                base = jnp.where(half == c, sneg[:, c * LANES:(c + 1) * LANES], base)
        else:
            base = sneg
        shift = jnp.bitwise_and(LANES - jnp.bitwise_and(first, LANES - 1), LANES - 1)
        tile_mask = pltpu.roll(base, shift, 1)
        tile_mask = jnp.where(lane >= F_HI, 0.0, tile_mask)
        dist = (j * (tile // LANES) - i * (qb // LANES)).astype(_F32)
        for h in range(N_HEADS):
            ext = tile_mask + (fixed_lanes[h] + dist * dist_lanes[h])
            qx_ref[h * qb:(h + 1) * qb, LANES:] = ext.astype(_BF)
        kx = kx_ref[0, pl.ds(k0, tile), :]
        z_ref[...] = lax.dot_general(qx_ref[...], kx, (((1,), (1,)), ((), ())),
                                     preferred_element_type=_F32)

    def absorb(z_ref, j, diagonal):
        k0 = pl.multiple_of(j * tile, tile)
        vt = v_ref[0, pl.ds(k0, tile), :]
        z = z_ref[...]
        if diagonal:
            q_off = jnp.bitwise_and(_iota((rows, tile), 0), qb - 1)
            z = jnp.where(_iota((rows, tile), 1) - q_off <= q0 - k0, z, MASK_BIAS)
        m_old = m_ref[...]
        m_new = jnp.maximum(m_old, jnp.max(z, axis=1, keepdims=True))
        p = jnp.exp(z - jnp.concatenate([m_new] * (tile // LANES), axis=1))
        a = jnp.exp(m_old - m_new)
        acc_ref[...] = a * acc_ref[...] + _dot(p, vt)
        m_ref[...] = m_new

    scores(n_tiles - 1, za_ref)

    any_q = jnp.broadcast_to(jnp.max(selm, axis=0, keepdims=True), (8, nblk))
    group = (jnp.right_shift(_iota((nblk, LANES), 0), int(math.log2(bpt))) == _iota((nblk, LANES), 1))
    tile_hits = _dot(any_q, jnp.where(group, 1.0, 0.0))
    for t in range(max_tiles):
        flag_ref[t] = (tile_hits[0, t] > 0.5).astype(jnp.int32)

    def scan(t, n):
        @pl.when(flag_ref[t] > 0)
        def _():
            list_ref[n] = t
        return n + (flag_ref[t] > 0).astype(jnp.int32)

    n_list = lax.fori_loop(0, n_tiles - 1, scan, 0)

    @pl.when(n_list == 0)
    def _():
        absorb(za_ref, n_tiles - 1, True)

    @pl.when(n_list > 0)
    def _():
        scores(list_ref[0], zb_ref)
        absorb(za_ref, n_tiles - 1, True)

    def body(kk, c):
        scores(list_ref[2 * kk + 1], za_ref)
        absorb(zb_ref, list_ref[2 * kk], False)
        scores(list_ref[2 * kk + 2], zb_ref)
        absorb(za_ref, list_ref[2 * kk + 1], False)
        return c

    n_pairs = jnp.right_shift(n_list - 1, 1)
    lax.fori_loop(0, n_pairs, body, 0)
    left = n_list - 1 - 2 * n_pairs

    @pl.when(jnp.logical_and(n_list > 0, left == 0))
    def _():
        absorb(zb_ref, list_ref[n_list - 1], False)

    @pl.when(jnp.logical_and(n_list > 0, left == 1))
    def _():
        scores(list_ref[n_list - 1], za_ref)
        absorb(zb_ref, list_ref[n_list - 2], False)
        absorb(za_ref, list_ref[n_list - 1], False)

    acc = acc_ref[...]
    total = pltpu.roll(acc, LANES - HEAD_DIM, 1)
    o = acc * (1.0 / jnp.maximum(total, 1e-30))
    out = jnp.zeros((qb, LANES), _F32)
    for h in range(N_HEADS):
        o_h = o[h * qb:(h + 1) * qb]
        if h:
            o_h = pltpu.roll(o_h, h * HEAD_DIM, 1)
        out = out + jnp.where(lane_head == h, o_h, 0.0)
    if gated:
        out = add_ref[0] + g_ref[0] * out
    o_ref[0] = out.astype(o_ref.dtype)


def _blk_attention(proj3, selm, gq, gkx, gv, blk_len, tile, qb, slopes, gate3=None, addend=None):
    B, S, _ = proj3.shape
    nblk = selm.shape[-1]
    assert S % tile == 0 and gkx % 2 == 0 and nblk % LANES == 0
    gated = gate3 is not None
    rows = N_HEADS * qb
    in_specs = [pl.BlockSpec((1, qb, LANES), lambda b, i: (b, i, gq)),
                pl.BlockSpec((1, S, 2 * LANES), lambda b, i: (b, 0, gkx // 2)),
                pl.BlockSpec((1, S, LANES), lambda b, i: (b, 0, gv)),
                pl.BlockSpec((1, qb, nblk), lambda b, i: (b, i, 0))]
    args = [proj3, proj3, proj3, selm]
    if gated:
        in_specs += [pl.BlockSpec((1, qb, LANES), lambda b, i: (b, i, 1)),
                     pl.BlockSpec((1, qb, LANES), lambda b, i: (b, i, 0))]
        args += [gate3, addend]
    return pl.pallas_call(
        functools.partial(_blk_attn_kernel, blk_len=blk_len, tile=tile, qb=qb, slopes=slopes, nblk=nblk, seq=S,
                          gated=gated),
        grid=(B, S // qb),
        in_specs=in_specs,
        out_specs=pl.BlockSpec((1, qb, LANES), lambda b, i: (b, i, 0)),
        out_shape=jax.ShapeDtypeStruct((B, S, LANES), _BF),
        scratch_shapes=[pltpu.VMEM((rows, 2 * LANES), _BF),
                        pltpu.VMEM((rows, LANES), _F32),
                        pltpu.VMEM((rows, LANES), _F32),
                        pltpu.VMEM((rows, tile), _F32),
                        pltpu.VMEM((rows, tile), _F32),
                        pltpu.SMEM((S // tile,), jnp.int32),
                        pltpu.SMEM((S // tile,), jnp.int32)],
        compiler_params=_params("parallel", "arbitrary"),
        name="blk_attn_%d" % blk_len,
    )(*args)


def _merge_kernel(x_ref, osb_ref, onsa_ref, omb_ref, wg_ref, wbr_ref, wo_ref, lg_ref, lb_ref, o_ref,
                  *, alpha):
    x = x_ref[...]
    d = x.shape[1]
    gates = jax.nn.sigmoid(_dot(x, wg_ref[...]))
    mix = (gates[:, :d] * _dot(osb_ref[...], wbr_ref[0])
           + gates[:, d:2 * d] * _dot(onsa_ref[...], wbr_ref[1])
           + gates[:, 2 * d:] * _dot(omb_ref[...], wbr_ref[2]))
    y = alpha * x + _dot(mix, wo_ref[...])
    o_ref[...] = _layer_norm(y, lg_ref[...], lb_ref[...])


def _merge(x2, o_sb, o_nsa, o_mb, w_gate, w_br, w_out, ln_g, ln_b, alpha, tm=256):
    T, D = x2.shape
    row = lambda i: (i, 0)
    fixed2 = lambda i: (0, 0)
    return pl.pallas_call(
        functools.partial(_merge_kernel, alpha=alpha),
        grid=(T // tm,),
        in_specs=[pl.BlockSpec((tm, D), row),
                  pl.BlockSpec((tm, LANES), row),
                  pl.BlockSpec((tm, LANES), row),
                  pl.BlockSpec((tm, LANES), row),
                  pl.BlockSpec((D, N_BRANCHES * D), fixed2),
                  pl.BlockSpec((N_BRANCHES, LANES, D), lambda i: (0, 0, 0)),
                  pl.BlockSpec((D, D), fixed2),
                  pl.BlockSpec((1, D), fixed2),
                  pl.BlockSpec((1, D), fixed2)],
        out_specs=pl.BlockSpec((tm, D), row),
        out_shape=jax.ShapeDtypeStruct((T, D), _F32),
        compiler_params=_params("parallel"),
        name="merge",
    )(x2, o_sb, o_nsa, o_mb, w_gate, w_br, w_out, ln_g, ln_b)


_NOT_RETRIEVED = 99.0


def _top_rows(s, k, exact):
    n = s.shape[0]
    rows = _iota(s.shape, 0).astype(_F32)
    rank = jnp.full(s.shape, _NOT_RETRIEVED, _F32)
    vals = []
    for p in range(k):
        m = jnp.max(s, axis=0, keepdims=True)
        hit = s == m
        if exact:
            hit = rows == jnp.min(jnp.where(hit, rows, float(n)), axis=0, keepdims=True)
        vals.append(m)
        rank = jnp.where(hit, float(p), rank)
        s = jnp.where(hit, -jnp.inf, s)
    return vals, rank


def _peer_route_kernel(x_ref, wq_ref, k1_ref, k2_ref, fa_ref, qb_ref, e1_ref, e2_ref):
    half = PEER_QDIM // 2
    qf = _dot(x_ref[...], wq_ref[...]).astype(_BF)
    tm = qf.shape[0]
    rows_k = _iota((PEER_TOPK, tm), 0)
    ncand = PEER_TOPK * PEER_TOPK
    pos = _iota((ncand, tm), 0).astype(_F32)

    def route(exact):
        most = jnp.zeros((1, tm), _F32)
        for h in range(PEER_HEADS):
            s1 = _dot_nt(k1_ref[h], qf[:, h * PEER_QDIM:h * PEER_QDIM + half])
            s2 = _dot_nt(k2_ref[h], qf[:, h * PEER_QDIM + half:(h + 1) * PEER_QDIM])
            v1, r1 = _top_rows(s1, PEER_TOPK, exact)
            v2, r2 = _top_rows(s2, PEER_TOPK, exact)
            for r in (r1, r2):
                most = jnp.maximum(most, jnp.sum(jnp.where(r < _NOT_RETRIEVED, 1.0, 0.0), axis=0, keepdims=True))
            v2_all = jnp.zeros((PEER_TOPK, tm), _F32)
            for qi in range(PEER_TOPK):
                v2_all = jnp.where(rows_k == qi, v2[qi], v2_all)
            cand = jnp.concatenate([v1[p] + v2_all for p in range(PEER_TOPK)], axis=0)
            c = cand
            pickf = jnp.zeros(cand.shape, _F32)
            for _ in range(PEER_TOPK):
                m = jnp.max(c, axis=0, keepdims=True)
                idx = jnp.min(jnp.where(c == m, pos, float(ncand)), axis=0, keepdims=True)
                hit = pos == idx
                pickf = jnp.where(hit, 1.0, pickf)
                c = jnp.where(hit, -jnp.inf, c)
            cmax = v1[0] + v2[0]
            z = jnp.sum(pickf * jnp.exp(cand - cmax), axis=0, keepdims=True)
            fa = jnp.zeros((PEER_NKEYS, tm), _F32)
            for p in range(PEER_TOPK):
                count = jnp.sum(pickf[p * PEER_TOPK:(p + 1) * PEER_TOPK], axis=0, keepdims=True)
                fa = jnp.where(r1 == float(p), count, fa)
            fa_ref[h] = fa
            qb_ref[h] = r2.astype(qb_ref.dtype)
            e1_ref[h] = jnp.exp(s1 - v1[0]) * (1.0 / z)
            e2_ref[h] = jnp.exp(s2 - v2[0]).astype(e2_ref.dtype)
        return jnp.max(most)

    @pl.when(route(False) > PEER_TOPK)
    def _():
        route(True)


def _peer_route(x2, wq, k1, k2, tm=256):
    T, D = x2.shape
    tab = jax.ShapeDtypeStruct((PEER_HEADS, PEER_NKEYS, T), _F32)
    tab_bf = jax.ShapeDtypeStruct((PEER_HEADS, PEER_NKEYS, T), _BF)
    tab_spec = pl.BlockSpec((PEER_HEADS, PEER_NKEYS, tm), lambda i: (0, 0, i))
    return pl.pallas_call(
        _peer_route_kernel,
        grid=(T // tm,),
        in_specs=[pl.BlockSpec((tm, D), lambda i: (i, 0)),
                  pl.BlockSpec(wq.shape, lambda i: (0, 0)),
                  pl.BlockSpec(k1.shape, lambda i: (0, 0, 0)),
                  pl.BlockSpec(k2.shape, lambda i: (0, 0, 0))],
        out_specs=[tab_spec] * 4,
        out_shape=[tab, tab_bf, tab, tab_bf],
        compiler_params=_params("parallel"),
        name="peer_route",
    )(x2, wq, k1, k2)


def _gelu_tanh(s):
    c1 = math.sqrt(2.0 / math.pi)
    k1 = jnp.asarray(c1, s.dtype)
    k2 = jnp.asarray(c1 * 0.044715, s.dtype)
    inner = s * (k1 + k2 * (s * s))
    half = jnp.asarray(0.5, s.dtype) * s
    return half + half * jnp.tanh(inner)


def _peer_dense_kernel(x_ref, xr_ref, u_ref, vt_ref, fa_ref, qb_ref, e1_ref, e2_ref, lg_ref, lb_ref, o_ref,
                       acc_ref, xb_ref, s_ref, c_ref, *, alpha, te, n_tiles, n_work):
    s_idx = pl.program_id(0)
    j_score = s_idx % n_tiles
    j_gate = jnp.maximum(s_idx - 1, 0) % n_tiles
    j_value = jnp.maximum(s_idx - 2, 0) % n_tiles

    @pl.when(s_idx == 0)
    def _():
        s_ref[...] = jnp.zeros_like(s_ref)
        c_ref[...] = jnp.zeros_like(c_ref)
        acc_ref[...] = jnp.zeros_like(acc_ref)

    @pl.when(jnp.logical_and(j_score == 0, s_idx < n_work))
    def _():
        xb_ref[...] = x_ref[...].astype(_BF)

    fresh = j_value == 0
    cur = s_idx % 2
    tm = s_ref.shape[2]
    tc = tm // 2
    for ck in range(tm // tc):
        cols = slice(ck * tc, (ck + 1) * tc)
        acc_ref[:, cols] = jnp.where(fresh, 0.0, acc_ref[:, cols]) + jnp.dot(
            vt_ref[...], c_ref[1 - cur, :, cols], preferred_element_type=_F32)
        s = s_ref[1 - cur, :, cols]
        act = _gelu_tanh(s)
        for r in range(te // PEER_NKEYS):
            a = j_gate * (te // PEER_NKEYS) + r
            gate = jnp.zeros((PEER_NKEYS, tc), _BF)
            for h in range(PEER_HEADS):
                fa = fa_ref[h, pl.ds(a, 1), cols].astype(_BF)
                e1 = e1_ref[h, pl.ds(a, 1), cols].astype(_BF)
                gate = gate + e1 * jnp.where(qb_ref[h, :, cols] < fa, e2_ref[h, :, cols], jnp.zeros((), _BF))
            c_ref[cur, r * PEER_NKEYS:(r + 1) * PEER_NKEYS, cols] = (
                gate * act[r * PEER_NKEYS:(r + 1) * PEER_NKEYS])
        s_ref[cur, :, cols] = _dot_nt(u_ref[...], xb_ref[cols, :]).astype(_BF)

    @pl.when(jnp.logical_and(j_value == n_tiles - 1, s_idx >= 2))
    def _():
        y = alpha * xr_ref[...] + acc_ref[...].T
        o_ref[...] = _layer_norm(y, lg_ref[...], lb_ref[...])


def _peer_dense(x2, u_all, vt_all, layer, tabs, ln_g, ln_b, alpha, tm=512):
    T, D = x2.shape
    n_tiles, te = vt_all.shape[1], vt_all.shape[3]
    n_tok = T // tm
    n_work = n_tok * n_tiles
    last = n_work - 1
    tok = lambda lag: (lambda s: (jnp.clip(s - lag, 0, last) // n_tiles, 0))
    exp_tile = lambda lag: (lambda s: jnp.clip(s - lag, 0, last) % n_tiles)
    tab_spec = pl.BlockSpec((PEER_HEADS, PEER_NKEYS, tm), lambda s: (0, 0, jnp.clip(s - 1, 0, last) // n_tiles))
    return pl.pallas_call(
        functools.partial(_peer_dense_kernel, alpha=alpha, te=te, n_tiles=n_tiles, n_work=n_work),
        grid=(n_work + 2,),
        in_specs=[pl.BlockSpec((tm, D), tok(0)),
                  pl.BlockSpec((tm, D), tok(2)),
                  pl.BlockSpec((None, te, D), lambda s: (layer, exp_tile(0)(s), 0)),
                  pl.BlockSpec((None, None, D, te), lambda s: (layer, exp_tile(2)(s), 0, 0)),
                  tab_spec, tab_spec, tab_spec, tab_spec,
                  pl.BlockSpec((1, D), lambda s: (0, 0)),
                  pl.BlockSpec((1, D), lambda s: (0, 0))],
        out_specs=pl.BlockSpec((tm, D), tok(2)),
        out_shape=jax.ShapeDtypeStruct((T, D), _F32),
        scratch_shapes=[pltpu.VMEM((D, tm), _F32), pltpu.VMEM((tm, D), _BF),
                        pltpu.VMEM((2, te, tm), _BF), pltpu.VMEM((2, te, tm), _BF)],
        compiler_params=_params("arbitrary"),
        name="peer_dense",
    )(x2, x2, u_all, vt_all, *tabs, ln_g, ln_b)


def _in_widths(d_model):
    w = N_HEADS * HEAD_DIM
    return (w, w, w, w) + (HEAD_DIM,) * 6 + (3 * N_HEADS, w, HEAD_DIM, HEAD_DIM, N_BRANCHES * d_model)


def _arrange_w_in(w_in, d_model):
    off = np.concatenate([[0], np.cumsum(_in_widths(d_model))])
    scale = HEAD_DIM ** -0.5
    seg = lambda n: w_in[:, off[n]:off[n + 1]]
    rep = lambda n: jnp.tile(seg(n), (1, N_HEADS))
    gate_cols = np.array([off[10] + h * 3 + c for c in range(3) for h in range(N_HEADS)
                          for _ in range(HEAD_DIM)])
    ckv = jnp.concatenate([seg(4), seg(5), jnp.zeros((w_in.shape[0], LANES - 2 * HEAD_DIM), w_in.dtype)], axis=1)
    blank = jnp.zeros((w_in.shape[0], LANES), w_in.dtype)
    once = lambda n: jnp.concatenate([seg(n), blank[:, HEAD_DIM:]], axis=1)
    groups = [seg(0) * scale, seg(1), seg(2), seg(3) * scale, rep(6), blank, rep(12), blank,
              once(7), rep(8), rep(9), seg(11) * scale, once(13), ckv, w_in[:, gate_cols]]
    return jnp.concatenate(groups, axis=1).astype(_BF), seg(14).astype(_BF)


def _arrange_compress(w_ck, w_cv, pe_k, pe_v):
    half = NSA_CMP_STRIDE
    wk = jnp.tile(w_ck.reshape(2, half, HEAD_DIM, HEAD_DIM), (1, 1, 1, N_HEADS))
    wv = jnp.tile(w_cv.reshape(2, half, HEAD_DIM, HEAD_DIM), (1, 1, 1, N_HEADS))
    w = jnp.zeros((2, half, LANES, 2 * LANES), _F32)
    w = w.at[:, :, :HEAD_DIM, :LANES].set(wk)
    w = w.at[:, :, HEAD_DIM:2 * HEAD_DIM, LANES:].set(wv)
    pe = jnp.zeros((2, half, LANES), _F32)
    pe = pe.at[:, :, :HEAD_DIM].set(pe_k.reshape(2, half, HEAD_DIM))
    pe = pe.at[:, :, HEAD_DIM:2 * HEAD_DIM].set(pe_v.reshape(2, half, HEAD_DIM))
    pe = jnp.broadcast_to(pe.reshape(2, 1, half * LANES), (2, 8, half * LANES))
    return w.reshape(2, half * LANES, 2 * LANES).astype(_BF), pe.astype(_BF)


def _key_features():
    out = np.zeros((2, KEY_TILE, LANES), np.float32)
    for n, (blk_len, tile) in enumerate(((NSA_SEL_LEN, NSA_KEY_TILE), (MOBA_BLOCK, MOBA_KEY_TILE))):
        c = np.arange(KEY_TILE) % tile
        out[n, np.arange(KEY_TILE), c // blk_len] = 1.0
        out[n, :, F_HI] = c // F_SPLIT
        out[n, :, F_LO] = c % F_SPLIT
        out[n, :, F_ONE] = 1.0
    return jnp.asarray(out, _BF)


def _importance_matrix(seq):
    ncp = seq // NSA_CMP_STRIDE
    nsel = seq // NSA_SEL_LEN
    ratio = NSA_SEL_LEN // NSA_CMP_STRIDE
    overlap = np.convolve(np.ones(ratio), np.ones(NSA_CMP_LEN // NSA_CMP_STRIDE))
    n_left = (NSA_CMP_LEN - NSA_CMP_STRIDE) // NSA_CMP_STRIDE
    m = np.zeros((ncp, -(-nsel // LANES) * LANES), np.float32)
    for j in range(nsel):
        for o, c in enumerate(overlap):
            n = ratio * j + o - n_left
            if 0 <= n < ncp - 1:
                m[n, j] += c
    return jnp.asarray(m, _BF)


def kernel(x, w_in, nsa_pe_k, nsa_pe_v, nsa_w_ck, nsa_w_cv, w_br_sb, w_br_nsa, w_br_moba, w_out, ln1_g, ln1_b, peer_wq, peer_k1, peer_k2, peer_u, peer_v, ln2_g, ln2_b):
    B, S, D = x.shape
    depth = w_in.shape[0]
    T = B * S
    assert S % MOBA_BLOCK == 0 and S >= NSA_WINDOW + Q_BLOCK
    alpha = (2.0 * depth) ** 0.25
    imp_mat = _importance_matrix(S)
    feats = _key_features()
    u_all = peer_u.astype(_BF)
    te = PEER_EXPERT_TILE
    vt_all = jnp.swapaxes(peer_v.reshape(depth, -1, te, D), 2, 3).astype(_BF)
    x2 = x.reshape(T, D)
    for l in range(depth):
        w_small, w_gate = _arrange_w_in(w_in[l], D)
        wc, pe = _arrange_compress(nsa_w_ck[l], nsa_w_cv[l], nsa_pe_k[l], nsa_pe_v[l])
        proj, ckv, ngate = _inproj(x2, w_small, feats)
        proj3 = proj.reshape(B, S, N_PROJ_GROUPS * LANES)
        gate3 = ngate.reshape(B, S, 3 * LANES)

        o_sb = _sb_attention(proj3)

        kc4, vc4 = _nsa_compress(ckv.reshape(B, S // NSA_CMP_STRIDE, NSA_CMP_STRIDE * LANES), wc, pe)
        o_cw, nsa_selm = _nsa_select(proj3, kc4, vc4, gate3, imp_mat)
        o_nsa = _blk_attention(proj3, nsa_selm, G_NQ, G_SK, G_SV, NSA_SEL_LEN, NSA_KEY_TILE, Q_BLOCK, NSA_SLOPES,
                               gate3=gate3, addend=o_cw)

        mb_selm = _moba_select(proj3, _moba_mean(proj3))
        o_mb = _blk_attention(proj3, mb_selm, G_MQ, G_MK, G_MV, MOBA_BLOCK, MOBA_KEY_TILE, MOBA_Q_ROWS, MOBA_SLOPES)

        w_br = jnp.stack([w_br_sb[l], w_br_nsa[l], w_br_moba[l]]).astype(_BF)
        x2 = _merge(x2, o_sb.reshape(T, LANES), o_nsa.reshape(T, LANES), o_mb.reshape(T, LANES),
                    w_gate, w_br, w_out[l].astype(_BF), ln1_g[l].reshape(1, D), ln1_b[l].reshape(1, D), alpha)

        wq = peer_wq[l].reshape(D, PEER_HEADS * PEER_QDIM).astype(_BF)
        tabs = _peer_route(x2, wq, peer_k1[l].astype(_BF), peer_k2[l].astype(_BF))
        x2 = _peer_dense(x2, u_all, vt_all, l, tabs, ln2_g[l].reshape(1, D), ln2_b[l].reshape(1, D), alpha)
    return x2.reshape(B, S, D)
```

```python
import functools
import math

import numpy as np
import jax
import jax.numpy as jnp
from jax import lax
from jax.experimental import pallas as pl
from jax.experimental.pallas import tpu as pltpu

HEAD_DIM = 32
N_HEADS = 4
Q_BLOCK = 128
NSA_CMP_LEN = 32
NSA_CMP_STRIDE = 16
NSA_SEL_LEN = 64
NSA_TOP_N = 8
NSA_WINDOW = 512
MOBA_BLOCK = 256
MOBA_TOPK = 3
PEER_HEADS = 4
PEER_NKEYS = 128
PEER_TOPK = 8
PEER_QDIM = 256
N_BRANCHES = 3
LN_EPS = 1e-5

LANES = 128
VMEM_LIMIT = 48 * 1024 * 1024

_BF = jnp.bfloat16
_F32 = jnp.float32

_ALIBI = [2.0 ** (-8.0 * (i + 1) / (2 * N_HEADS)) for i in range(2 * N_HEADS)]
NSA_SLOPES = tuple(_ALIBI[0::2])
MOBA_SLOPES = tuple(_ALIBI[1::2])

G_SBQ, G_SBK, G_SBV, G_NQ, G_SK, G_SKF, G_MK, G_MKF, G_SV, G_WK, G_WV, G_MQ, G_MV = range(13)
N_PROJ_GROUPS = 13
NSA_KEY_TILE = 512
MOBA_KEY_TILE = 1024
MOBA_Q_ROWS = 256
MOBA_SELECT_ROWS = 512
NSA_SELECT_ROWS = 256
NSA_ATTN_ROWS = 256
KEY_TILE = max(NSA_KEY_TILE, MOBA_KEY_TILE)
PEER_EXPERT_TILE = 1024
F_HI, F_LO, F_ONE = 125, 126, 127
F_SPLIT = 256
MASK_BIAS = -1e30
SCORE_FLOOR = -5e29


def _dot(a, b):
    return jnp.dot(a.astype(_BF), b.astype(_BF), preferred_element_type=_F32)


def _dot_nt(a, b):
    return lax.dot_general(a.astype(_BF), b.astype(_BF), (((1,), (1,)), ((), ())),
                           preferred_element_type=_F32)


def _dot_split(a, b):
    hi = a.astype(_BF)
    lo = (a - hi.astype(_F32)).astype(_BF)
    return (jnp.dot(hi, b, preferred_element_type=_F32)
            + jnp.dot(lo, b, preferred_element_type=_F32))


def _iota(shape, dim):
    return lax.broadcasted_iota(jnp.int32, shape, dim)


def _lane_head(rows=Q_BLOCK):
    return jnp.right_shift(_iota((rows, LANES), 1), int(math.log2(HEAD_DIM)))


def _head_queries(q, lane_head):
    qf = q.astype(_F32)
    return [jnp.where(lane_head == h, qf, 0.0).astype(_BF) for h in range(N_HEADS)]


def _stack_heads(q, lane_head):
    return jnp.concatenate(_head_queries(q, lane_head), axis=0)


def _unstack_heads(acc, lane_head):
    rows = lane_head.shape[0]
    out = jnp.zeros((rows, LANES), _F32)
    for h in range(N_HEADS):
        out = out + jnp.where(lane_head == h, acc[h * rows:(h + 1) * rows], 0.0)
    return out


def _first_max(x, ids, none):
    m = jnp.max(x, axis=1, keepdims=True)
    return m, jnp.min(jnp.where(x == m, ids, none), axis=1, keepdims=True)


def _params(*sem):
    return pltpu.CompilerParams(dimension_semantics=sem, vmem_limit_bytes=VMEM_LIMIT)


def _layer_norm(y, g, b):
    mu = jnp.mean(y, axis=-1, keepdims=True)
    d = y - mu
    var = jnp.mean(d * d, axis=-1, keepdims=True)
    return d * lax.rsqrt(var + LN_EPS) * g + b


def _inproj_kernel(x_ref, w_ref, feat_ref, proj_ref, ckv_ref, gate_ref):
    y = _dot(x_ref[...], w_ref[...])
    npj = N_PROJ_GROUPS * LANES
    proj_ref[...] = y[:, :npj].astype(proj_ref.dtype)
    proj_ref[:, G_SKF * LANES:(G_SKF + 1) * LANES] = feat_ref[0]
    proj_ref[:, G_MKF * LANES:(G_MKF + 1) * LANES] = feat_ref[1]
    ones = jnp.ones((y.shape[0], HEAD_DIM), proj_ref.dtype)
    proj_ref[:, G_SV * LANES + HEAD_DIM:G_SV * LANES + 2 * HEAD_DIM] = ones
    proj_ref[:, G_MV * LANES + HEAD_DIM:G_MV * LANES + 2 * HEAD_DIM] = ones
    ckv_ref[...] = y[:, npj:npj + LANES].astype(ckv_ref.dtype)
    gate_ref[...] = jax.nn.sigmoid(y[:, npj + LANES:])


def _inproj(x2, w_small, feats):
    T, D = x2.shape
    n = w_small.shape[1]
    npj = N_PROJ_GROUPS * LANES
    tm = KEY_TILE
    return pl.pallas_call(
        _inproj_kernel,
        grid=(T // tm,),
        in_specs=[pl.BlockSpec((tm, D), lambda i: (i, 0)),
                  pl.BlockSpec((D, n), lambda i: (0, 0)),
                  pl.BlockSpec((2, tm, LANES), lambda i: (0, 0, 0))],
        out_specs=[pl.BlockSpec((tm, npj), lambda i: (i, 0)),
                   pl.BlockSpec((tm, LANES), lambda i: (i, 0)),
                   pl.BlockSpec((tm, 3 * LANES), lambda i: (i, 0))],
        out_shape=[jax.ShapeDtypeStruct((T, npj), _BF),
                   jax.ShapeDtypeStruct((T, LANES), _BF),
                   jax.ShapeDtypeStruct((T, 3 * LANES), _F32)],
        compiler_params=_params("parallel"),
        name="inproj",
    )(x2, w_small, feats)


_SB_LOG_CUTOFF = -104.0
SB_BLOCKS = 2


def _sb_kernel(q_ref, k_ref, v_ref, o_ref):
    g = pl.program_id(1)
    lane_head = _lane_head()
    rows = N_HEADS * Q_BLOCK
    tri = (_iota((Q_BLOCK, Q_BLOCK), 0) > _iota((Q_BLOCK, Q_BLOCK), 1)).astype(_BF)
    q_off = jnp.bitwise_and(_iota((rows, Q_BLOCK), 0), Q_BLOCK - 1)
    diag_past = _iota((rows, Q_BLOCK), 1) < q_off
    blocks = [g * SB_BLOCKS + n for n in range(SB_BLOCKS)]
    qs = [_stack_heads(q_ref[0, n * Q_BLOCK:(n + 1) * Q_BLOCK, :], lane_head) for n in range(SB_BLOCKS)]

    def tile(q, j, carry, acc, diagonal=False, live=None):
        start = pl.multiple_of(jnp.maximum(j, 0) * Q_BLOCK, Q_BLOCK)
        kt = k_ref[0, pl.ds(start, Q_BLOCK), :]
        vt = v_ref[0, pl.ds(start, Q_BLOCK), :]
        z = _dot_nt(q, kt)
        ls = -(jnp.maximum(z, 0.0) + jnp.log(1.0 + jnp.exp(-jnp.abs(z))))
        if diagonal:
            ls = jnp.where(diag_past, ls, 0.0)
        if live is not None:
            ls = jnp.where(live, ls, 0.0)
        excl = _dot_split(ls, tri)
        w = jnp.exp(z + ls + excl + carry)
        if diagonal:
            w = jnp.where(diag_past, w, 0.0)
        if live is not None:
            w = jnp.where(live, w, 0.0)
        acc = acc + _dot(w, vt)
        carry = carry + jnp.sum(ls, axis=1, keepdims=True)
        return carry, acc

    zero_c, zero_a = jnp.zeros((rows, 1), _F32), jnp.zeros((rows, LANES), _F32)
    state = [tile(qs[n], blocks[n], zero_c, zero_a, diagonal=True) for n in range(SB_BLOCKS)]

    def furthest(t, carries):
        m = jnp.full((), -jnp.inf, _F32)
        for n in range(SB_BLOCKS):
            m = jnp.maximum(m, jnp.where(blocks[n] - 1 - t >= 0, jnp.max(carries[n]), -jnp.inf))
        return m

    def cond(st):
        t, _, cmax = st
        return jnp.logical_and(blocks[-1] - 1 - t >= 0, cmax > _SB_LOG_CUTOFF)

    def body(st):
        t, state, _ = st
        new = []
        for n in range(SB_BLOCKS):
            j = blocks[n] - 1 - t
            new.append(tile(qs[n], j, state[n][0], state[n][1], live=j >= 0))
        return t + 1, new, furthest(t + 1, [c for c, _ in new])

    _, state, _ = lax.while_loop(cond, body, (0, state, furthest(0, [c for c, _ in state])))
    for n in range(SB_BLOCKS):
        o_ref[0, n * Q_BLOCK:(n + 1) * Q_BLOCK, :] = _unstack_heads(state[n][1], lane_head).astype(o_ref.dtype)


def _sb_attention(proj3):
    B, S, _ = proj3.shape
    span = SB_BLOCKS * Q_BLOCK
    return pl.pallas_call(
        _sb_kernel,
        grid=(B, S // span),
        in_specs=[pl.BlockSpec((1, span, LANES), lambda b, i: (b, i, G_SBQ)),
                  pl.BlockSpec((1, S, LANES), lambda b, i: (b, 0, G_SBK)),
                  pl.BlockSpec((1, S, LANES), lambda b, i: (b, 0, G_SBV))],
        out_specs=pl.BlockSpec((1, span, LANES), lambda b, i: (b, i, 0)),
        out_shape=jax.ShapeDtypeStruct((B, S, LANES), _BF),
        compiler_params=_params("parallel", "arbitrary"),
        name="sb_attn",
    )(proj3, proj3, proj3)


def _nsa_compress_kernel(c_ref, w_ref, pe_ref, kc_ref, vc_ref):
    c = c_ref[0]
    a = _dot(c, w_ref[0]) + _dot(pe_ref[0], w_ref[0])[0:1]
    b = _dot(c, w_ref[1]) + _dot(pe_ref[1], w_ref[1])[0:1]
    n = a.shape[0]
    b_next = pltpu.roll(b, n - 1, 0)
    y = a + b_next
    kc_ref[0] = y[:, :LANES].astype(kc_ref.dtype)
    vc_ref[0] = y[:, LANES:].astype(vc_ref.dtype)


def _nsa_compress(ckv3, wc, pe):
    B, nchunk, width = ckv3.shape
    return pl.pallas_call(
        _nsa_compress_kernel,
        grid=(B,),
        in_specs=[pl.BlockSpec((1, nchunk, width), lambda b: (b, 0, 0)),
                  pl.BlockSpec((2, width, 2 * LANES), lambda b: (0, 0, 0)),
                  pl.BlockSpec((2, 8, width), lambda b: (0, 0, 0))],
        out_specs=[pl.BlockSpec((1, nchunk, LANES), lambda b: (b, 0, 0)),
                   pl.BlockSpec((1, nchunk, LANES), lambda b: (b, 0, 0))],
        out_shape=[jax.ShapeDtypeStruct((B, nchunk, LANES), _BF),
                   jax.ShapeDtypeStruct((B, nchunk, LANES), _BF)],
        compiler_params=_params("parallel"),
        name="nsa_compress",
    )(ckv3, wc, pe)


def _masked_exp(z, valid):
    zm = jnp.where(valid, z, MASK_BIAS)
    m = jnp.maximum(jnp.max(zm, axis=1, keepdims=True), SCORE_FLOOR)
    p = jnp.exp(zm - m)
    return p, 1.0 / jnp.maximum(jnp.sum(p, axis=1, keepdims=True), 1e-30)


def _nsa_select_kernel(q_ref, kc_ref, vc_ref, wk_ref, wv_ref, g_ref, m_ref, ocw_ref, selm_ref,
                       ocmp_ref, imp_ref, *, seq):
    qr = q_ref.shape[1]
    q0 = pl.program_id(1) * qr
    ncp = seq // NSA_CMP_STRIDE
    nsel = m_ref.shape[1]
    win = NSA_WINDOW + qr
    lane_head = _lane_head(qr)
    qs = _stack_heads(q_ref[0], lane_head)
    head_rows = lambda a, h: a[h * qr:(h + 1) * qr]

    def compressed(width):
        cend = _iota((1, width), 1) * NSA_CMP_STRIDE + (NSA_CMP_LEN - 1)
        valid_c = (q0 + _iota((qr, width), 0)) >= cend
        rel_c = (cend - q0).astype(_F32)
        zc = _dot_nt(qs, kc_ref[0, :width, :])
        pg = jnp.zeros((qr, width), _F32)
        pcs = []
        for h in range(N_HEADS):
            p, inv = _masked_exp(head_rows(zc, h) + NSA_SLOPES[h] * rel_c, valid_c)
            p = p * inv
            pcs.append(p.astype(_BF))
            pg = pg + p
        ocmp_ref[...] = _unstack_heads(_dot(jnp.concatenate(pcs, axis=0), vc_ref[0, :width, :]), lane_head)
        imp_ref[...] = _dot_split(pg, m_ref[:width, :])

    group = min(ncp, 2 * LANES)
    n_valid = jnp.right_shift(q0 + qr - NSA_CMP_LEN, int(math.log2(NSA_CMP_STRIDE))) + 1
    groups = jnp.minimum((n_valid + group - 1) // group, ncp // group)
    for k in range(1, ncp // group + 1):
        pl.when(groups == k)(functools.partial(compressed, k * group))
    o_cmp = ocmp_ref[...]

    imp = imp_ref[...]
    blk = _iota((qr, nsel), 1)
    cur = jnp.right_shift(q0 + _iota((qr, nsel), 0), int(math.log2(NSA_SEL_LEN)))
    forced = (blk == 0) | (blk == cur) | (blk == cur - 1)
    x = jnp.where(blk > cur, -jnp.inf, jnp.where(forced, jnp.inf, imp))
    blkf = blk.astype(_F32)
    sel = jnp.zeros((qr, nsel), _F32)
    for _ in range(min(NSA_TOP_N, seq // NSA_SEL_LEN)):
        _, idx = _first_max(x, blkf, float(nsel))
        hit = blkf == idx
        sel = jnp.where(hit, 1.0, sel)
        x = jnp.where(hit, -jnp.inf, x)
    selm_ref[0] = sel.astype(selm_ref.dtype)

    ws = pl.multiple_of(jnp.maximum(q0 - NSA_WINDOW, 0), LANES)
    kw = wk_ref[0, pl.ds(ws, win), :]
    vw = wv_ref[0, pl.ds(ws, win), :]
    kpos = ws + _iota((1, win), 1)
    dw = (q0 + _iota((qr, win), 0)) - kpos
    valid_w = (dw >= 0) & (dw < NSA_WINDOW)
    rel_w = (kpos - q0).astype(_F32)
    zw = _dot_nt(qs, kw)
    pws, invs = [], []
    for h in range(N_HEADS):
        p, inv = _masked_exp(head_rows(zw, h) + NSA_SLOPES[h] * rel_w, valid_w)
        pws.append(p.astype(_BF))
        invs.append(jnp.broadcast_to(inv, (qr, LANES)))
    o_win = _unstack_heads(_dot(jnp.concatenate(pws, axis=0), vw) * jnp.concatenate(invs, axis=0), lane_head)

    g = g_ref[0]
    ocw_ref[0] = g[:, :LANES] * o_cmp + g[:, 2 * LANES:] * o_win


def _nsa_select(proj3, kc4, vc4, gate3, imp_mat):
    B, S, _ = proj3.shape
    ncp = S // NSA_CMP_STRIDE
    nsel = imp_mat.shape[1]
    qr = NSA_SELECT_ROWS
    assert S % qr == 0 and S >= NSA_WINDOW + qr
    return pl.pallas_call(
        functools.partial(_nsa_select_kernel, seq=S),
        grid=(B, S // qr),
        in_specs=[pl.BlockSpec((1, qr, LANES), lambda b, i: (b, i, G_NQ)),
                  pl.BlockSpec((1, ncp, LANES), lambda b, i: (b, 0, 0)),
                  pl.BlockSpec((1, ncp, LANES), lambda b, i: (b, 0, 0)),
                  pl.BlockSpec((1, S, LANES), lambda b, i: (b, 0, G_WK)),
                  pl.BlockSpec((1, S, LANES), lambda b, i: (b, 0, G_WV)),
                  pl.BlockSpec((1, qr, 3 * LANES), lambda b, i: (b, i, 0)),
                  pl.BlockSpec((ncp, nsel), lambda b, i: (0, 0))],
        out_specs=[pl.BlockSpec((1, qr, LANES), lambda b, i: (b, i, 0)),
                   pl.BlockSpec((1, qr, nsel), lambda b, i: (b, i, 0))],
        out_shape=[jax.ShapeDtypeStruct((B, S, LANES), _F32),
                   jax.ShapeDtypeStruct((B, S, nsel), _BF)],
        scratch_shapes=[pltpu.VMEM((qr, LANES), _F32), pltpu.VMEM((qr, nsel), _F32)],
        compiler_params=_params("parallel", "arbitrary"),
        name="nsa_select",
    )(proj3, kc4, vc4, proj3, proj3, gate3, imp_mat)


def _moba_mean_kernel(k_ref, o_ref, *, nb):
    k = k_ref[0].astype(_F32)
    o_ref[0] = jnp.zeros(o_ref.shape[1:], _F32)
    o_ref[0, :nb, :] = jnp.mean(k.reshape(nb, MOBA_BLOCK, LANES), axis=1)


def _moba_mean(proj3):
    B, S, _ = proj3.shape
    nb = S // MOBA_BLOCK
    return pl.pallas_call(
        functools.partial(_moba_mean_kernel, nb=nb),
        grid=(B,),
        in_specs=[pl.BlockSpec((1, S, LANES), lambda b: (b, 0, G_MK))],
        out_specs=pl.BlockSpec((1, LANES, LANES), lambda b: (b, 0, 0)),
        out_shape=jax.ShapeDtypeStruct((B, LANES, LANES), _F32),
        compiler_params=_params("parallel"),
        name="moba_mean",
    )(proj3)


def _moba_select_kernel(q_ref, km_ref, selm_ref, *, nb):
    rows = q_ref.shape[1]
    q0 = pl.program_id(1) * rows
    sg = _dot_nt(q_ref[0], km_ref[0])
    blk = _iota((rows, LANES), 1)
    cur = jnp.right_shift(q0 + _iota((rows, LANES), 0), int(math.log2(MOBA_BLOCK)))
    x = jnp.where(blk < cur, sg, -jnp.inf)
    blkf = blk.astype(_F32)
    sel = jnp.where(blk == cur, 1.0, 0.0)
    for _ in range(min(MOBA_TOPK, nb)):
        m, idx = _first_max(x, blkf, float(LANES))
        hit = blkf == idx
        sel = jnp.where(hit & (m > -jnp.inf), 1.0, sel)
        x = jnp.where(hit, -jnp.inf, x)
    selm_ref[0] = sel.astype(selm_ref.dtype)


def _moba_select(proj3, kmean):
    B, S, _ = proj3.shape
    nb = S // MOBA_BLOCK
    rows = MOBA_SELECT_ROWS
    assert nb <= LANES and S % rows == 0
    return pl.pallas_call(
        functools.partial(_moba_select_kernel, nb=nb),
        grid=(B, S // rows),
        in_specs=[pl.BlockSpec((1, rows, LANES), lambda b, i: (b, i, G_MQ)),
                  pl.BlockSpec((1, LANES, LANES), lambda b, i: (b, 0, 0))],
        out_specs=pl.BlockSpec((1, rows, LANES), lambda b, i: (b, i, 0)),
        out_shape=jax.ShapeDtypeStruct((B, S, LANES), _BF),
        compiler_params=_params("parallel", "arbitrary"),
        name="moba_select",
    )(proj3, kmean)


def _blk_attn_kernel(*refs, blk_len, tile, qb, slopes, nblk, seq, gated):
    if gated:
        q_ref, kx_ref, v_ref, selm_ref, g_ref, add_ref, o_ref = refs[:7]
    else:
        q_ref, kx_ref, v_ref, selm_ref, o_ref = refs[:5]
    qx_ref, m_ref, acc_ref, za_ref, zb_ref, flag_ref, list_ref = refs[-7:]
    rows = N_HEADS * qb
    bpt = tile // blk_len
    max_tiles = seq // tile
    i = pl.program_id(1)
    q0 = i * qb
    lane_head = _lane_head(qb)
    lane = _iota((qb, LANES), 1)

    qx_ref[:, :LANES] = _stack_heads(q_ref[0], lane_head)
    m_ref[...] = jnp.full((rows, LANES), SCORE_FLOOR, _F32)
    acc_ref[...] = jnp.zeros((rows, LANES), _F32)
    lane1 = _iota((1, LANES), 1)
    fixed_lanes = [jnp.where(lane1 == F_HI, slopes[h] * F_SPLIT, jnp.where(lane1 == F_LO, slopes[h], 0.0))
                   for h in range(N_HEADS)]
    dist_lanes = [jnp.where(lane1 == F_ONE, slopes[h] * LANES, 0.0) for h in range(N_HEADS)]

    selm = selm_ref[0].astype(_F32)
    sneg = (1.0 - selm) * MASK_BIAS
    n_tiles = jnp.right_shift(q0 + qb + tile - 1, int(math.log2(tile)))

    def scores(j, z_ref):
        k0 = pl.multiple_of(j * tile, tile)
        first = j * bpt
        if nblk > LANES:
            half = jnp.right_shift(first, int(math.log2(LANES)))
            base = sneg[:, :LANES]
            for c in range(1, nblk // LANES):
                base = jnp.where(half == c, sneg[:, c * LANES:(c + 1) * LANES], base)
        else:
            base = sneg
        shift = jnp.bitwise_and(LANES - jnp.bitwise_and(first, LANES - 1), LANES - 1)
        tile_mask = pltpu.roll(base, shift, 1)
        tile_mask = jnp.where(lane >= F_HI, 0.0, tile_mask)
        dist = (j * (tile // LANES) - i * (qb // LANES)).astype(_F32)
        for h in range(N_HEADS):
            ext = tile_mask + (fixed_lanes[h] + dist * dist_lanes[h])
            qx_ref[h * qb:(h + 1) * qb, LANES:] = ext.astype(_BF)
        kx = kx_ref[0, pl.ds(k0, tile), :]
        z_ref[...] = lax.dot_general(qx_ref[...], kx, (((1,), (1,)), ((), ())),
                                     preferred_element_type=_F32)

    def absorb(z_ref, j, diagonal):
        k0 = pl.multiple_of(j * tile, tile)
        vt = v_ref[0, pl.ds(k0, tile), :]
        z = z_ref[...]
        if diagonal:
            q_off = jnp.bitwise_and(_iota((rows, tile), 0), qb - 1)
            z = jnp.where(_iota((rows, tile), 1) - q_off <= q0 - k0, z, MASK_BIAS)
        m_old = m_ref[...]
        m_new = jnp.maximum(m_old, jnp.max(z, axis=1, keepdims=True))
        p = jnp.exp(z - jnp.concatenate([m_new] * (tile // LANES), axis=1))
        a = jnp.exp(m_old - m_new)
        acc_ref[...] = a * acc_ref[...] + _dot(p, vt)
        m_ref[...] = m_new

    scores(n_tiles - 1, za_ref)

    any_q = jnp.broadcast_to(jnp.max(selm, axis=0, keepdims=True), (8, nblk))
    group = (jnp.right_shift(_iota((nblk, LANES), 0), int(math.log2(bpt))) == _iota((nblk, LANES), 1))
    tile_hits = _dot(any_q, jnp.where(group, 1.0, 0.0))
    for t in range(max_tiles):
        flag_ref[t] = (tile_hits[0, t] > 0.5).astype(jnp.int32)

    def scan(t, n):
        @pl.when(flag_ref[t] > 0)
        def _():
            list_ref[n] = t
        return n + (flag_ref[t] > 0).astype(jnp.int32)

    n_list = lax.fori_loop(0, n_tiles - 1, scan, 0)

    @pl.when(n_list == 0)
    def _():
        absorb(za_ref, n_tiles - 1, True)

    @pl.when(n_list > 0)
    def _():
        scores(list_ref[0], zb_ref)
        absorb(za_ref, n_tiles - 1, True)

    def body(kk, c):
        scores(list_ref[2 * kk + 1], za_ref)
        absorb(zb_ref, list_ref[2 * kk], False)
        scores(list_ref[2 * kk + 2], zb_ref)
        absorb(za_ref, list_ref[2 * kk + 1], False)
        return c

    n_pairs = jnp.right_shift(n_list - 1, 1)
    lax.fori_loop(0, n_pairs, body, 0)
    left = n_list - 1 - 2 * n_pairs

    @pl.when(jnp.logical_and(n_list > 0, left == 0))
    def _():
        absorb(zb_ref, list_ref[n_list - 1], False)

    @pl.when(jnp.logical_and(n_list > 0, left == 1))
    def _():
        scores(list_ref[n_list - 1], za_ref)
        absorb(zb_ref, list_ref[n_list - 2], False)
        absorb(za_ref, list_ref[n_list - 1], False)

    acc = acc_ref[...]
    total = pltpu.roll(acc, LANES - HEAD_DIM, 1)
    o = acc * (1.0 / jnp.maximum(total, 1e-30))
    out = jnp.zeros((qb, LANES), _F32)
    for h in range(N_HEADS):
        o_h = o[h * qb:(h + 1) * qb]
        if h:
            o_h = pltpu.roll(o_h, h * HEAD_DIM, 1)
        out = out + jnp.where(lane_head == h, o_h, 0.0)
    if gated:
        out = add_ref[0] + g_ref[0] * out
    o_ref[0] = out.astype(o_ref.dtype)


def _blk_attention(proj3, selm, gq, gkx, gv, blk_len, tile, qb, slopes, gate3=None, addend=None):
    B, S, _ = proj3.shape
    nblk = selm.shape[-1]
    assert S % tile == 0 and gkx % 2 == 0 and nblk % LANES == 0
    gated = gate3 is not None
    rows = N_HEADS * qb
    in_specs = [pl.BlockSpec((1, qb, LANES), lambda b, i: (b, i, gq)),
                pl.BlockSpec((1, S, 2 * LANES), lambda b, i: (b, 0, gkx // 2)),
                pl.BlockSpec((1, S, LANES), lambda b, i: (b, 0, gv)),
                pl.BlockSpec((1, qb, nblk), lambda b, i: (b, i, 0))]
    args = [proj3, proj3, proj3, selm]
    if gated:
        in_specs += [pl.BlockSpec((1, qb, LANES), lambda b, i: (b, i, 1)),
                     pl.BlockSpec((1, qb, LANES), lambda b, i: (b, i, 0))]
        args += [gate3, addend]
    return pl.pallas_call(
        functools.partial(_blk_attn_kernel, blk_len=blk_len, tile=tile, qb=qb, slopes=slopes, nblk=nblk, seq=S,
                          gated=gated),
        grid=(B, S // qb),
        in_specs=in_specs,
        out_specs=pl.BlockSpec((1, qb, LANES), lambda b, i: (b, i, 0)),
        out_shape=jax.ShapeDtypeStruct((B, S, LANES), _BF),
        scratch_shapes=[pltpu.VMEM((rows, 2 * LANES), _BF),
                        pltpu.VMEM((rows, LANES), _F32),
                        pltpu.VMEM((rows, LANES), _F32),
                        pltpu.VMEM((rows, tile), _F32),
                        pltpu.VMEM((rows, tile), _F32),
                        pltpu.SMEM((S // tile,), jnp.int32),
                        pltpu.SMEM((S // tile,), jnp.int32)],
        compiler_params=_params("parallel", "arbitrary"),
        name="blk_attn_%d" % blk_len,
    )(*args)


def _merge_kernel(x_ref, osb_ref, onsa_ref, omb_ref, wg_ref, wbr_ref, wo_ref, lg_ref, lb_ref, o_ref,
                  *, alpha):
    x = x_ref[...]
    d = x.shape[1]
    gates = jax.nn.sigmoid(_dot(x, wg_ref[...]))
    mix = (gates[:, :d] * _dot(osb_ref[...], wbr_ref[0])
           + gates[:, d:2 * d] * _dot(onsa_ref[...], wbr_ref[1])
           + gates[:, 2 * d:] * _dot(omb_ref[...], wbr_ref[2]))
    y = alpha * x + _dot(mix, wo_ref[...])
    o_ref[...] = _layer_norm(y, lg_ref[...], lb_ref[...])


def _merge(x2, o_sb, o_nsa, o_mb, w_gate, w_br, w_out, ln_g, ln_b, alpha, tm=256):
    T, D = x2.shape
    row = lambda i: (i, 0)
    fixed2 = lambda i: (0, 0)
    return pl.pallas_call(
        functools.partial(_merge_kernel, alpha=alpha),
        grid=(T // tm,),
        in_specs=[pl.BlockSpec((tm, D), row),
                  pl.BlockSpec((tm, LANES), row),
                  pl.BlockSpec((tm, LANES), row),
                  pl.BlockSpec((tm, LANES), row),
                  pl.BlockSpec((D, N_BRANCHES * D), fixed2),
                  pl.BlockSpec((N_BRANCHES, LANES, D), lambda i: (0, 0, 0)),
                  pl.BlockSpec((D, D), fixed2),
                  pl.BlockSpec((1, D), fixed2),
                  pl.BlockSpec((1, D), fixed2)],
        out_specs=pl.BlockSpec((tm, D), row),
        out_shape=jax.ShapeDtypeStruct((T, D), _F32),
        compiler_params=_params("parallel"),
        name="merge",
    )(x2, o_sb, o_nsa, o_mb, w_gate, w_br, w_out, ln_g, ln_b)


_NOT_RETRIEVED = 99.0


def _top_rows(s, k, exact):
    n = s.shape[0]
    rows = _iota(s.shape, 0).astype(_F32)
    rank = jnp.full(s.shape, _NOT_RETRIEVED, _F32)
    vals = []
    for p in range(k):
        m = jnp.max(s, axis=0, keepdims=True)
        hit = s == m
        if exact:
            hit = rows == jnp.min(jnp.where(hit, rows, float(n)), axis=0, keepdims=True)
        vals.append(m)
        rank = jnp.where(hit, float(p), rank)
        s = jnp.where(hit, -jnp.inf, s)
    return vals, rank


def _peer_route_kernel(x_ref, wq_ref, k1_ref, k2_ref, fa_ref, qb_ref, e1_ref, e2_ref):
    half = PEER_QDIM // 2
    qf = _dot(x_ref[...], wq_ref[...]).astype(_BF)
    tm = qf.shape[0]
    rows_k = _iota((PEER_TOPK, tm), 0)
    ncand = PEER_TOPK * PEER_TOPK
    pos = _iota((ncand, tm), 0).astype(_F32)

    def route(exact):
        most = jnp.zeros((1, tm), _F32)
        for h in range(PEER_HEADS):
            s1 = _dot_nt(k1_ref[h], qf[:, h * PEER_QDIM:h * PEER_QDIM + half])
            s2 = _dot_nt(k2_ref[h], qf[:, h * PEER_QDIM + half:(h + 1) * PEER_QDIM])
            v1, r1 = _top_rows(s1, PEER_TOPK, exact)
            v2, r2 = _top_rows(s2, PEER_TOPK, exact)
            for r in (r1, r2):
                most = jnp.maximum(most, jnp.sum(jnp.where(r < _NOT_RETRIEVED, 1.0, 0.0), axis=0, keepdims=True))
            v2_all = jnp.zeros((PEER_TOPK, tm), _F32)
            for qi in range(PEER_TOPK):
                v2_all = jnp.where(rows_k == qi, v2[qi], v2_all)
            cand = jnp.concatenate([v1[p] + v2_all for p in range(PEER_TOPK)], axis=0)
            c = cand
            pickf = jnp.zeros(cand.shape, _F32)
            for _ in range(PEER_TOPK):
                m = jnp.max(c, axis=0, keepdims=True)
                idx = jnp.min(jnp.where(c == m, pos, float(ncand)), axis=0, keepdims=True)
                hit = pos == idx
                pickf = jnp.where(hit, 1.0, pickf)
                c = jnp.where(hit, -jnp.inf, c)
            cmax = v1[0] + v2[0]
            z = jnp.sum(pickf * jnp.exp(cand - cmax), axis=0, keepdims=True)
            fa = jnp.zeros((PEER_NKEYS, tm), _F32)
            for p in range(PEER_TOPK):
                count = jnp.sum(pickf[p * PEER_TOPK:(p + 1) * PEER_TOPK], axis=0, keepdims=True)
                fa = jnp.where(r1 == float(p), count, fa)
            fa_ref[h] = fa
            qb_ref[h] = r2.astype(qb_ref.dtype)
            e1_ref[h] = jnp.exp(s1 - v1[0]) * (1.0 / z)
            e2_ref[h] = jnp.exp(s2 - v2[0]).astype(e2_ref.dtype)
        return jnp.max(most)

    @pl.when(route(False) > PEER_TOPK)
    def _():
        route(True)


def _peer_route(x2, wq, k1, k2, tm=256):
    T, D = x2.shape
    tab = jax.ShapeDtypeStruct((PEER_HEADS, PEER_NKEYS, T), _F32)
    tab_bf = jax.ShapeDtypeStruct((PEER_HEADS, PEER_NKEYS, T), _BF)
    tab_spec = pl.BlockSpec((PEER_HEADS, PEER_NKEYS, tm), lambda i: (0, 0, i))
    return pl.pallas_call(
        _peer_route_kernel,
        grid=(T // tm,),
        in_specs=[pl.BlockSpec((tm, D), lambda i: (i, 0)),
                  pl.BlockSpec(wq.shape, lambda i: (0, 0)),
                  pl.BlockSpec(k1.shape, lambda i: (0, 0, 0)),
                  pl.BlockSpec(k2.shape, lambda i: (0, 0, 0))],
        out_specs=[tab_spec] * 4,
        out_shape=[tab, tab_bf, tab, tab_bf],
        compiler_params=_params("parallel"),
        name="peer_route",
    )(x2, wq, k1, k2)


def _gelu_tanh(s):
    c1 = math.sqrt(2.0 / math.pi)
    k1 = jnp.asarray(c1, s.dtype)
    k2 = jnp.asarray(c1 * 0.044715, s.dtype)
    inner = s * (k1 + k2 * (s * s))
    half = jnp.asarray(0.5, s.dtype) * s
    return half + half * jnp.tanh(inner)


def _peer_dense_kernel(x_ref, xr_ref, u_ref, vt_ref, fa_ref, qb_ref, e1_ref, e2_ref, lg_ref, lb_ref, o_ref,
                       acc_ref, xb_ref, s_ref, c_ref, *, alpha, te, n_tiles, n_work):
    s_idx = pl.program_id(0)
    j_score = s_idx % n_tiles
    j_gate = jnp.maximum(s_idx - 1, 0) % n_tiles
    j_value = jnp.maximum(s_idx - 2, 0) % n_tiles

    @pl.when(s_idx == 0)
    def _():
        s_ref[...] = jnp.zeros_like(s_ref)
        c_ref[...] = jnp.zeros_like(c_ref)
        acc_ref[...] = jnp.zeros_like(acc_ref)

    @pl.when(jnp.logical_and(j_score == 0, s_idx < n_work))
    def _():
        xb_ref[...] = x_ref[...].astype(_BF)

    fresh = j_value == 0
    cur = s_idx % 2
    tm = s_ref.shape[2]
    tc = tm // 2
    for ck in range(tm // tc):
        cols = slice(ck * tc, (ck + 1) * tc)
        acc_ref[:, cols] = jnp.where(fresh, 0.0, acc_ref[:, cols]) + jnp.dot(
            vt_ref[...], c_ref[1 - cur, :, cols], preferred_element_type=_F32)
        s = s_ref[1 - cur, :, cols]
        act = _gelu_tanh(s)
        for r in range(te // PEER_NKEYS):
            a = j_gate * (te // PEER_NKEYS) + r
            gate = jnp.zeros((PEER_NKEYS, tc), _BF)
            for h in range(PEER_HEADS):
                fa = fa_ref[h, pl.ds(a, 1), cols].astype(_BF)
                e1 = e1_ref[h, pl.ds(a, 1), cols].astype(_BF)
                gate = gate + e1 * jnp.where(qb_ref[h, :, cols] < fa, e2_ref[h, :, cols], jnp.zeros((), _BF))
            c_ref[cur, r * PEER_NKEYS:(r + 1) * PEER_NKEYS, cols] = (
                gate * act[r * PEER_NKEYS:(r + 1) * PEER_NKEYS])
        s_ref[cur, :, cols] = _dot_nt(u_ref[...], xb_ref[cols, :]).astype(_BF)

    @pl.when(jnp.logical_and(j_value == n_tiles - 1, s_idx >= 2))
    def _():
        y = alpha * xr_ref[...] + acc_ref[...].T
        o_ref[...] = _layer_norm(y, lg_ref[...], lb_ref[...])


def _peer_dense(x2, u_all, vt_all, layer, tabs, ln_g, ln_b, alpha, tm=512):
    T, D = x2.shape
    n_tiles, te = vt_all.shape[1], vt_all.shape[3]
    n_tok = T // tm
    n_work = n_tok * n_tiles
    last = n_work - 1
    tok = lambda lag: (lambda s: (jnp.clip(s - lag, 0, last) // n_tiles, 0))
    exp_tile = lambda lag: (lambda s: jnp.clip(s - lag, 0, last) % n_tiles)
    tab_spec = pl.BlockSpec((PEER_HEADS, PEER_NKEYS, tm), lambda s: (0, 0, jnp.clip(s - 1, 0, last) // n_tiles))
    return pl.pallas_call(
        functools.partial(_peer_dense_kernel, alpha=alpha, te=te, n_tiles=n_tiles, n_work=n_work),
        grid=(n_work + 2,),
        in_specs=[pl.BlockSpec((tm, D), tok(0)),
                  pl.BlockSpec((tm, D), tok(2)),
                  pl.BlockSpec((None, te, D), lambda s: (layer, exp_tile(0)(s), 0)),
                  pl.BlockSpec((None, None, D, te), lambda s: (layer, exp_tile(2)(s), 0, 0)),
                  tab_spec, tab_spec, tab_spec, tab_spec,
                  pl.BlockSpec((1, D), lambda s: (0, 0)),
                  pl.BlockSpec((1, D), lambda s: (0, 0))],
        out_specs=pl.BlockSpec((tm, D), tok(2)),
        out_shape=jax.ShapeDtypeStruct((T, D), _F32),
        scratch_shapes=[pltpu.VMEM((D, tm), _F32), pltpu.VMEM((tm, D), _BF),
                        pltpu.VMEM((2, te, tm), _BF), pltpu.VMEM((2, te, tm), _BF)],
        compiler_params=_params("arbitrary"),
        name="peer_dense",
    )(x2, x2, u_all, vt_all, *tabs, ln_g, ln_b)


def _in_widths(d_model):
    w = N_HEADS * HEAD_DIM
    return (w, w, w, w) + (HEAD_DIM,) * 6 + (3 * N_HEADS, w, HEAD_DIM, HEAD_DIM, N_BRANCHES * d_model)


def _arrange_w_in(w_in, d_model):
    off = np.concatenate([[0], np.cumsum(_in_widths(d_model))])
    scale = HEAD_DIM ** -0.5
    seg = lambda n: w_in[:, off[n]:off[n + 1]]
    rep = lambda n: jnp.tile(seg(n), (1, N_HEADS))
    gate_cols = np.array([off[10] + h * 3 + c for c in range(3) for h in range(N_HEADS)
                          for _ in range(HEAD_DIM)])
    ckv = jnp.concatenate([seg(4), seg(5), jnp.zeros((w_in.shape[0], LANES - 2 * HEAD_DIM), w_in.dtype)], axis=1)
    blank = jnp.zeros((w_in.shape[0], LANES), w_in.dtype)
    once = lambda n: jnp.concatenate([seg(n), blank[:, HEAD_DIM:]], axis=1)
    groups = [seg(0) * scale, seg(1), seg(2), seg(3) * scale, rep(6), blank, rep(12), blank,
              once(7), rep(8), rep(9), seg(11) * scale, once(13), ckv, w_in[:, gate_cols]]
    return jnp.concatenate(groups, axis=1).astype(_BF), seg(14).astype(_BF)


def _arrange_compress(w_ck, w_cv, pe_k, pe_v):
    half = NSA_CMP_STRIDE
    wk = jnp.tile(w_ck.reshape(2, half, HEAD_DIM, HEAD_DIM), (1, 1, 1, N_HEADS))
    wv = jnp.tile(w_cv.reshape(2, half, HEAD_DIM, HEAD_DIM), (1, 1, 1, N_HEADS))
    w = jnp.zeros((2, half, LANES, 2 * LANES), _F32)
    w = w.at[:, :, :HEAD_DIM, :LANES].set(wk)
    w = w.at[:, :, HEAD_DIM:2 * HEAD_DIM, LANES:].set(wv)
    pe = jnp.zeros((2, half, LANES), _F32)
    pe = pe.at[:, :, :HEAD_DIM].set(pe_k.reshape(2, half, HEAD_DIM))
    pe = pe.at[:, :, HEAD_DIM:2 * HEAD_DIM].set(pe_v.reshape(2, half, HEAD_DIM))
    pe = jnp.broadcast_to(pe.reshape(2, 1, half * LANES), (2, 8, half * LANES))
    return w.reshape(2, half * LANES, 2 * LANES).astype(_BF), pe.astype(_BF)


def _key_features():
    out = np.zeros((2, KEY_TILE, LANES), np.float32)
    for n, (blk_len, tile) in enumerate(((NSA_SEL_LEN, NSA_KEY_TILE), (MOBA_BLOCK, MOBA_KEY_TILE))):
        c = np.arange(KEY_TILE) % tile
        out[n, np.arange(KEY_TILE), c // blk_len] = 1.0
        out[n, :, F_HI] = c // F_SPLIT
        out[n, :, F_LO] = c % F_SPLIT
        out[n, :, F_ONE] = 1.0
    return jnp.asarray(out, _BF)


def _importance_matrix(seq):
    ncp = seq // NSA_CMP_STRIDE
    nsel = seq // NSA_SEL_LEN
    ratio = NSA_SEL_LEN // NSA_CMP_STRIDE
    overlap = np.convolve(np.ones(ratio), np.ones(NSA_CMP_LEN // NSA_CMP_STRIDE))
    n_left = (NSA_CMP_LEN - NSA_CMP_STRIDE) // NSA_CMP_STRIDE
    m = np.zeros((ncp, -(-nsel // LANES) * LANES), np.float32)
    for j in range(nsel):
        for o, c in enumerate(overlap):
            n = ratio * j + o - n_left
            if 0 <= n < ncp - 1:
                m[n, j] += c
    return jnp.asarray(m, _BF)


def kernel(x, w_in, nsa_pe_k, nsa_pe_v, nsa_w_ck, nsa_w_cv, w_br_sb, w_br_nsa, w_br_moba, w_out, ln1_g, ln1_b, peer_wq, peer_k1, peer_k2, peer_u, peer_v, ln2_g, ln2_b):
    B, S, D = x.shape
    depth = w_in.shape[0]
    T = B * S
    assert S % MOBA_BLOCK == 0 and S >= NSA_WINDOW + Q_BLOCK
    alpha = (2.0 * depth) ** 0.25
    imp_mat = _importance_matrix(S)
    feats = _key_features()
    u_all = peer_u.astype(_BF)
    te = PEER_EXPERT_TILE
    vt_all = jnp.swapaxes(peer_v.reshape(depth, -1, te, D), 2, 3).astype(_BF)
    x2 = x.reshape(T, D)
    for l in range(depth):
        w_small, w_gate = _arrange_w_in(w_in[l], D)
        wc, pe = _arrange_compress(nsa_w_ck[l], nsa_w_cv[l], nsa_pe_k[l], nsa_pe_v[l])
        proj, ckv, ngate = _inproj(x2, w_small, feats)
        proj3 = proj.reshape(B, S, N_PROJ_GROUPS * LANES)
        gate3 = ngate.reshape(B, S, 3 * LANES)

        o_sb = _sb_attention(proj3)

        kc4, vc4 = _nsa_compress(ckv.reshape(B, S // NSA_CMP_STRIDE, NSA_CMP_STRIDE * LANES), wc, pe)
        o_cw, nsa_selm = _nsa_select(proj3, kc4, vc4, gate3, imp_mat)
        o_nsa = _blk_attention(proj3, nsa_selm, G_NQ, G_SK, G_SV, NSA_SEL_LEN, NSA_KEY_TILE, NSA_ATTN_ROWS, NSA_SLOPES,
                               gate3=gate3, addend=o_cw)

        mb_selm = _moba_select(proj3, _moba_mean(proj3))
        o_mb = _blk_attention(proj3, mb_selm, G_MQ, G_MK, G_MV, MOBA_BLOCK, MOBA_KEY_TILE, MOBA_Q_ROWS, MOBA_SLOPES)

        w_br = jnp.stack([w_br_sb[l], w_br_nsa[l], w_br_moba[l]]).astype(_BF)
        x2 = _merge(x2, o_sb.reshape(T, LANES), o_nsa.reshape(T, LANES), o_mb.reshape(T, LANES),
                    w_gate, w_br, w_out[l].astype(_BF), ln1_g[l].reshape(1, D), ln1_b[l].reshape(1, D), alpha)

        wq = peer_wq[l].reshape(D, PEER_HEADS * PEER_QDIM).astype(_BF)
        tabs = _peer_route(x2, wq, peer_k1[l].astype(_BF), peer_k2[l].astype(_BF))
        x2 = _peer_dense(x2, u_all, vt_all, l, tabs, ln2_g[l].reshape(1, D), ln2_b[l].reshape(1, D), alpha)
    return x2.reshape(B, S, D)
```
